```python
import jax, jax.numpy as jnp
from jax import lax
import numpy as np

D_MODEL = 1024
BATCH = 16
SEQ = 256
DEPTH = 4
DEC_BATCH = 4
DEC_SEQ = 4096
PAST_LEN = 512

GRID_W = 64
HGRN_HEADS = 8
HGRN_DK = 128
HGRN_DV = 128
HGRN_WIDTH = HGRN_HEADS * HGRN_DK
HGRN_CHUNK = 64
ATTN_HEADS = 16
KV_HEADS = 4
HEAD_DIM = 64
ATTN_WIDTH = ATTN_HEADS * HEAD_DIM
KV_WIDTH = KV_HEADS * HEAD_DIM
Q_BLOCK = 128
ATTN_SCALE = HEAD_DIM ** -0.5
ROPE_THETA = 10000.0
PEER_HEADS = 8
PEER_KEY_DIM = 256
PEER_HALF = PEER_KEY_DIM // 2
N_KEYS = 128
N_EXPERTS = N_KEYS * N_KEYS
PEER_TOPK = 16
PEER_BLOCK = 128
DEEPNORM_ALPHA = (2 * DEPTH) ** 0.25
DEEPNORM_BETA = (8 * DEPTH) ** -0.25
EPS = 1e-6
IN_SIZES = (HGRN_WIDTH, HGRN_WIDTH, HGRN_WIDTH, HGRN_WIDTH, HGRN_WIDTH, ATTN_WIDTH, KV_WIDTH, KV_WIDTH, D_MODEL, D_MODEL)
IN_COLS = sum(IN_SIZES)
IN_OFFSETS = tuple(np.cumsum(IN_SIZES)[:-1].tolist())

kernel_name = 'hybrid_hgrn2_gqa_peer_diffusion_step'


def layer_norm(x, g, b):
    xf = x.astype(jnp.float32)
    mu = jnp.mean(xf, axis=-1, keepdims=True)
    var = jnp.mean(jnp.square(xf - mu), axis=-1, keepdims=True)
    return ((xf - mu) * lax.rsqrt(var + EPS)).astype(x.dtype) * g + b


def rms_norm(x, g):
    xf = x.astype(jnp.float32)
    return (xf * lax.rsqrt(jnp.mean(xf * xf, axis=-1, keepdims=True) + EPS)).astype(g.dtype) * g


def axial_rope(x, rows):
    L = x.shape[1]
    row = jnp.broadcast_to(jnp.arange(rows, dtype=jnp.float32)[:, None], (rows, GRID_W)).reshape(L)
    col = jnp.broadcast_to(jnp.arange(GRID_W, dtype=jnp.float32)[None, :], (rows, GRID_W)).reshape(L)
    n_freq = HEAD_DIM // 4
    inv = jnp.power(ROPE_THETA, -jnp.arange(n_freq, dtype=jnp.float32) / n_freq)
    ang = jnp.concatenate([row[:, None] * inv, col[:, None] * inv], axis=-1)
    cos = jnp.cos(ang)[None, :, None, :].astype(x.dtype)
    sin = jnp.sin(ang)[None, :, None, :].astype(x.dtype)
    x1, x2 = x[..., :HEAD_DIM // 2], x[..., HEAD_DIM // 2:]
    return jnp.concatenate([x1 * cos - x2 * sin, x1 * sin + x2 * cos], axis=-1)


def gla_chunk_scan(q, k, v, logf, s0):
    B, L, H, _ = q.shape
    n = L // HGRN_CHUNK

    def chunks(t):
        return jnp.moveaxis(t.astype(jnp.float32).reshape(B, n, HGRN_CHUNK, H, t.shape[-1]), 1, 0)

    mask = jnp.tril(jnp.ones((HGRN_CHUNK, HGRN_CHUNK), dtype=bool))[None, :, :, None, None]

    def step(S, inp):
        qc, kc, vc, gc = inp
        b = jnp.cumsum(gc, axis=1)
        decay = jnp.exp(jnp.where(mask, b[:, :, None] - b[:, None, :], -jnp.inf))
        scores = jnp.einsum('bihd,bjhd,bijhd->bhij', qc, kc, decay)
        o = (jnp.einsum('bhij,bjhv->bihv', scores, vc)
             + jnp.einsum('bihd,bhdv->bihv', qc * jnp.exp(b), S))
        b_last = b[:, -1]
        S = (jnp.exp(b_last)[..., None] * S
             + jnp.einsum('bjhd,bjhv->bhdv', kc * jnp.exp(b_last[:, None] - b), vc))
        return S, o

    S, o = lax.scan(step, s0.astype(jnp.float32), (chunks(q), chunks(k), chunks(v), chunks(logf)))
    o = jnp.moveaxis(o, 0, 1).reshape(B, L, H, v.shape[-1])
    return o, S


def hgrn2_branch(hq, hf_fwd, hf_bwd, hi, hg, lb_fwd, lb_bwd, norm_g, s0_fwd, s0_bwd):
    B, L, _ = hq.shape

    def heads(t):
        return t.reshape(B, L, HGRN_HEADS, -1)

    q = heads(jax.nn.silu(hq))
    v = heads(hi)

    def gates(z, lb):
        z = heads(z).astype(jnp.float32)
        lb = lb.reshape(HGRN_HEADS, HGRN_DK)
        f = lb + (1.0 - lb) * jax.nn.sigmoid(z)
        k = (1.0 - lb) * jax.nn.sigmoid(-z)
        return jnp.log(f), k

    logf_f, k_f = gates(hf_fwd, lb_fwd)
    logf_b, k_b = gates(hf_bwd, lb_bwd)
    o_f, s_f = gla_chunk_scan(q, k_f, v, logf_f, s0_fwd)

    def flip(t):
        return jnp.flip(t, axis=1)

    o_b, s_b = gla_chunk_scan(flip(q), flip(k_b), flip(v), flip(logf_b), s0_bwd)
    o = (o_f + flip(o_b)).astype(hq.dtype)
    o = rms_norm(o, norm_g) * heads(jax.nn.silu(hg))
    return o.reshape(B, L, HGRN_WIDTH), s_f, s_b


def attend(q, k, v):
    B, L, H, hd = q.shape
    nb = L // Q_BLOCK
    G = H // KV_HEADS
    qb = jnp.moveaxis(q.reshape(B, nb, Q_BLOCK, KV_HEADS, G, hd), 1, 0)

    def block(qi):
        s = jnp.einsum('bqkgd,bskd->bkgqs', qi, k).astype(jnp.float32) * ATTN_SCALE
        p = jax.nn.softmax(s, axis=-1).astype(v.dtype)
        return jnp.einsum('bkgqs,bskd->bqkgd', p, v)

    o = lax.map(block, qb)
    return jnp.moveaxis(o, 0, 1).reshape(B, L, H * hd)


def peer_mixer(h, w_query, sub_keys, u_tab, v_tab):
    B, L, D = h.shape
    T = B * L
    x = h.reshape(T, D)
    q = (x @ w_query).reshape(T, PEER_HEADS, 2, PEER_HALF)
    s = jnp.einsum('thpd,hpnd->thpn', q, sub_keys).astype(jnp.float32)
    s_top, i_top = lax.top_k(s, PEER_TOPK)
    cand_s = (s_top[:, :, 0, :, None] + s_top[:, :, 1, None, :]).reshape(T, PEER_HEADS, PEER_TOPK * PEER_TOPK)
    cand_i = (i_top[:, :, 0, :, None] * N_KEYS + i_top[:, :, 1, None, :]).reshape(T, PEER_HEADS, PEER_TOPK * PEER_TOPK)
    best_s, best_pos = lax.top_k(cand_s, PEER_TOPK)
    expert = jnp.take_along_axis(cand_i, best_pos, axis=-1)
    gate = jax.nn.softmax(best_s, axis=-1).astype(h.dtype)
    nb = T // PEER_BLOCK

    def block(args):
        xb, eb, gb = args
        a = jax.nn.gelu(jnp.einsum('td,thkd->thk', xb, u_tab[eb]), approximate=False)
        return jnp.einsum('thk,thkd->td', gb * a, v_tab[eb])

    out = lax.map(block, (x.reshape(nb, PEER_BLOCK, D),
                          expert.reshape(nb, PEER_BLOCK, PEER_HEADS, PEER_TOPK),
                          gate.reshape(nb, PEER_BLOCK, PEER_HEADS, PEER_TOPK)))
    return out.reshape(B, L, D)


def trunk_layer(x, mod, w_in, lb_fwd, lb_bwd, hgrn_norm_g, q_norm_g, k_norm_g, w_branch_rec, w_branch_att,
                w_out, ln1_g, ln1_b, ln2_g, ln2_b, peer_w_query, peer_sub_keys, peer_u, peer_v,
                s0_fwd, s0_bwd, ctx_k, ctx_v):
    B, L, _ = x.shape
    sh1, sc1, g1, sh2, sc2, g2 = jnp.split(mod[:, None, :], 6, axis=-1)
    h = x * (1.0 + sc1) + sh1
    hq, hff, hfb, hi, hg, aq, ak, av, ga, gb = jnp.split(h @ w_in, IN_OFFSETS, axis=-1)
    o_rec, s_f, s_b = hgrn2_branch(hq, hff, hfb, hi, hg, lb_fwd, lb_bwd, hgrn_norm_g, s0_fwd, s0_bwd)
    q = rms_norm(aq.reshape(B, L, ATTN_HEADS, HEAD_DIM), q_norm_g)
    k = rms_norm(ak.reshape(B, L, KV_HEADS, HEAD_DIM), k_norm_g)
    v = av.reshape(B, L, KV_HEADS, HEAD_DIM)
    if ctx_k is None:
        o_att = attend(q, k, v)
    else:
        rows = L // GRID_W
        o_att = attend(axial_rope(q, rows),
                       jnp.concatenate([axial_rope(k, rows), ctx_k], axis=1),
                       jnp.concatenate([v, ctx_v], axis=1))
    y = jax.nn.sigmoid(ga) * (o_rec @ w_branch_rec) + jax.nn.sigmoid(gb) * (o_att @ w_branch_att)
    x = layer_norm(DEEPNORM_ALPHA * x + g1 * (y @ w_out), ln1_g, ln1_b)
    h2 = x * (1.0 + sc2) + sh2
    x = layer_norm(DEEPNORM_ALPHA * x + g2 * peer_mixer(h2, peer_w_query, peer_sub_keys, peer_u, peer_v), ln2_g, ln2_b)
    return x, k, v, jnp.stack([s_f, s_b], axis=1).astype(x.dtype)


def setup_inputs(seed: int = 0) -> dict:
    key = jax.random.key(seed)
    ks = jax.random.split(key, 26)

    def nrm(k, shape, s):
        return jax.random.normal(k, shape, jnp.float32) * s

    D = D_MODEL
    return {
        'x_prompt': nrm(ks[0], (BATCH, SEQ, D), 1.0),
        'x_sample': nrm(ks[1], (DEC_BATCH, DEC_SEQ, D), 1.0),
        'cache_attn_k': nrm(ks[2], (DEC_BATCH, DEPTH, PAST_LEN, KV_HEADS, HEAD_DIM), 1.0),
        'cache_attn_v': nrm(ks[3], (DEC_BATCH, DEPTH, PAST_LEN, KV_HEADS, HEAD_DIM), 1.0),
        'state_hgrn': nrm(ks[4], (DEC_BATCH, DEPTH, 2, HGRN_HEADS, HGRN_DK, HGRN_DV), 0.5),
        'c': nrm(ks[5], (DEC_BATCH, D), 1.0),
        'c_ctx': nrm(ks[6], (D,), 1.0),
        'w_mod': nrm(ks[7], (DEPTH, D, 6 * D), 0.5 * D ** -0.5),
        'b_mod': nrm(ks[8], (DEPTH, 6 * D), 0.02),
        'w_in': nrm(ks[9], (DEPTH, D, IN_COLS), D ** -0.5),
        'hgrn_lb_logits': nrm(ks[10], (2, DEPTH, HGRN_WIDTH), 0.1),
        'hgrn_norm_g': 1.0 + nrm(ks[11], (DEPTH, HGRN_DV), 0.02),
        'q_norm_g': 1.0 + nrm(ks[12], (DEPTH, HEAD_DIM), 0.02),
        'k_norm_g': 1.0 + nrm(ks[13], (DEPTH, HEAD_DIM), 0.02),
        'w_branch_rec': nrm(ks[14], (DEPTH, HGRN_WIDTH, D), HGRN_WIDTH ** -0.5),
        'w_branch_att': nrm(ks[15], (DEPTH, ATTN_WIDTH, D), ATTN_WIDTH ** -0.5),
        'w_out': nrm(ks[16], (DEPTH, D, D), DEEPNORM_BETA * D ** -0.5),
        'ln1_g': 1.0 + nrm(ks[17], (DEPTH, D), 0.02),
        'ln1_b': nrm(ks[18], (DEPTH, D), 0.02),
        'ln2_g': 1.0 + nrm(ks[19], (DEPTH, D), 0.02),
        'ln2_b': nrm(ks[20], (DEPTH, D), 0.02),
        'peer_w_query': nrm(ks[21], (DEPTH, D, PEER_HEADS * PEER_KEY_DIM), D ** -0.5),
        'peer_sub_keys': nrm(ks[22], (DEPTH, PEER_HEADS, 2, N_KEYS, PEER_HALF), PEER_HALF ** -0.5),
        'peer_u': nrm(ks[23], (DEPTH, N_EXPERTS, D), D ** -0.5),
        'peer_v': nrm(ks[24], (DEPTH, N_EXPERTS, D), DEEPNORM_BETA),
    }


def reference(x_prompt, x_sample, cache_attn_k, cache_attn_v, state_hgrn, c, c_ctx, w_mod, b_mod, w_in,
              hgrn_lb_logits, hgrn_norm_g, q_norm_g, k_norm_g, w_branch_rec, w_branch_att, w_out,
              ln1_g, ln1_b, ln2_g, ln2_b, peer_w_query, peer_sub_keys, peer_u, peer_v):
    lb = jnp.cumsum(jax.nn.softmax(hgrn_lb_logits.astype(jnp.float32), axis=1), axis=1)
    lb = lb - lb[:, :1]
    zero_state = jnp.zeros((x_prompt.shape[0], HGRN_HEADS, HGRN_DK, HGRN_DV), jnp.float32)
    silu_ctx = jax.nn.silu(c_ctx)[None, :]
    silu_c = jax.nn.silu(c)
    xp, xs = x_prompt, x_sample
    new_k, new_v, new_s = [], [], []
    for l in range(DEPTH):
        shared = (w_in[l], lb[0, l], lb[1, l], hgrn_norm_g[l], q_norm_g[l], k_norm_g[l], w_branch_rec[l],
                  w_branch_att[l], w_out[l], ln1_g[l], ln1_b[l], ln2_g[l], ln2_b[l], peer_w_query[l],
                  peer_sub_keys[l], peer_u[l], peer_v[l])
        xp, k_l, v_l, s_l = trunk_layer(xp, silu_ctx @ w_mod[l] + b_mod[l], *shared,
                                        zero_state, zero_state, None, None)
        new_k.append(k_l)
        new_v.append(v_l)
        new_s.append(s_l)
        xs, _, _, _ = trunk_layer(xs, silu_c @ w_mod[l] + b_mod[l], *shared,
                                  state_hgrn[:, l, 0], state_hgrn[:, l, 1],
                                  cache_attn_k[:, l], cache_attn_v[:, l])
    return (xp, xs, jnp.stack(new_k, axis=1), jnp.stack(new_v, axis=1), jnp.stack(new_s, axis=1))
```

```python
import functools
import math

import numpy as np
import jax
import jax.numpy as jnp
from jax import lax
from jax.experimental import pallas as pl
from jax.experimental.pallas import tpu as pltpu

F32 = jnp.float32
BF16 = jnp.bfloat16

D_MODEL = 1024
GRID_W = 64
HGRN_HEADS = 8
HGRN_DK = 128
HGRN_WIDTH = HGRN_HEADS * HGRN_DK
HGRN_CHUNK = 64
ATTN_HEADS = 16
KV_HEADS = 4
HEAD_DIM = 64
GROUP = ATTN_HEADS // KV_HEADS
ATTN_WIDTH = ATTN_HEADS * HEAD_DIM
KV_WIDTH = KV_HEADS * HEAD_DIM
ATTN_SCALE = HEAD_DIM ** -0.5
ROPE_THETA = 10000.0
PEER_HEADS = 8
PEER_HALF = 128
N_KEYS = 128
N_EXPERTS = N_KEYS * N_KEYS
PEER_TOPK = 16
MODEL_DEPTH = 4
DEEPNORM_ALPHA = (2 * MODEL_DEPTH) ** 0.25
EPS = 1e-6
NEG_INF = float("-inf")

IN_COLS = 5 * HGRN_WIDTH + ATTN_WIDTH + 2 * D_MODEL + 2 * KV_WIDTH
COL_AQ = 5 * HGRN_WIDTH
COL_GA = COL_AQ + ATTN_WIDTH
COL_GB = COL_GA + D_MODEL
COL_AK = COL_GB + D_MODEL
COL_AV = COL_AK + KV_WIDTH

VMEM_LIMIT = 56 * 1024 * 1024


def _params(*sem):
    return pltpu.CompilerParams(dimension_semantics=sem, vmem_limit_bytes=VMEM_LIMIT)


def _tile(n, pref):
    t = min(n, pref)
    while n % t or t % 8:
        t -= 1
    return t


def _sigmoid(x):
    return 1.0 / (1.0 + jnp.exp(-x))


def _dot(a, b):
    return jnp.dot(a, b, preferred_element_type=F32)


def _dot_nt(a, b):
    return lax.dot_general(a, b, (((1,), (1,)), ((), ())), preferred_element_type=F32)


def _mod_kernel(cond_ref, w_ref, b_ref, o_ref):
    c = cond_ref[...]
    s = c * _sigmoid(c)
    o_ref[0] = jnp.dot(s, w_ref[0], precision=lax.Precision.HIGHEST, preferred_element_type=F32) + b_ref[0]


def _modulation(cond8, w_mod, b_mod):
    depth, d, n = w_mod.shape
    tn = _tile(n, 1536)
    return pl.pallas_call(
        _mod_kernel,
        grid=(depth, n // tn),
        in_specs=[
            pl.BlockSpec((8, d), lambda l, j: (0, 0)),
            pl.BlockSpec((1, d, tn), lambda l, j: (l, 0, j)),
            pl.BlockSpec((1, 1, tn), lambda l, j: (l, 0, j)),
        ],
        out_specs=pl.BlockSpec((1, 8, tn), lambda l, j: (l, 0, j)),
        out_shape=jax.ShapeDtypeStruct((depth, 8, n), F32),
        compiler_params=_params("parallel", "parallel"),
        name="modulation",
    )(cond8, w_mod, b_mod.reshape(depth, 1, n))


def _inproj_kernel(x_ref, sh_ref, sc_ref, w_ref, z_ref, h_scr):
    @pl.when(pl.program_id(1) == 0)
    def _():
        h_scr[...] = (x_ref[...] * (1.0 + sc_ref[0]) + sh_ref[0]).astype(BF16)

    z_ref[...] = _dot(h_scr[...], w_ref[...])


def _in_proj(x2d, mod_l, w_in_bf, rows_of_tile, tm):
    t, d = x2d.shape
    n = w_in_bf.shape[1]
    tn = 512
    mod3 = mod_l.reshape(8, 1, 6 * d)
    return pl.pallas_call(
        _inproj_kernel,
        grid=(t // tm, n // tn),
        in_specs=[
            pl.BlockSpec((tm, d), lambda i, j: (i, 0)),
            pl.BlockSpec((1, 1, d), lambda i, j: (rows_of_tile(i), 0, 0)),
            pl.BlockSpec((1, 1, d), lambda i, j: (rows_of_tile(i), 0, 1)),
            pl.BlockSpec((d, tn), lambda i, j: (0, j)),
        ],
        out_specs=pl.BlockSpec((tm, tn), lambda i, j: (i, j)),
        out_shape=jax.ShapeDtypeStruct((t, n), F32),
        scratch_shapes=[pltpu.VMEM((tm, d), BF16)],
        compiler_params=_params("parallel", "arbitrary"),
        name="in_proj",
    )(x2d, mod3, mod3, w_in_bf)


def _level_reference_rows(beta, c):
    sub = lax.broadcasted_iota(jnp.int32, (8, HGRN_DK), 0)

    def bc(i, n=8):
        return jnp.broadcast_to(beta[i:i + 1, :], (n, HGRN_DK))

    out = []
    m = 1
    while m < c:
        blocks = []
        if m >= 8:
            for blk in range(c // (2 * m)):
                blocks.append(bc(blk * 2 * m + m, 2 * m))
        elif m == 4:
            for v in range(c // 8):
                blocks.append(bc(8 * v + 4))
        elif m == 2:
            for v in range(c // 8):
                blocks.append(jnp.where(sub < 4, bc(8 * v + 2), bc(8 * v + 6)))
        else:
            for v in range(c // 8):
                lo = jnp.where(sub < 2, bc(8 * v + 1), bc(8 * v + 3))
                hi = jnp.where(sub < 6, bc(8 * v + 5), bc(8 * v + 7))
                blocks.append(jnp.where(sub < 4, lo, hi))
        out.append(blocks[0] if len(blocks) == 1 else jnp.concatenate(blocks, axis=0))
        m *= 2
    return out


def _hgrn_chunk(zq, zf, v, lb, st, tri, eye, q_half, pair_masks, reverse, c):
    q = zq * _sigmoid(zq)
    sig = _sigmoid(zf)
    f = lb + (1.0 - lb) * sig
    k = (1.0 - lb) * (1.0 - sig)
    g = jnp.log(f)
    g1 = g.astype(BF16)
    r1 = g - g1.astype(F32)
    g2 = r1.astype(BF16)
    g3 = (r1 - g2.astype(F32)).astype(BF16)
    beta3 = _dot(tri, jnp.concatenate([g1, g2, g3], axis=1))
    beta = beta3[:, :HGRN_DK] + beta3[:, HGRN_DK:2 * HGRN_DK] + beta3[:, 2 * HGRN_DK:]
    last = 0 if reverse else c - 1
    beta_tot = beta[last:last + 1, :]

    a = jnp.where(eye, _dot_nt(q.astype(BF16), k.astype(BF16)), 0.0)
    for r, qh, pm in zip(_level_reference_rows(beta, c), q_half, pair_masks):
        e = jnp.exp(-jnp.abs(beta - r))
        u = (jnp.where(qh, q, k) * e).astype(BF16)
        a = a + jnp.where(pm, _dot_nt(u, u), 0.0)

    v_bf = v.astype(BF16)
    qb = (q * jnp.exp(beta)).astype(BF16)
    o = _dot(a.astype(BF16), v_bf) + _dot_nt(qb, st.astype(BF16))
    kb = (k * jnp.exp(beta_tot - beta)).astype(BF16)
    st_new = st * jnp.exp(beta_tot) + lax.dot_general(
        v_bf, kb, (((0,), (0,)), ((), ())), preferred_element_type=F32)
    return o, st_new


def _hgrn_kernel(rbf_ref, rbb_ref, first_ref, seq_ref,
                 qf_ref, ff_ref, vf_ref, qb_ref, fb_ref, vb_ref, lb_ref, s0_ref,
                 of_ref, ob_ref, sout_ref, st_scr, *, c, n_last):
    s = pl.program_id(0)

    @pl.when(first_ref[s] == 1)
    def _():
        for d in range(2):
            for h in range(HGRN_HEADS):
                st_scr[d, h] = s0_ref[0, d, h].T

    row = lax.broadcasted_iota(jnp.int32, (c, c), 0)
    col = lax.broadcasted_iota(jnp.int32, (c, c), 1)
    rowd = lax.broadcasted_iota(jnp.int32, (c, HGRN_DK), 0)
    eye = row == col

    for d, (q_ref, f_ref, v_ref, o_ref) in enumerate(
            ((qf_ref, ff_ref, vf_ref, of_ref), (qb_ref, fb_ref, vb_ref, ob_ref))):
        reverse = d == 1
        tri = (col >= row if reverse else col <= row).astype(BF16)
        q_half, pair_masks = [], []
        m = 1
        while m < c:
            q_par = 0 if reverse else 1
            q_half.append(((rowd // m) % 2) == q_par)
            pair_masks.append(((row // (2 * m)) == (col // (2 * m)))
                              & (((row // m) % 2) == q_par) & (((col // m) % 2) == 1 - q_par))
            m *= 2

        def head(h, carry, q_ref=q_ref, f_ref=f_ref, v_ref=v_ref, o_ref=o_ref, d=d, reverse=reverse,
                 tri=tri, q_half=q_half, pair_masks=pair_masks):
            cs = pl.ds(pl.multiple_of(h * HGRN_DK, HGRN_DK), HGRN_DK)
            o, st_new = _hgrn_chunk(q_ref[:, cs], f_ref[:, cs], v_ref[:, cs], lb_ref[pl.ds(d, 1), cs],
                                    st_scr[d, h], tri, eye, q_half, pair_masks, reverse, c)
            o_ref[:, cs] = o
            st_scr[d, h] = st_new
            return carry

        lax.fori_loop(0, HGRN_HEADS, head, 0)

    @pl.when(first_ref[jnp.minimum(s + 1, n_last)] == 1)
    def _():
        for d in range(2):
            for h in range(HGRN_HEADS):
                sout_ref[0, d, h] = st_scr[d, h].T


def _hgrn_scan(z2d, lb2, s0_all, seq_rows, c):
    t = z2d.shape[0]
    rbf, rbb, first, seq = [], [], [], []
    for si, (r0, length) in enumerate(seq_rows):
        n = length // c
        for ci in range(n):
            rbf.append(r0 // c + ci)
            rbb.append(r0 // c + n - 1 - ci)
            first.append(1 if ci == 0 else 0)
            seq.append(si)
    n_steps = len(rbf)
    first.append(1)
    w = HGRN_WIDTH

    def zspec(which, colblk):
        if which == 0:
            return pl.BlockSpec((c, w), lambda s, rbf, rbb, fi, sq: (rbf[s], colblk))
        return pl.BlockSpec((c, w), lambda s, rbf, rbb, fi, sq: (rbb[s], colblk))

    n_seq = len(seq_rows)
    state_spec = pl.BlockSpec((1, 2, HGRN_HEADS, HGRN_DK, HGRN_DK),
                              lambda s, rbf, rbb, fi, sq: (sq[s], 0, 0, 0, 0))
    grid_spec = pltpu.PrefetchScalarGridSpec(
        num_scalar_prefetch=4,
        grid=(n_steps,),
        in_specs=[zspec(0, 0), zspec(0, 1), zspec(0, 3), zspec(1, 0), zspec(1, 2), zspec(1, 3),
                  pl.BlockSpec((2, w), lambda s, *_: (0, 0)), state_spec],
        out_specs=[zspec(0, 0), zspec(1, 0), state_spec],
        scratch_shapes=[pltpu.VMEM((2, HGRN_HEADS, HGRN_DK, HGRN_DK), F32)],
    )
    return pl.pallas_call(
        functools.partial(_hgrn_kernel, c=c, n_last=n_steps),
        grid_spec=grid_spec,
        out_shape=[jax.ShapeDtypeStruct((t, w), F32), jax.ShapeDtypeStruct((t, w), F32),
                   jax.ShapeDtypeStruct((n_seq, 2, HGRN_HEADS, HGRN_DK, HGRN_DK), F32)],
        compiler_params=_params("arbitrary"),
        name="hgrn_scan",
    )(jnp.asarray(rbf, jnp.int32), jnp.asarray(rbb, jnp.int32), jnp.asarray(first, jnp.int32),
      jnp.asarray(seq, jnp.int32), z2d, z2d, z2d, z2d, z2d, z2d, lb2, s0_all)


def _qk_prep_kernel(x_ref, g_ref, cos_ref, sin_ref, swap_ref, o_ref, *, rope, scale):
    x = x_ref[0]
    ms = jnp.mean(x * x, axis=-1, keepdims=True)
    y = x * lax.rsqrt(ms + EPS) * g_ref[...]
    if rope:
        y_sw = _dot(y.astype(BF16), swap_ref[...])
        y = y * cos_ref[...] + y_sw * sin_ref[...]
    o_ref[0] = y * scale


def _qk_prep(x, gain, cos, sin, rope, scale):
    r, length, hd = x.shape
    tl = _tile(length, 1024)
    swap = np.zeros((hd, hd), np.float32)
    for i in range(hd):
        swap[(i + hd // 2) % hd, i] = 1.0
    return pl.pallas_call(
        functools.partial(_qk_prep_kernel, rope=rope, scale=scale),
        grid=(r, length // tl),
        in_specs=[
            pl.BlockSpec((1, tl, hd), lambda i, j: (i, j, 0)),
            pl.BlockSpec((1, hd), lambda i, j: (0, 0)),
            pl.BlockSpec((tl, hd), lambda i, j: (j, 0)),
            pl.BlockSpec((tl, hd), lambda i, j: (j, 0)),
            pl.BlockSpec((hd, hd), lambda i, j: (0, 0)),
        ],
        out_specs=pl.BlockSpec((1, tl, hd), lambda i, j: (i, j, 0)),
        out_shape=jax.ShapeDtypeStruct((r, length, hd), F32),
        compiler_params=_params("parallel", "parallel"),
        name="qk_prep",
    )(x, gain.reshape(1, hd), cos, sin, jnp.asarray(swap, BF16))


def _attn_kernel(q_ref, k_ref, v_ref, o_ref, *, ts, n_s, tq):
    rows = GROUP * tq
    q = q_ref[0, 0].reshape(rows, HEAD_DIM).astype(BF16)

    def body(i, carry):
        m, l, acc = carry
        sl = pl.ds(pl.multiple_of(i * ts, ts), ts)
        k = k_ref[0, 0, sl, :].astype(BF16)
        v = v_ref[0, 0, sl, :].astype(BF16)
        s = _dot_nt(q, k)
        m_new = jnp.maximum(m, jnp.max(s, axis=-1, keepdims=True))
        p = jnp.exp(s - m_new)
        alpha = jnp.exp(m - m_new)
        l = alpha * l + jnp.sum(p, axis=-1, keepdims=True)
        acc = alpha * acc + _dot(p.astype(BF16), v)
        return m_new, l, acc

    m0 = jnp.full((rows, 1), NEG_INF, F32)
    l0 = jnp.zeros((rows, 1), F32)
    acc0 = jnp.zeros((rows, HEAD_DIM), F32)
    m, l, acc = lax.fori_loop(0, n_s, body, (m0, l0, acc0))
    o_ref[0, 0] = (acc / l).reshape(GROUP, tq, HEAD_DIM)


def _attention(q, k, v):
    b, kvh, g, length, hd = q.shape
    s_len = k.shape[2]
    tq = _tile(length, 64)
    ts = _tile(s_len, 512)
    return pl.pallas_call(
        functools.partial(_attn_kernel, ts=ts, n_s=s_len // ts, tq=tq),
        grid=(b, kvh, length // tq),
        in_specs=[
            pl.BlockSpec((1, 1, g, tq, hd), lambda bi, hi, qi: (bi, hi, 0, qi, 0)),
            pl.BlockSpec((1, 1, s_len, hd), lambda bi, hi, qi: (bi, hi, 0, 0)),
            pl.BlockSpec((1, 1, s_len, hd), lambda bi, hi, qi: (bi, hi, 0, 0)),
        ],
        out_specs=pl.BlockSpec((1, 1, g, tq, hd), lambda bi, hi, qi: (bi, hi, 0, qi, 0)),
        out_shape=jax.ShapeDtypeStruct(q.shape, F32),
        compiler_params=_params("parallel", "parallel", "parallel"),
        name="attention",
    )(q, k, v)


def _layer_norm(t, g, b):
    mu = jnp.mean(t, axis=-1, keepdims=True)
    tc = t - mu
    var = jnp.mean(tc * tc, axis=-1, keepdims=True)
    return tc * lax.rsqrt(var + EPS) * g + b


def _merge_kernel(of_ref, ob_ref, hg_ref, ga_ref, gb_ref, oatt_ref, x_ref, g1_ref, sh2_ref, sc2_ref,
                  wrec_ref, watt_ref, wout_ref, ng_ref, lng_ref, lnb_ref, x1_ref, h2_ref, orec_scr):
    for h in range(HGRN_HEADS):
        cs = slice(h * HGRN_DK, (h + 1) * HGRN_DK)
        o = of_ref[:, cs] + ob_ref[:, cs]
        ms = jnp.mean(o * o, axis=-1, keepdims=True)
        zg = hg_ref[:, cs]
        orec_scr[:, cs] = (o * lax.rsqrt(ms + EPS) * ng_ref[...] * (zg * _sigmoid(zg))).astype(BF16)
    y = (_sigmoid(ga_ref[...]) * _dot(orec_scr[...], wrec_ref[...])
         + _sigmoid(gb_ref[...]) * _dot(oatt_ref[...].astype(BF16), watt_ref[...]))
    u = _dot(y.astype(BF16), wout_ref[...])
    x1 = _layer_norm(DEEPNORM_ALPHA * x_ref[...] + g1_ref[0] * u, lng_ref[...], lnb_ref[...])
    x1_ref[...] = x1
    h2_ref[...] = (x1 * (1.0 + sc2_ref[0]) + sh2_ref[0]).astype(BF16)


def _merge(o_f, o_b, z2d, o_att, x2d, mod_l, w_rec, w_att, w_out, norm_g, ln_g, ln_b, rows_of_tile, tm):
    t, d = x2d.shape
    mod3 = mod_l.reshape(8, 1, 6 * d)

    def rows(colblk):
        return pl.BlockSpec((tm, d), lambda i: (i, colblk))

    def modspec(chunk):
        return pl.BlockSpec((1, 1, d), lambda i: (rows_of_tile(i), 0, chunk))

    def full(shape):
        return pl.BlockSpec(shape, lambda i: (0,) * len(shape))

    return pl.pallas_call(
        _merge_kernel,
        grid=(t // tm,),
        in_specs=[rows(0), rows(0), rows(4), rows(COL_GA // d), rows(COL_GB // d), rows(0), rows(0),
                  modspec(2), modspec(3), modspec(4),
                  full((d, d)), full((d, d)), full((d, d)), full((1, HGRN_DK)), full((1, d)), full((1, d))],
        out_specs=[rows(0), rows(0)],
        out_shape=[jax.ShapeDtypeStruct((t, d), F32), jax.ShapeDtypeStruct((t, d), BF16)],
        scratch_shapes=[pltpu.VMEM((tm, d), BF16)],
        compiler_params=_params("parallel"),
        name="merge",
    )(o_f, o_b, z2d, z2d, z2d, o_att, x2d, mod3, mod3, mod3, w_rec, w_att, w_out,
      norm_g.reshape(1, HGRN_DK), ln_g.reshape(1, d), ln_b.reshape(1, d))


def _top16(s, n):
    iota = lax.broadcasted_iota(jnp.int32, s.shape, 0).astype(F32)
    vals, idxs = [], []
    for _ in range(PEER_TOPK):
        m = jnp.max(s, axis=0, keepdims=True)
        idx = jnp.min(jnp.where(s == m, iota, float(n)), axis=0, keepdims=True)
        vals.append(m)
        idxs.append(idx)
        s = jnp.where(iota == idx, NEG_INF, s)
    return vals, idxs


def _peer_topk_kernel(h2_ref, wq_ref, keys_ref, ida_ref, idb_ref, gate_ref):
    qf = _dot(h2_ref[...], wq_ref[...])
    k = PEER_TOPK
    for h in range(PEER_HEADS):
        tops = []
        for p in range(2):
            seg = (h * 2 + p) * PEER_HALF
            st = _dot_nt(keys_ref[h, p], qf[:, seg:seg + PEER_HALF].astype(BF16))
            tops.append(_top16(st, N_KEYS))
        (v0, i0), (v1, i1) = tops
        v1c = jnp.concatenate(v1, axis=0)
        i1c = jnp.concatenate(i1, axis=0)
        cand = jnp.concatenate([v0[a] + v1c for a in range(k)], axis=0)
        cand_a = jnp.concatenate([jnp.broadcast_to(i0[a], v1c.shape) for a in range(k)], axis=0)
        cand_b = jnp.concatenate([i1c] * k, axis=0)
        iota = lax.broadcasted_iota(jnp.int32, cand.shape, 0).astype(F32)
        best, ea, eb = [], [], []
        for _ in range(k):
            m = jnp.max(cand, axis=0, keepdims=True)
            pos = jnp.min(jnp.where(cand == m, iota, float(k * k)), axis=0, keepdims=True)
            hit = iota == pos
            best.append(m)
            ea.append(jnp.max(jnp.where(hit, cand_a, -1.0), axis=0, keepdims=True))
            eb.append(jnp.max(jnp.where(hit, cand_b, -1.0), axis=0, keepdims=True))
            cand = jnp.where(hit, NEG_INF, cand)
        best = jnp.concatenate(best, axis=0)
        e = jnp.exp(best - best[0:1, :])
        gate_ref[h * k:(h + 1) * k, :] = e / jnp.sum(e, axis=0, keepdims=True)
        ida_ref[h * k:(h + 1) * k, :] = jnp.concatenate(ea, axis=0)
        idb_ref[h * k:(h + 1) * k, :] = jnp.concatenate(eb, axis=0)


def _peer_topk(h2, wq_bf, keys_bf, tb):
    t, d = h2.shape
    nq = wq_bf.shape[1]
    rows = PEER_HEADS * PEER_TOPK
    out = jax.ShapeDtypeStruct((rows, t), F32)
    ospec = pl.BlockSpec((rows, tb), lambda i: (0, i))
    return pl.pallas_call(
        _peer_topk_kernel,
        grid=(t // tb,),
        in_specs=[
            pl.BlockSpec((tb, d), lambda i: (i, 0)),
            pl.BlockSpec((d, nq), lambda i: (0, 0)),
            pl.BlockSpec((PEER_HEADS, 2, N_KEYS, PEER_HALF), lambda i: (0, 0, 0, 0)),
        ],
        out_specs=[ospec, ospec, ospec],
        out_shape=[out, out, out],
        compiler_params=_params("parallel"),
        name="peer_topk",
    )(h2, wq_bf, keys_bf)


def _gelu(a):
    return 0.5 * a * (1.0 + lax.erf(a * (1.0 / math.sqrt(2.0))))


def _peer_mix_kernel(h2_ref, ida_ref, idb_ref, gate_ref, ut_ref, v_ref, x1_ref, g2_ref, lng_ref, lnb_ref,
                     x2_ref, w_scr, p_scr, acc_scr, *, tb, te):
    j = pl.program_id(1)
    rows_per_tile = te // N_KEYS

    @pl.when(j == 0)
    def _():
        acc_scr[...] = jnp.zeros_like(acc_scr)
        iota = lax.broadcasted_iota(jnp.int32, (N_KEYS, PEER_HEADS * PEER_TOPK), 0).astype(F32)

        def tok(t, carry):
            a_row = ida_ref[pl.ds(t, 1), :]
            b_row = idb_ref[pl.ds(t, 1), :]
            g_row = gate_ref[pl.ds(t, 1), :]
            at = jnp.where(iota == a_row, 1.0, 0.0).astype(BF16)
            gbt = jnp.where(iota == b_row, g_row, 0.0).astype(BF16)
            w_scr[pl.ds(pl.multiple_of(t * N_KEYS, N_KEYS), N_KEYS), :] = _dot_nt(at, gbt)
            return carry

        lax.fori_loop(0, tb, tok, 0)

    a = _dot(h2_ref[...], ut_ref[...])
    for r in range(rows_per_tile):
        w = w_scr[pl.ds(j * rows_per_tile + r, tb, stride=N_KEYS), :]
        cs = slice(r * N_KEYS, (r + 1) * N_KEYS)
        p_scr[:, cs] = (w * _gelu(a[:, cs])).astype(BF16)
    acc_scr[...] += _dot(p_scr[...], v_ref[...])

    @pl.when(j == pl.num_programs(1) - 1)
    def _():
        x2_ref[...] = _layer_norm(DEEPNORM_ALPHA * x1_ref[...] + g2_ref[0] * acc_scr[...],
                                  lng_ref[...], lnb_ref[...])


def _peer_mix(h2, ida, idb, gate, ut_bf, v_bf, x1, mod_l, ln_g, ln_b, rows_of_tile, tb):
    t, d = x1.shape
    n_e = v_bf.shape[0]
    te = 1024
    hk = PEER_HEADS * PEER_TOPK
    mod3 = mod_l.reshape(8, 1, 6 * d)
    return pl.pallas_call(
        functools.partial(_peer_mix_kernel, tb=tb, te=te),
        grid=(t // tb, n_e // te),
        in_specs=[
            pl.BlockSpec((tb, d), lambda i, j: (i, 0)),
            pl.BlockSpec((tb, hk), lambda i, j: (i, 0)),
            pl.BlockSpec((tb, hk), lambda i, j: (i, 0)),
            pl.BlockSpec((tb, hk), lambda i, j: (i, 0)),
            pl.BlockSpec((d, te), lambda i, j: (0, j)),
            pl.BlockSpec((te, d), lambda i, j: (j, 0)),
            pl.BlockSpec((tb, d), lambda i, j: (i, 0)),
            pl.BlockSpec((1, 1, d), lambda i, j: (rows_of_tile(i), 0, 5)),
            pl.BlockSpec((1, d), lambda i, j: (0, 0)),
            pl.BlockSpec((1, d), lambda i, j: (0, 0)),
        ],
        out_specs=pl.BlockSpec((tb, d), lambda i, j: (i, 0)),
        out_shape=jax.ShapeDtypeStruct((t, d), F32),
        scratch_shapes=[pltpu.VMEM((tb * N_KEYS, N_KEYS), F32), pltpu.VMEM((tb, te), BF16),
                        pltpu.VMEM((tb, d), F32)],
        compiler_params=_params("parallel", "arbitrary"),
        name="peer_mix",
    )(h2, ida, idb, gate, ut_bf, v_bf, x1, mod3, ln_g.reshape(1, d), ln_b.reshape(1, d))


def _rope_tables(length):
    rows = length // GRID_W
    row = jnp.broadcast_to(jnp.arange(rows, dtype=F32)[:, None], (rows, GRID_W)).reshape(length)
    col = jnp.broadcast_to(jnp.arange(GRID_W, dtype=F32)[None, :], (rows, GRID_W)).reshape(length)
    n_freq = HEAD_DIM // 4
    inv = jnp.power(ROPE_THETA, -jnp.arange(n_freq, dtype=F32) / n_freq)
    ang = jnp.concatenate([row[:, None] * inv, col[:, None] * inv], axis=-1)
    cos, sin = jnp.cos(ang), jnp.sin(ang)
    return jnp.concatenate([cos, cos], axis=-1), jnp.concatenate([-sin, sin], axis=-1)


def kernel(x_prompt, x_sample, cache_attn_k, cache_attn_v, state_hgrn, c, c_ctx, w_mod, b_mod, w_in,
           hgrn_lb_logits, hgrn_norm_g, q_norm_g, k_norm_g, w_branch_rec, w_branch_att, w_out,
           ln1_g, ln1_b, ln2_g, ln2_b, peer_w_query, peer_sub_keys, peer_u, peer_v):
    nb, seq, d = x_prompt.shape
    db, dseq, _ = x_sample.shape
    depth = w_in.shape[0]
    past = cache_attn_k.shape[2]
    t_ctx, t_lat = nb * seq, db * dseq
    t_all = t_ctx + t_lat
    assert d == D_MODEL and 1 + db <= 8

    tm = _tile(math.gcd(t_ctx, dseq), 1024)
    tm_merge = _tile(math.gcd(t_ctx, dseq), 256)
    tb_peer = _tile(math.gcd(t_ctx, dseq), 256)

    def rows_of_tile_fn(tile):
        n_ctx = t_ctx // tile
        per = dseq // tile
        return lambda i: jnp.where(i < n_ctx, 0, 1 + (i - n_ctx) // per)

    cond8 = jnp.concatenate([c_ctx[None, :], c, jnp.zeros((8 - 1 - db, d), F32)], axis=0)
    mod = _modulation(cond8, w_mod, b_mod)

    lb = jnp.cumsum(jax.nn.softmax(hgrn_lb_logits.astype(F32), axis=1), axis=1)
    lb = lb - lb[:, :1]

    segs = jnp.split(w_in, np.cumsum([HGRN_WIDTH] * 5 + [ATTN_WIDTH, KV_WIDTH, KV_WIDTH, D_MODEL]).tolist(), axis=-1)
    hq_w, hff_w, hfb_w, hi_w, hg_w, aq_w, ak_w, av_w, ga_w, gb_w = segs
    w_in_bf = jnp.concatenate([hq_w, hff_w, hfb_w, hi_w, hg_w, aq_w, ga_w, gb_w, ak_w, av_w], axis=-1).astype(BF16)
    w_rec_bf, w_att_bf, w_out_bf = (w.astype(BF16) for w in (w_branch_rec, w_branch_att, w_out))
    wq_bf = peer_w_query.astype(BF16)
    keys_bf = peer_sub_keys.astype(BF16)
    ut_bf = jnp.swapaxes(peer_u, 1, 2).astype(BF16)
    v_bf = peer_v.astype(BF16)

    cos_t, sin_t = _rope_tables(dseq)
    ones_t, zeros_t = jnp.ones((seq, HEAD_DIM), F32), jnp.zeros((seq, HEAD_DIM), F32)

    seq_rows = [(i * seq, seq) for i in range(nb)] + [(t_ctx + i * dseq, dseq) for i in range(db)]
    zero_state = jnp.zeros((nb, 2, HGRN_HEADS, HGRN_DK, HGRN_DK), F32)

    x2d = jnp.concatenate([x_prompt.reshape(t_ctx, d), x_sample.reshape(t_lat, d)], axis=0)
    new_k, new_v, new_s = [], [], []
    for l in range(depth):
        z2d = _in_proj(x2d, mod[l], w_in_bf[l], rows_of_tile_fn(tm), tm)

        s0_all = jnp.concatenate([zero_state, state_hgrn[:, l]], axis=0)
        o_f, o_b, s_fin = _hgrn_scan(z2d, lb[:, l], s0_all, seq_rows, HGRN_CHUNK)
        new_s.append(s_fin[:nb])

        aq, ak, av = z2d[:, COL_AQ:COL_GA], z2d[:, COL_AK:COL_AV], z2d[:, COL_AV:]

        def heads_major(a2d, b, length, nh):
            return jnp.transpose(a2d.reshape(b, length, nh, HEAD_DIM), (0, 2, 1, 3))

        q_c = heads_major(aq[:t_ctx], nb, seq, ATTN_HEADS).reshape(nb * ATTN_HEADS, seq, HEAD_DIM)
        k_c = heads_major(ak[:t_ctx], nb, seq, KV_HEADS).reshape(nb * KV_HEADS, seq, HEAD_DIM)
        v_c = heads_major(av[:t_ctx], nb, seq, KV_HEADS)
        q_c = _qk_prep(q_c, q_norm_g[l], ones_t, zeros_t, False, ATTN_SCALE)
        k_c = _qk_prep(k_c, k_norm_g[l], ones_t, zeros_t, False, 1.0).reshape(nb, KV_HEADS, seq, HEAD_DIM)
        o_c = _attention(q_c.reshape(nb, KV_HEADS, GROUP, seq, HEAD_DIM), k_c, v_c)
        new_k.append(jnp.transpose(k_c, (0, 2, 1, 3)))
        new_v.append(av[:t_ctx].reshape(nb, seq, KV_HEADS, HEAD_DIM))

        q_s = heads_major(aq[t_ctx:], db, dseq, ATTN_HEADS).reshape(db * ATTN_HEADS, dseq, HEAD_DIM)
        k_s = heads_major(ak[t_ctx:], db, dseq, KV_HEADS).reshape(db * KV_HEADS, dseq, HEAD_DIM)
        v_s = heads_major(av[t_ctx:], db, dseq, KV_HEADS)
        q_s = _qk_prep(q_s, q_norm_g[l], cos_t, sin_t, True, ATTN_SCALE)
        k_s = _qk_prep(k_s, k_norm_g[l], cos_t, sin_t, True, 1.0).reshape(db, KV_HEADS, dseq, HEAD_DIM)
        k_s = jnp.concatenate([k_s, jnp.transpose(cache_attn_k[:, l], (0, 2, 1, 3))], axis=2)
        v_s = jnp.concatenate([v_s, jnp.transpose(cache_attn_v[:, l], (0, 2, 1, 3))], axis=2)
        o_s = _attention(q_s.reshape(db, KV_HEADS, GROUP, dseq, HEAD_DIM), k_s, v_s)

        def token_major(o, b, length):
            return jnp.transpose(o.reshape(b, ATTN_HEADS, length, HEAD_DIM), (0, 2, 1, 3)).reshape(b * length, ATTN_WIDTH)

        o_att = jnp.concatenate([token_major(o_c, nb, seq), token_major(o_s, db, dseq)], axis=0)

        x1, h2 = _merge(o_f, o_b, z2d, o_att, x2d, mod[l], w_rec_bf[l], w_att_bf[l], w_out_bf[l],
                        hgrn_norm_g[l], ln1_g[l], ln1_b[l], rows_of_tile_fn(tm_merge), tm_merge)

        ida, idb, gate = _peer_topk(h2, wq_bf[l], keys_bf[l], tb_peer)
        x2d = _peer_mix(h2, ida.T, idb.T, gate.T, ut_bf[l], v_bf[l], x1, mod[l], ln2_g[l], ln2_b[l],
                        rows_of_tile_fn(tb_peer), tb_peer)

    y_prompt = x2d[:t_ctx].reshape(nb, seq, d)
    y_sample = x2d[t_ctx:].reshape(db, dseq, d)
    return (y_prompt, y_sample, jnp.stack(new_k, axis=1), jnp.stack(new_v, axis=1), jnp.stack(new_s, axis=1))
```

```python
import functools
import math

import numpy as np
import jax
import jax.numpy as jnp
from jax import lax
from jax.experimental import pallas as pl
from jax.experimental.pallas import tpu as pltpu

F32 = jnp.float32
BF16 = jnp.bfloat16

D_MODEL = 1024
GRID_W = 64
HGRN_HEADS = 8
HGRN_DK = 128
HGRN_WIDTH = HGRN_HEADS * HGRN_DK
HGRN_CHUNK = 64
ATTN_HEADS = 16
KV_HEADS = 4
HEAD_DIM = 64
GROUP = ATTN_HEADS // KV_HEADS
ATTN_WIDTH = ATTN_HEADS * HEAD_DIM
KV_WIDTH = KV_HEADS * HEAD_DIM
ATTN_SCALE = HEAD_DIM ** -0.5
ROPE_THETA = 10000.0
PEER_HEADS = 8
PEER_HALF = 128
N_KEYS = 128
N_EXPERTS = N_KEYS * N_KEYS
PEER_TOPK = 16
MODEL_DEPTH = 4
DEEPNORM_ALPHA = (2 * MODEL_DEPTH) ** 0.25
EPS = 1e-6
NEG_INF = float("-inf")

IN_COLS = 5 * HGRN_WIDTH + ATTN_WIDTH + 2 * D_MODEL + 2 * KV_WIDTH
COL_AQ = 5 * HGRN_WIDTH
COL_GA = COL_AQ + ATTN_WIDTH
COL_GB = COL_GA + D_MODEL
COL_AK = COL_GB + D_MODEL
COL_AV = COL_AK + KV_WIDTH

VMEM_LIMIT = 56 * 1024 * 1024
LOG2_E = math.log2(math.e)
ATTN_TQ = 256
ATTN_TS = 512


def _params(*sem):
    return pltpu.CompilerParams(dimension_semantics=sem, vmem_limit_bytes=VMEM_LIMIT)


def _tile(n, pref):
    t = min(n, pref)
    while n % t or t % 8:
        t -= 1
    return t


def _sigmoid(x):
    return 1.0 / (1.0 + jnp.exp(-x))


def _dot(a, b):
    return jnp.dot(a, b, preferred_element_type=F32)


def _dot_nt(a, b):
    return lax.dot_general(a, b, (((1,), (1,)), ((), ())), preferred_element_type=F32)


def _mod_kernel(cond_ref, w_ref, b_ref, o_ref):
    c = cond_ref[...]
    s = c * _sigmoid(c)
    o_ref[0] = jnp.dot(s, w_ref[0], precision=lax.Precision.HIGHEST, preferred_element_type=F32) + b_ref[0]


def _modulation(cond8, w_mod, b_mod):
    depth, d, n = w_mod.shape
    tn = _tile(n, 1536)
    return pl.pallas_call(
        _mod_kernel,
        grid=(depth, n // tn),
        in_specs=[
            pl.BlockSpec((8, d), lambda l, j: (0, 0)),
            pl.BlockSpec((1, d, tn), lambda l, j: (l, 0, j)),
            pl.BlockSpec((1, 1, tn), lambda l, j: (l, 0, j)),
        ],
        out_specs=pl.BlockSpec((1, 8, tn), lambda l, j: (l, 0, j)),
        out_shape=jax.ShapeDtypeStruct((depth, 8, n), F32),
        compiler_params=_params("parallel", "parallel"),
        name="modulation",
    )(cond8, w_mod, b_mod.reshape(depth, 1, n))


def _inproj_kernel(x_ref, sh_ref, sc_ref, w_ref, z_ref, h_scr):
    @pl.when(pl.program_id(1) == 0)
    def _():
        h_scr[...] = (x_ref[...] * (1.0 + sc_ref[0]) + sh_ref[0]).astype(BF16)

    z_ref[...] = _dot(h_scr[...], w_ref[...])


def _in_proj(x2d, mod_l, w_in_bf, rows_of_tile, tm):
    t, d = x2d.shape
    n = w_in_bf.shape[1]
    tn = 512
    mod3 = mod_l.reshape(8, 1, 6 * d)
    return pl.pallas_call(
        _inproj_kernel,
        grid=(t // tm, n // tn),
        in_specs=[
            pl.BlockSpec((tm, d), lambda i, j: (i, 0)),
            pl.BlockSpec((1, 1, d), lambda i, j: (rows_of_tile(i), 0, 0)),
            pl.BlockSpec((1, 1, d), lambda i, j: (rows_of_tile(i), 0, 1)),
            pl.BlockSpec((d, tn), lambda i, j: (0, j)),
        ],
        out_specs=pl.BlockSpec((tm, tn), lambda i, j: (i, j)),
        out_shape=jax.ShapeDtypeStruct((t, n), F32),
        scratch_shapes=[pltpu.VMEM((tm, d), BF16)],
        compiler_params=_params("parallel", "arbitrary"),
        name="in_proj",
    )(x2d, mod3, mod3, w_in_bf)


def _level_reference_rows(beta, c):
    sub = lax.broadcasted_iota(jnp.int32, (8, HGRN_DK), 0)

    def bc(i, n=8):
        return jnp.broadcast_to(beta[i:i + 1, :], (n, HGRN_DK))

    out = []
    m = 1
    while m < c:
        blocks = []
        if m >= 8:
            for blk in range(c // (2 * m)):
                blocks.append(bc(blk * 2 * m + m, 2 * m))
        elif m == 4:
            for v in range(c // 8):
                blocks.append(bc(8 * v + 4))
        elif m == 2:
            for v in range(c // 8):
                blocks.append(jnp.where(sub < 4, bc(8 * v + 2), bc(8 * v + 6)))
        else:
            for v in range(c // 8):
                lo = jnp.where(sub < 2, bc(8 * v + 1), bc(8 * v + 3))
                hi = jnp.where(sub < 6, bc(8 * v + 5), bc(8 * v + 7))
                blocks.append(jnp.where(sub < 4, lo, hi))
        out.append(blocks[0] if len(blocks) == 1 else jnp.concatenate(blocks, axis=0))
        m *= 2
    return out


def _hgrn_chunk(zq, zf, v, lb, st, tri, eye, q_half, pair_masks, reverse, c):
    q = zq * _sigmoid(zq)
    sig = _sigmoid(zf)
    f = lb + (1.0 - lb) * sig
    k = (1.0 - lb) * (1.0 - sig)
    g = jnp.log(f)
    g1 = g.astype(BF16)
    r1 = g - g1.astype(F32)
    g2 = r1.astype(BF16)
    g3 = (r1 - g2.astype(F32)).astype(BF16)
    beta3 = _dot(tri, jnp.concatenate([g1, g2, g3], axis=1))
    beta = beta3[:, :HGRN_DK] + beta3[:, HGRN_DK:2 * HGRN_DK] + beta3[:, 2 * HGRN_DK:]
    last = 0 if reverse else c - 1
    beta_tot = beta[last:last + 1, :]

    a = jnp.where(eye, _dot_nt(q.astype(BF16), k.astype(BF16)), 0.0)
    for r, qh, pm in zip(_level_reference_rows(beta, c), q_half, pair_masks):
        e = jnp.exp(-jnp.abs(beta - r))
        u = (jnp.where(qh, q, k) * e).astype(BF16)
        a = a + jnp.where(pm, _dot_nt(u, u), 0.0)

    v_bf = v.astype(BF16)
    qb = (q * jnp.exp(beta)).astype(BF16)
    o = _dot(a.astype(BF16), v_bf) + _dot_nt(qb, st.astype(BF16))
    kb = (k * jnp.exp(beta_tot - beta)).astype(BF16)
    st_new = st * jnp.exp(beta_tot) + lax.dot_general(
        v_bf, kb, (((0,), (0,)), ((), ())), preferred_element_type=F32)
    return o, st_new


def _hgrn_kernel(rbf_ref, rbb_ref, first_ref, seq_ref,
                 qf_ref, ff_ref, vf_ref, qb_ref, fb_ref, vb_ref, lb_ref, s0_ref,
                 of_ref, ob_ref, sout_ref, st_scr, *, c, n_last):
    s = pl.program_id(0)

    @pl.when(first_ref[s] == 1)
    def _():
        for d in range(2):
            for h in range(HGRN_HEADS):
                st_scr[d, h] = s0_ref[0, d, h].T

    row = lax.broadcasted_iota(jnp.int32, (c, c), 0)
    col = lax.broadcasted_iota(jnp.int32, (c, c), 1)
    rowd = lax.broadcasted_iota(jnp.int32, (c, HGRN_DK), 0)
    eye = row == col

    for d, (q_ref, f_ref, v_ref, o_ref) in enumerate(
            ((qf_ref, ff_ref, vf_ref, of_ref), (qb_ref, fb_ref, vb_ref, ob_ref))):
        reverse = d == 1
        tri = (col >= row if reverse else col <= row).astype(BF16)
        q_half, pair_masks = [], []
        m = 1
        while m < c:
            q_par = 0 if reverse else 1
            q_half.append(((rowd // m) % 2) == q_par)
            pair_masks.append(((row // (2 * m)) == (col // (2 * m)))
                              & (((row // m) % 2) == q_par) & (((col // m) % 2) == 1 - q_par))
            m *= 2

        def head(h, carry, q_ref=q_ref, f_ref=f_ref, v_ref=v_ref, o_ref=o_ref, d=d, reverse=reverse,
                 tri=tri, q_half=q_half, pair_masks=pair_masks):
            cs = pl.ds(pl.multiple_of(h * HGRN_DK, HGRN_DK), HGRN_DK)
            o, st_new = _hgrn_chunk(q_ref[:, cs], f_ref[:, cs], v_ref[:, cs], lb_ref[pl.ds(d, 1), cs],
                                    st_scr[d, h], tri, eye, q_half, pair_masks, reverse, c)
            o_ref[:, cs] = o
            st_scr[d, h] = st_new
            return carry

        lax.fori_loop(0, HGRN_HEADS, head, 0, unroll=True)

    @pl.when(first_ref[jnp.minimum(s + 1, n_last)] == 1)
    def _():
        for d in range(2):
            for h in range(HGRN_HEADS):
                sout_ref[0, d, h] = st_scr[d, h].T


def _hgrn_scan(z2d, lb2, s0_all, seq_rows, c):
    t = z2d.shape[0]
    rbf, rbb, first, seq = [], [], [], []
    for si, (r0, length) in enumerate(seq_rows):
        n = length // c
        for ci in range(n):
            rbf.append(r0 // c + ci)
            rbb.append(r0 // c + n - 1 - ci)
            first.append(1 if ci == 0 else 0)
            seq.append(si)
    n_steps = len(rbf)
    first.append(1)
    w = HGRN_WIDTH

    def zspec(which, colblk):
        if which == 0:
            return pl.BlockSpec((c, w), lambda s, rbf, rbb, fi, sq: (rbf[s], colblk))
        return pl.BlockSpec((c, w), lambda s, rbf, rbb, fi, sq: (rbb[s], colblk))

    n_seq = len(seq_rows)
    state_spec = pl.BlockSpec((1, 2, HGRN_HEADS, HGRN_DK, HGRN_DK),
                              lambda s, rbf, rbb, fi, sq: (sq[s], 0, 0, 0, 0))
    grid_spec = pltpu.PrefetchScalarGridSpec(
        num_scalar_prefetch=4,
        grid=(n_steps,),
        in_specs=[zspec(0, 0), zspec(0, 1), zspec(0, 3), zspec(1, 0), zspec(1, 2), zspec(1, 3),
                  pl.BlockSpec((2, w), lambda s, *_: (0, 0)), state_spec],
        out_specs=[zspec(0, 0), zspec(1, 0), state_spec],
        scratch_shapes=[pltpu.VMEM((2, HGRN_HEADS, HGRN_DK, HGRN_DK), F32)],
    )
    return pl.pallas_call(
        functools.partial(_hgrn_kernel, c=c, n_last=n_steps),
        grid_spec=grid_spec,
        out_shape=[jax.ShapeDtypeStruct((t, w), F32), jax.ShapeDtypeStruct((t, w), F32),
                   jax.ShapeDtypeStruct((n_seq, 2, HGRN_HEADS, HGRN_DK, HGRN_DK), F32)],
        compiler_params=_params("arbitrary"),
        name="hgrn_scan",
    )(jnp.asarray(rbf, jnp.int32), jnp.asarray(rbb, jnp.int32), jnp.asarray(first, jnp.int32),
      jnp.asarray(seq, jnp.int32), z2d, z2d, z2d, z2d, z2d, z2d, lb2, s0_all)


def _qk_prep_kernel(x_ref, g_ref, cos_ref, sin_ref, swap_ref, o_ref, *, rope, scale):
    x = x_ref[0]
    ms = jnp.mean(x * x, axis=-1, keepdims=True)
    y = x * lax.rsqrt(ms + EPS) * g_ref[...]
    if rope:
        y_sw = _dot(y.astype(BF16), swap_ref[...])
        y = y * cos_ref[...] + y_sw * sin_ref[...]
    o_ref[0] = (y * scale).astype(o_ref.dtype)


def _qk_prep(x, gain, cos, sin, rope, scale, out_dtype):
    r, length, hd = x.shape
    tl = _tile(length, 1024)
    swap = np.zeros((hd, hd), np.float32)
    for i in range(hd):
        swap[(i + hd // 2) % hd, i] = 1.0
    return pl.pallas_call(
        functools.partial(_qk_prep_kernel, rope=rope, scale=scale),
        grid=(r, length // tl),
        in_specs=[
            pl.BlockSpec((1, tl, hd), lambda i, j: (i, j, 0)),
            pl.BlockSpec((1, hd), lambda i, j: (0, 0)),
            pl.BlockSpec((tl, hd), lambda i, j: (j, 0)),
            pl.BlockSpec((tl, hd), lambda i, j: (j, 0)),
            pl.BlockSpec((hd, hd), lambda i, j: (0, 0)),
        ],
        out_specs=pl.BlockSpec((1, tl, hd), lambda i, j: (i, j, 0)),
        out_shape=jax.ShapeDtypeStruct((r, length, hd), out_dtype),
        compiler_params=_params("parallel", "parallel"),
        name="qk_prep",
    )(x, gain.reshape(1, hd), cos, sin, jnp.asarray(swap, BF16))


def _attn_kernel(q_ref, k_ref, v_ref, o_ref, *, ts, n_s, tq):
    rows = GROUP * tq
    q = q_ref[0, 0].reshape(rows, HEAD_DIM)

    def body(i, carry):
        m, acc = carry
        sl = pl.ds(pl.multiple_of(i * ts, ts), ts)
        s = _dot_nt(q, k_ref[0, 0, sl, :])
        m_new = jnp.maximum(m, jnp.max(s, axis=-1, keepdims=True))
        p = jnp.exp2(s - m_new)
        acc = jnp.exp2(m - m_new) * acc + _dot(p.astype(BF16), v_ref[0, 0, sl, :])
        return m_new, acc

    m0 = jnp.full((rows, 1), NEG_INF, F32)
    acc0 = jnp.zeros((rows, 2 * HEAD_DIM), F32)
    m, acc = lax.fori_loop(0, n_s, body, (m0, acc0), unroll=True)
    o_ref[0, 0] = (acc[:, :HEAD_DIM] / acc[:, HEAD_DIM:HEAD_DIM + 1]).reshape(GROUP, tq, HEAD_DIM)


def _attention(q, k, v_aug):
    b, kvh, g, length, hd = q.shape
    s_len = k.shape[2]
    tq = _tile(length, ATTN_TQ)
    ts = _tile(s_len, ATTN_TS)
    return pl.pallas_call(
        functools.partial(_attn_kernel, ts=ts, n_s=s_len // ts, tq=tq),
        grid=(b, kvh, length // tq),
        in_specs=[
            pl.BlockSpec((1, 1, g, tq, hd), lambda bi, hi, qi: (bi, hi, 0, qi, 0)),
            pl.BlockSpec((1, 1, s_len, hd), lambda bi, hi, qi: (bi, hi, 0, 0)),
            pl.BlockSpec((1, 1, s_len, 2 * hd), lambda bi, hi, qi: (bi, hi, 0, 0)),
        ],
        out_specs=pl.BlockSpec((1, 1, g, tq, hd), lambda bi, hi, qi: (bi, hi, 0, qi, 0)),
        out_shape=jax.ShapeDtypeStruct(q.shape, F32),
        compiler_params=_params("parallel", "parallel", "parallel"),
        name="attention",
    )(q, k, v_aug)


def _layer_norm(t, g, b):
    mu = jnp.mean(t, axis=-1, keepdims=True)
    tc = t - mu
    var = jnp.mean(tc * tc, axis=-1, keepdims=True)
    return tc * lax.rsqrt(var + EPS) * g + b


def _merge_kernel(of_ref, ob_ref, hg_ref, ga_ref, gb_ref, oatt_ref, x_ref, g1_ref, sh2_ref, sc2_ref,
                  wrec_ref, watt_ref, wout_ref, ng_ref, lng_ref, lnb_ref, x1_ref, h2_ref, orec_scr):
    for h in range(HGRN_HEADS):
        cs = slice(h * HGRN_DK, (h + 1) * HGRN_DK)
        o = of_ref[:, cs] + ob_ref[:, cs]
        ms = jnp.mean(o * o, axis=-1, keepdims=True)
        zg = hg_ref[:, cs]
        orec_scr[:, cs] = (o * lax.rsqrt(ms + EPS) * ng_ref[...] * (zg * _sigmoid(zg))).astype(BF16)
    y = (_sigmoid(ga_ref[...]) * _dot(orec_scr[...], wrec_ref[...])
         + _sigmoid(gb_ref[...]) * _dot(oatt_ref[...].astype(BF16), watt_ref[...]))
    u = _dot(y.astype(BF16), wout_ref[...])
    x1 = _layer_norm(DEEPNORM_ALPHA * x_ref[...] + g1_ref[0] * u, lng_ref[...], lnb_ref[...])
    x1_ref[...] = x1
    h2_ref[...] = x1 * (1.0 + sc2_ref[0]) + sh2_ref[0]


def _merge(o_f, o_b, z2d, o_att, x2d, mod_l, w_rec, w_att, w_out, norm_g, ln_g, ln_b, rows_of_tile, tm):
    t, d = x2d.shape
    mod3 = mod_l.reshape(8, 1, 6 * d)

    def rows(colblk):
        return pl.BlockSpec((tm, d), lambda i: (i, colblk))

    def modspec(chunk):
        return pl.BlockSpec((1, 1, d), lambda i: (rows_of_tile(i), 0, chunk))

    def full(shape):
        return pl.BlockSpec(shape, lambda i: (0,) * len(shape))

    return pl.pallas_call(
        _merge_kernel,
        grid=(t // tm,),
        in_specs=[rows(0), rows(0), rows(4), rows(COL_GA // d), rows(COL_GB // d), rows(0), rows(0),
                  modspec(2), modspec(3), modspec(4),
                  full((d, d)), full((d, d)), full((d, d)), full((1, HGRN_DK)), full((1, d)), full((1, d))],
        out_specs=[rows(0), rows(0)],
        out_shape=[jax.ShapeDtypeStruct((t, d), F32), jax.ShapeDtypeStruct((t, d), F32)],
        scratch_shapes=[pltpu.VMEM((tm, d), BF16)],
        compiler_params=_params("parallel"),
        name="merge",
    )(o_f, o_b, z2d, z2d, z2d, o_att, x2d, mod3, mod3, mod3, w_rec, w_att, w_out,
      norm_g.reshape(1, HGRN_DK), ln_g.reshape(1, d), ln_b.reshape(1, d))


def _top16(s, n):
    iota = lax.broadcasted_iota(jnp.int32, s.shape, 0).astype(F32)
    vals, idxs = [], []
    for _ in range(PEER_TOPK):
        m = jnp.max(s, axis=0, keepdims=True)
        idx = jnp.min(jnp.where(s == m, iota, float(n)), axis=0, keepdims=True)
        vals.append(m)
        idxs.append(idx)
        s = jnp.where(iota == idx, NEG_INF, s)
    return vals, idxs


def _candidate_positions():
    rows = []
    for a in range(4):
        for b in range(16 if a == 0 else 8):
            rows.append((a, b, (a + 1) * (b + 1) <= PEER_TOPK))
    for b, n_a in ((0, 16), (1, 8), (2, 8)):
        for a in range(n_a):
            rows.append((a, b, a >= 4 and (a + 1) * (b + 1) <= PEER_TOPK))
    return np.array([[a * PEER_TOPK + b if ok else -1.0] for a, b, ok in rows], np.float32)


def _peer_topk_kernel(h2_ref, wq_ref, keys_ref, pos_ref, ida_ref, idb_ref, gate_ref):
    qf = _dot(h2_ref[...].astype(BF16), wq_ref[...])
    k = PEER_TOPK
    tb = qf.shape[0]
    posb = jnp.broadcast_to(pos_ref[...], (pos_ref.shape[0], tb))
    valid = posb >= 0.0
    for h in range(PEER_HEADS):
        tops = []
        for p in range(2):
            seg = (h * 2 + p) * PEER_HALF
            st = _dot_nt(keys_ref[h, p], qf[:, seg:seg + PEER_HALF].astype(BF16))
            tops.append(_top16(st, N_KEYS))
        (v0, i0), (v1, i1) = tops
        v0c = jnp.concatenate(v0, axis=0)
        v1c = jnp.concatenate(v1, axis=0)
        cand = jnp.concatenate([v0[0] + v1c, v0[1] + v1c[:8], v0[2] + v1c[:8], v0[3] + v1c[:8],
                                v0c + v1[0], v0c[:8] + v1[1], v0c[:8] + v1[2]], axis=0)
        cand = jnp.where(valid, cand, NEG_INF)
        best, poss = [], []
        for _ in range(k):
            m = jnp.max(cand, axis=0, keepdims=True)
            pos = jnp.min(jnp.where(cand == m, posb, float(k * k)), axis=0, keepdims=True)
            best.append(m)
            poss.append(pos)
            cand = jnp.where(posb == pos, NEG_INF, cand)
        best = jnp.concatenate(best, axis=0)
        posc = jnp.concatenate(poss, axis=0)
        a_rank = jnp.floor(posc * (1.0 / k))
        b_rank = posc - k * a_rank
        ea = jnp.zeros_like(posc)
        eb = jnp.zeros_like(posc)
        for r in range(k):
            ea = jnp.where(a_rank == float(r), i0[r], ea)
            eb = jnp.where(b_rank == float(r), i1[r], eb)
        e = jnp.exp(best - best[0:1, :])
        gate_ref[h * k:(h + 1) * k, :] = e / jnp.sum(e, axis=0, keepdims=True)
        ida_ref[h * k:(h + 1) * k, :] = ea
        idb_ref[h * k:(h + 1) * k, :] = eb


def _peer_topk(h2, wq_bf, keys_bf, tb):
    t, d = h2.shape
    nq = wq_bf.shape[1]
    rows = PEER_HEADS * PEER_TOPK
    out = jax.ShapeDtypeStruct((rows, t), F32)
    ospec = pl.BlockSpec((rows, tb), lambda i: (0, i))
    pos = _candidate_positions()
    return pl.pallas_call(
        _peer_topk_kernel,
        grid=(t // tb,),
        in_specs=[
            pl.BlockSpec((tb, d), lambda i: (i, 0)),
            pl.BlockSpec((d, nq), lambda i: (0, 0)),
            pl.BlockSpec((PEER_HEADS, 2, N_KEYS, PEER_HALF), lambda i: (0, 0, 0, 0)),
            pl.BlockSpec(pos.shape, lambda i: (0, 0)),
        ],
        out_specs=[ospec, ospec, ospec],
        out_shape=[out, out, out],
        compiler_params=_params("parallel"),
        name="peer_topk",
    )(h2, wq_bf, keys_bf, jnp.asarray(pos))


RSQRT2 = 1.0 / math.sqrt(2.0)


def _peer_mix_kernel(h2_ref, ida_ref, idb_ref, gate_ref, ut_ref, v_ref, x1_ref, g2_ref, lng_ref, lnb_ref,
                     x2_ref, w_scr, p_scr, acc_scr, h2_scr, *, tb, te, pitch):
    j = pl.program_id(1)
    rows_per_tile = te // N_KEYS

    @pl.when(j == 0)
    def _():
        acc_scr[...] = jnp.zeros_like(acc_scr)
        h2_scr[...] = h2_ref[...].astype(BF16)
        iota = lax.broadcasted_iota(jnp.int32, (N_KEYS, PEER_HEADS * PEER_TOPK), 0).astype(F32)

        def tok(t, carry):
            a_row = ida_ref[pl.ds(t, 1), :]
            b_row = idb_ref[pl.ds(t, 1), :]
            g_row = gate_ref[pl.ds(t, 1), :] * RSQRT2
            at = jnp.where(iota == a_row, 1.0, 0.0).astype(BF16)
            gbt = jnp.where(iota == b_row, g_row, 0.0).astype(BF16)
            w_scr[pl.ds(t, N_KEYS, stride=pitch), :] = _dot_nt(at, gbt)
            return carry

        lax.fori_loop(0, tb, tok, 0, unroll=8)

    a = _dot(h2_scr[...], ut_ref[...])
    for r in range(rows_per_tile):
        w = w_scr[pl.ds(pl.multiple_of((j * rows_per_tile + r) * pitch, 8), tb), :]
        cs = slice(r * N_KEYS, (r + 1) * N_KEYS)
        ar = a[:, cs]
        p_scr[:, cs] = (w * (ar * (1.0 + lax.erf(ar)))).astype(BF16)
    acc_scr[...] += _dot(p_scr[...], v_ref[...])

    @pl.when(j == pl.num_programs(1) - 1)
    def _():
        x2_ref[...] = _layer_norm(DEEPNORM_ALPHA * x1_ref[...] + g2_ref[0] * acc_scr[...],
                                  lng_ref[...], lnb_ref[...])


def _peer_mix(h2, ida, idb, gate, ut_bf, v_bf, x1, mod_l, ln_g, ln_b, rows_of_tile, tb):
    t, d = x1.shape
    n_e = v_bf.shape[0]
    te = 2048
    hk = PEER_HEADS * PEER_TOPK
    pitch = tb + 8
    mod3 = mod_l.reshape(8, 1, 6 * d)
    return pl.pallas_call(
        functools.partial(_peer_mix_kernel, tb=tb, te=te, pitch=pitch),
        grid=(t // tb, n_e // te),
        in_specs=[
            pl.BlockSpec((tb, d), lambda i, j: (i, 0)),
            pl.BlockSpec((tb, hk), lambda i, j: (i, 0)),
            pl.BlockSpec((tb, hk), lambda i, j: (i, 0)),
            pl.BlockSpec((tb, hk), lambda i, j: (i, 0)),
            pl.BlockSpec((d, te), lambda i, j: (0, j)),
            pl.BlockSpec((te, d), lambda i, j: (j, 0)),
            pl.BlockSpec((tb, d), lambda i, j: (i, 0)),
            pl.BlockSpec((1, 1, d), lambda i, j: (rows_of_tile(i), 0, 5)),
            pl.BlockSpec((1, d), lambda i, j: (0, 0)),
            pl.BlockSpec((1, d), lambda i, j: (0, 0)),
        ],
        out_specs=pl.BlockSpec((tb, d), lambda i, j: (i, 0)),
        out_shape=jax.ShapeDtypeStruct((t, d), F32),
        scratch_shapes=[pltpu.VMEM((pitch * N_KEYS, N_KEYS), F32), pltpu.VMEM((tb, te), BF16),
                        pltpu.VMEM((tb, d), F32), pltpu.VMEM((tb, d), BF16)],
        compiler_params=_params("parallel", "arbitrary"),
        name="peer_mix",
    )(h2, ida, idb, gate, ut_bf, v_bf, x1, mod3, ln_g.reshape(1, d), ln_b.reshape(1, d))


def _rope_tables(length):
    rows = length // GRID_W
    row = jnp.broadcast_to(jnp.arange(rows, dtype=F32)[:, None], (rows, GRID_W)).reshape(length)
    col = jnp.broadcast_to(jnp.arange(GRID_W, dtype=F32)[None, :], (rows, GRID_W)).reshape(length)
    n_freq = HEAD_DIM // 4
    inv = jnp.power(ROPE_THETA, -jnp.arange(n_freq, dtype=F32) / n_freq)
    ang = jnp.concatenate([row[:, None] * inv, col[:, None] * inv], axis=-1)
    cos, sin = jnp.cos(ang), jnp.sin(ang)
    return jnp.concatenate([cos, cos], axis=-1), jnp.concatenate([-sin, sin], axis=-1)


def kernel(x_prompt, x_sample, cache_attn_k, cache_attn_v, state_hgrn, c, c_ctx, w_mod, b_mod, w_in,
           hgrn_lb_logits, hgrn_norm_g, q_norm_g, k_norm_g, w_branch_rec, w_branch_att, w_out,
           ln1_g, ln1_b, ln2_g, ln2_b, peer_w_query, peer_sub_keys, peer_u, peer_v):
    nb, seq, d = x_prompt.shape
    db, dseq, _ = x_sample.shape
    depth = w_in.shape[0]
    past = cache_attn_k.shape[2]
    t_ctx, t_lat = nb * seq, db * dseq
    t_all = t_ctx + t_lat
    assert d == D_MODEL and 1 + db <= 8

    tm = _tile(math.gcd(t_ctx, dseq), 1024)
    tm_merge = _tile(math.gcd(t_ctx, dseq), 256)
    tb_peer = _tile(math.gcd(t_ctx, dseq), 256)

    def rows_of_tile_fn(tile):
        n_ctx = t_ctx // tile
        per = dseq // tile
        return lambda i: jnp.where(i < n_ctx, 0, 1 + (i - n_ctx) // per)

    cond8 = jnp.concatenate([c_ctx[None, :], c, jnp.zeros((8 - 1 - db, d), F32)], axis=0)
    mod = _modulation(cond8, w_mod, b_mod)

    lb = jnp.cumsum(jax.nn.softmax(hgrn_lb_logits.astype(F32), axis=1), axis=1)
    lb = lb - lb[:, :1]

    segs = jnp.split(w_in, np.cumsum([HGRN_WIDTH] * 5 + [ATTN_WIDTH, KV_WIDTH, KV_WIDTH, D_MODEL]).tolist(), axis=-1)
    hq_w, hff_w, hfb_w, hi_w, hg_w, aq_w, ak_w, av_w, ga_w, gb_w = segs
    w_in_bf = jnp.concatenate([hq_w, hff_w, hfb_w, hi_w, hg_w, aq_w, ga_w, gb_w, ak_w, av_w], axis=-1).astype(BF16)
    w_rec_bf, w_att_bf, w_out_bf = (w.astype(BF16) for w in (w_branch_rec, w_branch_att, w_out))
    wq_bf = peer_w_query.astype(BF16)
    keys_bf = peer_sub_keys.astype(BF16)
    ut_bf = (jnp.swapaxes(peer_u, 1, 2) * RSQRT2).astype(BF16)
    v_bf = peer_v.astype(BF16)

    cos_t, sin_t = _rope_tables(dseq)
    ones_t, zeros_t = jnp.ones((seq, HEAD_DIM), F32), jnp.zeros((seq, HEAD_DIM), F32)

    seq_rows = [(i * seq, seq) for i in range(nb)] + [(t_ctx + i * dseq, dseq) for i in range(db)]
    zero_state = jnp.zeros((nb, 2, HGRN_HEADS, HGRN_DK, HGRN_DK), F32)

    x2d = jnp.concatenate([x_prompt.reshape(t_ctx, d), x_sample.reshape(t_lat, d)], axis=0)
    new_k, new_v, new_s = [], [], []
    for l in range(depth):
        z2d = _in_proj(x2d, mod[l], w_in_bf[l], rows_of_tile_fn(tm), tm)

        s0_all = jnp.concatenate([zero_state, state_hgrn[:, l]], axis=0)
        o_f, o_b, s_fin = _hgrn_scan(z2d, lb[:, l], s0_all, seq_rows, HGRN_CHUNK)
        new_s.append(s_fin[:nb])

        aq, ak, av = z2d[:, COL_AQ:COL_GA], z2d[:, COL_AK:COL_AV], z2d[:, COL_AV:]

        def heads_major(a2d, b, length, nh):
            return jnp.transpose(a2d.reshape(b, length, nh, HEAD_DIM), (0, 2, 1, 3))

        def with_ones(v):
            pad = jnp.zeros(v.shape[:-1] + (HEAD_DIM,), F32).at[..., 0].set(1.0)
            return jnp.concatenate([v, pad], axis=-1).astype(BF16)

        q_scale = ATTN_SCALE * LOG2_E
        q_c = heads_major(aq[:t_ctx], nb, seq, ATTN_HEADS).reshape(nb * ATTN_HEADS, seq, HEAD_DIM)
        k_c = heads_major(ak[:t_ctx], nb, seq, KV_HEADS).reshape(nb * KV_HEADS, seq, HEAD_DIM)
        v_c = heads_major(av[:t_ctx], nb, seq, KV_HEADS)
        q_c = _qk_prep(q_c, q_norm_g[l], ones_t, zeros_t, False, q_scale, BF16)
        k_c = _qk_prep(k_c, k_norm_g[l], ones_t, zeros_t, False, 1.0, F32).reshape(nb, KV_HEADS, seq, HEAD_DIM)
        o_c = _attention(q_c.reshape(nb, KV_HEADS, GROUP, seq, HEAD_DIM), k_c.astype(BF16), with_ones(v_c))
        new_k.append(jnp.transpose(k_c, (0, 2, 1, 3)))
        new_v.append(av[:t_ctx].reshape(nb, seq, KV_HEADS, HEAD_DIM))

        q_s = heads_major(aq[t_ctx:], db, dseq, ATTN_HEADS).reshape(db * ATTN_HEADS, dseq, HEAD_DIM)
        k_s = heads_major(ak[t_ctx:], db, dseq, KV_HEADS).reshape(db * KV_HEADS, dseq, HEAD_DIM)
        v_s = heads_major(av[t_ctx:], db, dseq, KV_HEADS)
        q_s = _qk_prep(q_s, q_norm_g[l], cos_t, sin_t, True, q_scale, BF16)
        k_s = _qk_prep(k_s, k_norm_g[l], cos_t, sin_t, True, 1.0, BF16).reshape(db, KV_HEADS, dseq, HEAD_DIM)
        k_s = jnp.concatenate([k_s, jnp.transpose(cache_attn_k[:, l], (0, 2, 1, 3)).astype(BF16)], axis=2)
        v_s = jnp.concatenate([v_s, jnp.transpose(cache_attn_v[:, l], (0, 2, 1, 3))], axis=2)
        o_s = _attention(q_s.reshape(db, KV_HEADS, GROUP, dseq, HEAD_DIM), k_s, with_ones(v_s))

        def token_major(o, b, length):
            return jnp.transpose(o.reshape(b, ATTN_HEADS, length, HEAD_DIM), (0, 2, 1, 3)).reshape(b * length, ATTN_WIDTH)

        o_att = jnp.concatenate([token_major(o_c, nb, seq), token_major(o_s, db, dseq)], axis=0)

        x1, h2 = _merge(o_f, o_b, z2d, o_att, x2d, mod[l], w_rec_bf[l], w_att_bf[l], w_out_bf[l],
                        hgrn_norm_g[l], ln1_g[l], ln1_b[l], rows_of_tile_fn(tm_merge), tm_merge)

        ida, idb, gate = _peer_topk(h2, wq_bf[l], keys_bf[l], tb_peer)
        x2d = _peer_mix(h2, ida.T, idb.T, gate.T, ut_bf[l], v_bf[l], x1, mod[l], ln2_g[l], ln2_b[l],
                        rows_of_tile_fn(tb_peer), tb_peer)

    y_prompt = x2d[:t_ctx].reshape(nb, seq, d)
    y_sample = x2d[t_ctx:].reshape(db, dseq, d)
    return (y_prompt, y_sample, jnp.stack(new_k, axis=1), jnp.stack(new_v, axis=1), jnp.stack(new_s, axis=1))
```

```python
import functools
import math

import numpy as np
import jax
import jax.numpy as jnp
from jax import lax
from jax.experimental import pallas as pl
from jax.experimental.pallas import tpu as pltpu

F32 = jnp.float32
BF16 = jnp.bfloat16

D_MODEL = 1024
GRID_W = 64
HGRN_HEADS = 8
HGRN_DK = 128
HGRN_WIDTH = HGRN_HEADS * HGRN_DK
HGRN_CHUNK = 64
ATTN_HEADS = 16
KV_HEADS = 4
HEAD_DIM = 64
GROUP = ATTN_HEADS // KV_HEADS
ATTN_WIDTH = ATTN_HEADS * HEAD_DIM
KV_WIDTH = KV_HEADS * HEAD_DIM
ATTN_SCALE = HEAD_DIM ** -0.5
ROPE_THETA = 10000.0
PEER_HEADS = 8
PEER_HALF = 128
N_KEYS = 128
N_EXPERTS = N_KEYS * N_KEYS
PEER_TOPK = 16
MODEL_DEPTH = 4
DEEPNORM_ALPHA = (2 * MODEL_DEPTH) ** 0.25
EPS = 1e-6
NEG_INF = float("-inf")

IN_COLS = 5 * HGRN_WIDTH + ATTN_WIDTH + 2 * D_MODEL + 2 * KV_WIDTH
COL_AQ = 5 * HGRN_WIDTH
COL_GA = COL_AQ + ATTN_WIDTH
COL_GB = COL_GA + D_MODEL
COL_AK = COL_GB + D_MODEL
COL_AV = COL_AK + KV_WIDTH

VMEM_LIMIT = 56 * 1024 * 1024
LOG2_E = math.log2(math.e)
ATTN_TQ = 256
ATTN_TS = 512


def _params(*sem):
    return pltpu.CompilerParams(dimension_semantics=sem, vmem_limit_bytes=VMEM_LIMIT)


def _tile(n, pref):
    t = min(n, pref)
    while n % t or t % 8:
        t -= 1
    return t


def _sigmoid(x):
    return 1.0 / (1.0 + jnp.exp(-x))


def _dot(a, b):
    return jnp.dot(a, b, preferred_element_type=F32)


def _dot_nt(a, b):
    return lax.dot_general(a, b, (((1,), (1,)), ((), ())), preferred_element_type=F32)


def _mod_kernel(cond_ref, w_ref, b_ref, o_ref):
    c = cond_ref[...]
    s = c * _sigmoid(c)
    o_ref[0] = jnp.dot(s, w_ref[0], precision=lax.Precision.HIGHEST, preferred_element_type=F32) + b_ref[0]


def _modulation(cond8, w_mod, b_mod):
    depth, d, n = w_mod.shape
    tn = _tile(n, 1536)
    return pl.pallas_call(
        _mod_kernel,
        grid=(depth, n // tn),
        in_specs=[
            pl.BlockSpec((8, d), lambda l, j: (0, 0)),
            pl.BlockSpec((1, d, tn), lambda l, j: (l, 0, j)),
            pl.BlockSpec((1, 1, tn), lambda l, j: (l, 0, j)),
        ],
        out_specs=pl.BlockSpec((1, 8, tn), lambda l, j: (l, 0, j)),
        out_shape=jax.ShapeDtypeStruct((depth, 8, n), F32),
        compiler_params=_params("parallel", "parallel"),
        name="modulation",
    )(cond8, w_mod, b_mod.reshape(depth, 1, n))


def _inproj_kernel(x_ref, sh_ref, sc_ref, w_ref, z_ref, h_scr):
    @pl.when(pl.program_id(1) == 0)
    def _():
        h_scr[...] = (x_ref[...] * (1.0 + sc_ref[0]) + sh_ref[0]).astype(BF16)

    z_ref[...] = _dot(h_scr[...], w_ref[...])


def _in_proj(x2d, mod_l, w_in_bf, rows_of_tile, tm):
    t, d = x2d.shape
    n = w_in_bf.shape[1]
    tn = 512
    mod3 = mod_l.reshape(8, 1, 6 * d)
    return pl.pallas_call(
        _inproj_kernel,
        grid=(t // tm, n // tn),
        in_specs=[
            pl.BlockSpec((tm, d), lambda i, j: (i, 0)),
            pl.BlockSpec((1, 1, d), lambda i, j: (rows_of_tile(i), 0, 0)),
            pl.BlockSpec((1, 1, d), lambda i, j: (rows_of_tile(i), 0, 1)),
            pl.BlockSpec((d, tn), lambda i, j: (0, j)),
        ],
        out_specs=pl.BlockSpec((tm, tn), lambda i, j: (i, j)),
        out_shape=jax.ShapeDtypeStruct((t, n), F32),
        scratch_shapes=[pltpu.VMEM((tm, d), BF16)],
        compiler_params=_params("parallel", "arbitrary"),
        name="in_proj",
    )(x2d, mod3, mod3, w_in_bf)


def _level_reference_rows(beta, c):
    sub = lax.broadcasted_iota(jnp.int32, (8, HGRN_DK), 0)

    def bc(i, n=8):
        return jnp.broadcast_to(beta[i:i + 1, :], (n, HGRN_DK))

    out = []
    m = 1
    while m < c:
        blocks = []
        if m >= 8:
            for blk in range(c // (2 * m)):
                blocks.append(bc(blk * 2 * m + m, 2 * m))
        elif m == 4:
            for v in range(c // 8):
                blocks.append(bc(8 * v + 4))
        elif m == 2:
            for v in range(c // 8):
                blocks.append(jnp.where(sub < 4, bc(8 * v + 2), bc(8 * v + 6)))
        else:
            for v in range(c // 8):
                lo = jnp.where(sub < 2, bc(8 * v + 1), bc(8 * v + 3))
                hi = jnp.where(sub < 6, bc(8 * v + 5), bc(8 * v + 7))
                blocks.append(jnp.where(sub < 4, lo, hi))
        out.append(blocks[0] if len(blocks) == 1 else jnp.concatenate(blocks, axis=0))
        m *= 2
    return out


def _hgrn_chunk(zq, zf, v, lb, st, tri, eye, q_half, pair_masks, reverse, c):
    q = zq * _sigmoid(zq)
    sig = _sigmoid(zf)
    f = lb + (1.0 - lb) * sig
    k = (1.0 - lb) * (1.0 - sig)
    g = jnp.log(f)
    g1 = g.astype(BF16)
    r1 = g - g1.astype(F32)
    g2 = r1.astype(BF16)
    g3 = (r1 - g2.astype(F32)).astype(BF16)
    beta3 = _dot(tri, jnp.concatenate([g1, g2, g3], axis=1))
    beta = beta3[:, :HGRN_DK] + beta3[:, HGRN_DK:2 * HGRN_DK] + beta3[:, 2 * HGRN_DK:]
    last = 0 if reverse else c - 1
    beta_tot = beta[last:last + 1, :]

    a = jnp.where(eye, _dot_nt(q.astype(BF16), k.astype(BF16)), 0.0)
    for r, qh, pm in zip(_level_reference_rows(beta, c), q_half, pair_masks):
        e = jnp.exp(-jnp.abs(beta - r))
        u = (jnp.where(qh, q, k) * e).astype(BF16)
        a = a + jnp.where(pm, _dot_nt(u, u), 0.0)

    v_bf = v.astype(BF16)
    qb = (q * jnp.exp(beta)).astype(BF16)
    o = _dot(a.astype(BF16), v_bf) + _dot_nt(qb, st.astype(BF16))
    kb = (k * jnp.exp(beta_tot - beta)).astype(BF16)
    st_new = st * jnp.exp(beta_tot) + lax.dot_general(
        v_bf, kb, (((0,), (0,)), ((), ())), preferred_element_type=F32)
    return o, st_new


def _hgrn_kernel(rbf_ref, rbb_ref, first_ref, seq_ref,
                 qf_ref, ff_ref, vf_ref, qb_ref, fb_ref, vb_ref, lb_ref, s0_ref,
                 of_ref, ob_ref, sout_ref, st_scr, *, c, n_last):
    s = pl.program_id(0)

    @pl.when(first_ref[s] == 1)
    def _():
        for d in range(2):
            for h in range(HGRN_HEADS):
                st_scr[d, h] = s0_ref[0, d, h].T

    row = lax.broadcasted_iota(jnp.int32, (c, c), 0)
    col = lax.broadcasted_iota(jnp.int32, (c, c), 1)
    rowd = lax.broadcasted_iota(jnp.int32, (c, HGRN_DK), 0)
    eye = row == col

    for d, (q_ref, f_ref, v_ref, o_ref) in enumerate(
            ((qf_ref, ff_ref, vf_ref, of_ref), (qb_ref, fb_ref, vb_ref, ob_ref))):
        reverse = d == 1
        tri = (col >= row if reverse else col <= row).astype(BF16)
        q_half, pair_masks = [], []
        m = 1
        while m < c:
            q_par = 0 if reverse else 1
            q_half.append(((rowd // m) % 2) == q_par)
            pair_masks.append(((row // (2 * m)) == (col // (2 * m)))
                              & (((row // m) % 2) == q_par) & (((col // m) % 2) == 1 - q_par))
            m *= 2

        def head(h, carry, q_ref=q_ref, f_ref=f_ref, v_ref=v_ref, o_ref=o_ref, d=d, reverse=reverse,
                 tri=tri, q_half=q_half, pair_masks=pair_masks):
            cs = pl.ds(pl.multiple_of(h * HGRN_DK, HGRN_DK), HGRN_DK)
            o, st_new = _hgrn_chunk(q_ref[:, cs], f_ref[:, cs], v_ref[:, cs], lb_ref[pl.ds(d, 1), cs],
                                    st_scr[d, h], tri, eye, q_half, pair_masks, reverse, c)
            o_ref[:, cs] = o
            st_scr[d, h] = st_new
            return carry

        lax.fori_loop(0, HGRN_HEADS, head, 0, unroll=True)

    @pl.when(first_ref[jnp.minimum(s + 1, n_last)] == 1)
    def _():
        for d in range(2):
            for h in range(HGRN_HEADS):
                sout_ref[0, d, h] = st_scr[d, h].T


def _hgrn_scan(z2d, lb2, s0_all, seq_rows, c):
    t = z2d.shape[0]
    rbf, rbb, first, seq = [], [], [], []
    for si, (r0, length) in enumerate(seq_rows):
        n = length // c
        for ci in range(n):
            rbf.append(r0 // c + ci)
            rbb.append(r0 // c + n - 1 - ci)
            first.append(1 if ci == 0 else 0)
            seq.append(si)
    n_steps = len(rbf)
    first.append(1)
    w = HGRN_WIDTH

    def zspec(which, colblk):
        if which == 0:
            return pl.BlockSpec((c, w), lambda s, rbf, rbb, fi, sq: (rbf[s], colblk))
        return pl.BlockSpec((c, w), lambda s, rbf, rbb, fi, sq: (rbb[s], colblk))

    n_seq = len(seq_rows)
    state_spec = pl.BlockSpec((1, 2, HGRN_HEADS, HGRN_DK, HGRN_DK),
                              lambda s, rbf, rbb, fi, sq: (sq[s], 0, 0, 0, 0))
    grid_spec = pltpu.PrefetchScalarGridSpec(
        num_scalar_prefetch=4,
        grid=(n_steps,),
        in_specs=[zspec(0, 0), zspec(0, 1), zspec(0, 3), zspec(1, 0), zspec(1, 2), zspec(1, 3),
                  pl.BlockSpec((2, w), lambda s, *_: (0, 0)), state_spec],
        out_specs=[zspec(0, 0), zspec(1, 0), state_spec],
        scratch_shapes=[pltpu.VMEM((2, HGRN_HEADS, HGRN_DK, HGRN_DK), F32)],
    )
    return pl.pallas_call(
        functools.partial(_hgrn_kernel, c=c, n_last=n_steps),
        grid_spec=grid_spec,
        out_shape=[jax.ShapeDtypeStruct((t, w), F32), jax.ShapeDtypeStruct((t, w), F32),
                   jax.ShapeDtypeStruct((n_seq, 2, HGRN_HEADS, HGRN_DK, HGRN_DK), F32)],
        compiler_params=_params("arbitrary"),
        name="hgrn_scan",
    )(jnp.asarray(rbf, jnp.int32), jnp.asarray(rbb, jnp.int32), jnp.asarray(first, jnp.int32),
      jnp.asarray(seq, jnp.int32), z2d, z2d, z2d, z2d, z2d, z2d, lb2, s0_all)


def _qkv_prep_kernel(aq_ref, ak_ref, av_ref, qg_ref, kg_ref, cos_ref, sin_ref, swap_ref, *out_refs,
                     rope, q_scale, keep_k):
    q_out, k_out, v_out = out_refs[:3]
    hd = HEAD_DIM

    def norm_rope(x, g):
        ms = jnp.mean(x * x, axis=-1, keepdims=True)
        y = x * lax.rsqrt(ms + EPS) * g
        if rope:
            y = y * cos_ref[...] + _dot(y.astype(BF16), swap_ref[...]) * sin_ref[...]
        return y

    for j in range(ATTN_WIDTH // 128):
        pair = aq_ref[:, j * 128:(j + 1) * 128]
        for i in range(2):
            y = norm_rope(pair[:, i * hd:(i + 1) * hd], qg_ref[...])
            q_out[0, 2 * j + i] = (y * q_scale).astype(BF16)
    tl = ak_ref.shape[0]
    ones_col = jnp.where(lax.broadcasted_iota(jnp.int32, (tl, hd), 1) == 0, 1.0, 0.0).astype(BF16)
    for j in range(KV_WIDTH // 128):
        kpair = ak_ref[:, j * 128:(j + 1) * 128]
        vpair = av_ref[:, j * 128:(j + 1) * 128]
        for i in range(2):
            h = 2 * j + i
            y = norm_rope(kpair[:, i * hd:(i + 1) * hd], kg_ref[...])
            k_out[0, h] = y.astype(BF16)
            if keep_k:
                out_refs[3][:, h * hd:(h + 1) * hd] = y
            v_out[0, h, :, :hd] = vpair[:, i * hd:(i + 1) * hd].astype(BF16)
            v_out[0, h, :, hd:] = ones_col


def _qkv_prep(z2d, row0, n_seq, length, q_gain, k_gain, cos, sin, rope, keep_k):
    hd = HEAD_DIM
    tl = _tile(length, 512)
    nt = length // tl
    blk0 = row0 // tl
    swap = np.zeros((hd, hd), np.float32)
    for i in range(hd):
        swap[(i + hd // 2) % hd, i] = 1.0

    def zspec(width, col):
        return pl.BlockSpec((tl, width), lambda b, j: (blk0 + b * nt + j, col // width))

    def full(shape):
        return pl.BlockSpec(shape, lambda b, j: (0,) * len(shape))

    out_specs = [pl.BlockSpec((1, ATTN_HEADS, tl, hd), lambda b, j: (b, 0, j, 0)),
                 pl.BlockSpec((1, KV_HEADS, tl, hd), lambda b, j: (b, 0, j, 0)),
                 pl.BlockSpec((1, KV_HEADS, tl, 2 * hd), lambda b, j: (b, 0, j, 0))]
    out_shape = [jax.ShapeDtypeStruct((n_seq, ATTN_HEADS, length, hd), BF16),
                 jax.ShapeDtypeStruct((n_seq, KV_HEADS, length, hd), BF16),
                 jax.ShapeDtypeStruct((n_seq, KV_HEADS, length, 2 * hd), BF16)]
    if keep_k:
        out_specs.append(pl.BlockSpec((tl, KV_WIDTH), lambda b, j: (b * nt + j, 0)))
        out_shape.append(jax.ShapeDtypeStruct((n_seq * length, KV_WIDTH), F32))
    return pl.pallas_call(
        functools.partial(_qkv_prep_kernel, rope=rope, q_scale=ATTN_SCALE * LOG2_E, keep_k=keep_k),
        grid=(n_seq, nt),
        in_specs=[zspec(ATTN_WIDTH, COL_AQ), zspec(KV_WIDTH, COL_AK), zspec(KV_WIDTH, COL_AV),
                  full((1, hd)), full((1, hd)),
                  pl.BlockSpec((tl, hd), lambda b, j: (j, 0)), pl.BlockSpec((tl, hd), lambda b, j: (j, 0)),
                  full((hd, hd))],
        out_specs=out_specs,
        out_shape=out_shape,
        compiler_params=_params("parallel", "parallel"),
        name="qkv_prep",
    )(z2d, z2d, z2d, q_gain.reshape(1, hd), k_gain.reshape(1, hd), cos, sin, jnp.asarray(swap, BF16))


def _attn_kernel(q_ref, k_ref, v_ref, *rest, ts, n_s, tq):
    o_ref = rest[-1]
    rows = GROUP * tq
    q = q_ref[0].reshape(rows, HEAD_DIM)

    def body(i, carry):
        m, acc = carry
        sl = pl.ds(pl.multiple_of(i * ts, ts), ts)
        s = _dot_nt(q, k_ref[0, 0, sl, :])
        m_new = jnp.maximum(m, jnp.max(s, axis=-1, keepdims=True))
        p = jnp.exp2(s - m_new)
        acc = jnp.exp2(m - m_new) * acc + _dot(p.astype(BF16), v_ref[0, 0, sl, :])
        return m_new, acc

    m0 = jnp.full((rows, 1), NEG_INF, F32)
    acc0 = jnp.zeros((rows, 2 * HEAD_DIM), F32)
    m, acc = lax.fori_loop(0, n_s, body, (m0, acc0), unroll=True)
    o = acc[:, :HEAD_DIM] / acc[:, HEAD_DIM:HEAD_DIM + 1]
    for g in range(GROUP):
        o_ref[:, g * HEAD_DIM:(g + 1) * HEAD_DIM] = o[g * tq:(g + 1) * tq]


def _attention(q, k, v_aug, o_prev, row0, t_all):
    b, _, length, hd = q.shape
    s_len = k.shape[2]
    tq = _tile(length, ATTN_TQ)
    ts = _tile(s_len, ATTN_TS)
    nq = length // tq
    blk0 = row0 // tq
    in_specs = [
        pl.BlockSpec((1, GROUP, tq, hd), lambda bi, hi, qi: (bi, hi, qi, 0)),
        pl.BlockSpec((1, 1, s_len, hd), lambda bi, hi, qi: (bi, hi, 0, 0)),
        pl.BlockSpec((1, 1, s_len, 2 * hd), lambda bi, hi, qi: (bi, hi, 0, 0)),
    ]
    args = [q, k, v_aug]
    aliases = {}
    if o_prev is not None:
        in_specs.append(pl.BlockSpec(memory_space=pl.ANY))
        args.append(o_prev)
        aliases = {3: 0}
    return pl.pallas_call(
        functools.partial(_attn_kernel, ts=ts, n_s=s_len // ts, tq=tq),
        grid=(b, KV_HEADS, nq),
        in_specs=in_specs,
        out_specs=pl.BlockSpec((tq, GROUP * hd), lambda bi, hi, qi: (blk0 + bi * nq + qi, hi)),
        out_shape=jax.ShapeDtypeStruct((t_all, ATTN_WIDTH), F32),
        input_output_aliases=aliases,
        compiler_params=_params("parallel", "parallel", "parallel"),
        name="attention",
    )(*args)


def _layer_norm(t, g, b):
    mu = jnp.mean(t, axis=-1, keepdims=True)
    tc = t - mu
    var = jnp.mean(tc * tc, axis=-1, keepdims=True)
    return tc * lax.rsqrt(var + EPS) * g + b


def _merge_kernel(of_ref, ob_ref, hg_ref, ga_ref, gb_ref, oatt_ref, x_ref, g1_ref, sh2_ref, sc2_ref,
                  wrec_ref, watt_ref, wout_ref, ng_ref, lng_ref, lnb_ref, x1_ref, h2_ref, orec_scr):
    for h in range(HGRN_HEADS):
        cs = slice(h * HGRN_DK, (h + 1) * HGRN_DK)
        o = of_ref[:, cs] + ob_ref[:, cs]
        ms = jnp.mean(o * o, axis=-1, keepdims=True)
        zg = hg_ref[:, cs]
        orec_scr[:, cs] = (o * lax.rsqrt(ms + EPS) * ng_ref[...] * (zg * _sigmoid(zg))).astype(BF16)
    y = (_sigmoid(ga_ref[...]) * _dot(orec_scr[...], wrec_ref[...])
         + _sigmoid(gb_ref[...]) * _dot(oatt_ref[...].astype(BF16), watt_ref[...]))
    u = _dot(y.astype(BF16), wout_ref[...])
    x1 = _layer_norm(DEEPNORM_ALPHA * x_ref[...] + g1_ref[0] * u, lng_ref[...], lnb_ref[...])
    x1_ref[...] = x1
    h2_ref[...] = x1 * (1.0 + sc2_ref[0]) + sh2_ref[0]


def _merge(o_f, o_b, z2d, o_att, x2d, mod_l, w_rec, w_att, w_out, norm_g, ln_g, ln_b, rows_of_tile, tm):
    t, d = x2d.shape
    mod3 = mod_l.reshape(8, 1, 6 * d)

    def rows(colblk):
        return pl.BlockSpec((tm, d), lambda i: (i, colblk))

    def modspec(chunk):
        return pl.BlockSpec((1, 1, d), lambda i: (rows_of_tile(i), 0, chunk))

    def full(shape):
        return pl.BlockSpec(shape, lambda i: (0,) * len(shape))

    return pl.pallas_call(
        _merge_kernel,
        grid=(t // tm,),
        in_specs=[rows(0), rows(0), rows(4), rows(COL_GA // d), rows(COL_GB // d), rows(0), rows(0),
                  modspec(2), modspec(3), modspec(4),
                  full((d, d)), full((d, d)), full((d, d)), full((1, HGRN_DK)), full((1, d)), full((1, d))],
        out_specs=[rows(0), rows(0)],
        out_shape=[jax.ShapeDtypeStruct((t, d), F32), jax.ShapeDtypeStruct((t, d), F32)],
        scratch_shapes=[pltpu.VMEM((tm, d), BF16)],
        compiler_params=_params("parallel"),
        name="merge",
    )(o_f, o_b, z2d, z2d, z2d, o_att, x2d, mod3, mod3, mod3, w_rec, w_att, w_out,
      norm_g.reshape(1, HGRN_DK), ln_g.reshape(1, d), ln_b.reshape(1, d))


def _top16(s, n):
    iota = lax.broadcasted_iota(jnp.int32, s.shape, 0).astype(F32)
    vals, idxs = [], []
    for _ in range(PEER_TOPK):
        m = jnp.max(s, axis=0, keepdims=True)
        idx = jnp.min(jnp.where(s == m, iota, float(n)), axis=0, keepdims=True)
        vals.append(m)
        idxs.append(idx)
        s = jnp.where(iota == idx, NEG_INF, s)
    return vals, idxs


def _candidate_positions():
    rows = []
    for a in range(4):
        for b in range(16 if a == 0 else 8):
            rows.append((a, b, (a + 1) * (b + 1) <= PEER_TOPK))
    for b, n_a in ((0, 16), (1, 8), (2, 8)):
        for a in range(n_a):
            rows.append((a, b, a >= 4 and (a + 1) * (b + 1) <= PEER_TOPK))
    return np.array([[a * PEER_TOPK + b if ok else -1.0] for a, b, ok in rows], np.float32)


def _peer_topk_kernel(h2_ref, wq_ref, keys_ref, pos_ref, ida_ref, idb_ref, gate_ref):
    qf = _dot(h2_ref[...].astype(BF16), wq_ref[...])
    k = PEER_TOPK
    tb = qf.shape[0]
    posb = jnp.broadcast_to(pos_ref[...], (pos_ref.shape[0], tb))
    valid = posb >= 0.0
    for h in range(PEER_HEADS):
        tops = []
        for p in range(2):
            seg = (h * 2 + p) * PEER_HALF
            st = _dot_nt(keys_ref[h, p], qf[:, seg:seg + PEER_HALF].astype(BF16))
            tops.append(_top16(st, N_KEYS))
        (v0, i0), (v1, i1) = tops
        v0c = jnp.concatenate(v0, axis=0)
        v1c = jnp.concatenate(v1, axis=0)
        cand = jnp.concatenate([v0[0] + v1c, v0[1] + v1c[:8], v0[2] + v1c[:8], v0[3] + v1c[:8],
                                v0c + v1[0], v0c[:8] + v1[1], v0c[:8] + v1[2]], axis=0)
        cand = jnp.where(valid, cand, NEG_INF)
        best, poss = [], []
        for _ in range(k):
            m = jnp.max(cand, axis=0, keepdims=True)
            pos = jnp.min(jnp.where(cand == m, posb, float(k * k)), axis=0, keepdims=True)
            best.append(m)
            poss.append(pos)
            cand = jnp.where(posb == pos, NEG_INF, cand)
        best = jnp.concatenate(best, axis=0)
        posc = jnp.concatenate(poss, axis=0)
        a_rank = jnp.floor(posc * (1.0 / k))
        b_rank = posc - k * a_rank
        ea = jnp.zeros_like(posc)
        eb = jnp.zeros_like(posc)
        for r in range(k):
            ea = jnp.where(a_rank == float(r), i0[r], ea)
            eb = jnp.where(b_rank == float(r), i1[r], eb)
        e = jnp.exp(best - best[0:1, :])
        gate_ref[h * k:(h + 1) * k, :] = e / jnp.sum(e, axis=0, keepdims=True)
        ida_ref[h * k:(h + 1) * k, :] = ea
        idb_ref[h * k:(h + 1) * k, :] = eb


def _peer_topk(h2, wq_bf, keys_bf, tb):
    t, d = h2.shape
    nq = wq_bf.shape[1]
    rows = PEER_HEADS * PEER_TOPK
    out = jax.ShapeDtypeStruct((rows, t), F32)
    ospec = pl.BlockSpec((rows, tb), lambda i: (0, i))
    pos = _candidate_positions()
    return pl.pallas_call(
        _peer_topk_kernel,
        grid=(t // tb,),
        in_specs=[
            pl.BlockSpec((tb, d), lambda i: (i, 0)),
            pl.BlockSpec((d, nq), lambda i: (0, 0)),
            pl.BlockSpec((PEER_HEADS, 2, N_KEYS, PEER_HALF), lambda i: (0, 0, 0, 0)),
            pl.BlockSpec(pos.shape, lambda i: (0, 0)),
        ],
        out_specs=[ospec, ospec, ospec],
        out_shape=[out, out, out],
        compiler_params=_params("parallel"),
        name="peer_topk",
    )(h2, wq_bf, keys_bf, jnp.asarray(pos))


RSQRT2 = 1.0 / math.sqrt(2.0)


def _peer_mix_kernel(h2_ref, ida_ref, idb_ref, gate_ref, ut_ref, v_ref, x1_ref, g2_ref, lng_ref, lnb_ref,
                     x2_ref, w_scr, p_scr, acc_scr, h2_scr, *, tb, te, pitch):
    j = pl.program_id(1)
    rows_per_tile = te // N_KEYS

    @pl.when(j == 0)
    def _():
        acc_scr[...] = jnp.zeros_like(acc_scr)
        h2_scr[...] = h2_ref[...].astype(BF16)
        iota = lax.broadcasted_iota(jnp.int32, (N_KEYS, PEER_HEADS * PEER_TOPK), 0).astype(F32)

        def tok(t, carry):
            a_row = ida_ref[pl.ds(t, 1), :]
            b_row = idb_ref[pl.ds(t, 1), :]
            g_row = gate_ref[pl.ds(t, 1), :] * RSQRT2
            at = jnp.where(iota == a_row, 1.0, 0.0).astype(BF16)
            gbt = jnp.where(iota == b_row, g_row, 0.0).astype(BF16)
            w_scr[pl.ds(t, N_KEYS, stride=pitch), :] = _dot_nt(at, gbt)
            return carry

        lax.fori_loop(0, tb, tok, 0, unroll=32)

    a = _dot(h2_scr[...], ut_ref[...])
    for r in range(rows_per_tile):
        w = w_scr[pl.ds(pl.multiple_of((j * rows_per_tile + r) * pitch, 8), tb), :]
        cs = slice(r * N_KEYS, (r + 1) * N_KEYS)
        ar = a[:, cs]
        p_scr[:, cs] = (w * (ar * (1.0 + lax.erf(ar)))).astype(BF16)
    acc_scr[...] += _dot(p_scr[...], v_ref[...])

    @pl.when(j == pl.num_programs(1) - 1)
    def _():
        x2_ref[...] = _layer_norm(DEEPNORM_ALPHA * x1_ref[...] + g2_ref[0] * acc_scr[...],
                                  lng_ref[...], lnb_ref[...])


def _peer_mix(h2, ida, idb, gate, ut_bf, v_bf, x1, mod_l, ln_g, ln_b, rows_of_tile, tb):
    t, d = x1.shape
    n_e = v_bf.shape[0]
    te = 2048
    hk = PEER_HEADS * PEER_TOPK
    pitch = tb + 8
    mod3 = mod_l.reshape(8, 1, 6 * d)
    return pl.pallas_call(
        functools.partial(_peer_mix_kernel, tb=tb, te=te, pitch=pitch),
        grid=(t // tb, n_e // te),
        in_specs=[
            pl.BlockSpec((tb, d), lambda i, j: (i, 0)),
            pl.BlockSpec((tb, hk), lambda i, j: (i, 0)),
            pl.BlockSpec((tb, hk), lambda i, j: (i, 0)),
            pl.BlockSpec((tb, hk), lambda i, j: (i, 0)),
            pl.BlockSpec((d, te), lambda i, j: (0, j)),
            pl.BlockSpec((te, d), lambda i, j: (j, 0)),
            pl.BlockSpec((tb, d), lambda i, j: (i, 0)),
            pl.BlockSpec((1, 1, d), lambda i, j: (rows_of_tile(i), 0, 5)),
            pl.BlockSpec((1, d), lambda i, j: (0, 0)),
            pl.BlockSpec((1, d), lambda i, j: (0, 0)),
        ],
        out_specs=pl.BlockSpec((tb, d), lambda i, j: (i, 0)),
        out_shape=jax.ShapeDtypeStruct((t, d), F32),
        scratch_shapes=[pltpu.VMEM((pitch * N_KEYS, N_KEYS), F32), pltpu.VMEM((tb, te), BF16),
                        pltpu.VMEM((tb, d), F32), pltpu.VMEM((tb, d), BF16)],
        compiler_params=_params("parallel", "arbitrary"),
        name="peer_mix",
    )(h2, ida, idb, gate, ut_bf, v_bf, x1, mod3, ln_g.reshape(1, d), ln_b.reshape(1, d))


def _rope_tables(length):
    rows = length // GRID_W
    row = jnp.broadcast_to(jnp.arange(rows, dtype=F32)[:, None], (rows, GRID_W)).reshape(length)
    col = jnp.broadcast_to(jnp.arange(GRID_W, dtype=F32)[None, :], (rows, GRID_W)).reshape(length)
    n_freq = HEAD_DIM // 4
    inv = jnp.power(ROPE_THETA, -jnp.arange(n_freq, dtype=F32) / n_freq)
    ang = jnp.concatenate([row[:, None] * inv, col[:, None] * inv], axis=-1)
    cos, sin = jnp.cos(ang), jnp.sin(ang)
    return jnp.concatenate([cos, cos], axis=-1), jnp.concatenate([-sin, sin], axis=-1)


def kernel(x_prompt, x_sample, cache_attn_k, cache_attn_v, state_hgrn, c, c_ctx, w_mod, b_mod, w_in,
           hgrn_lb_logits, hgrn_norm_g, q_norm_g, k_norm_g, w_branch_rec, w_branch_att, w_out,
           ln1_g, ln1_b, ln2_g, ln2_b, peer_w_query, peer_sub_keys, peer_u, peer_v):
    nb, seq, d = x_prompt.shape
    db, dseq, _ = x_sample.shape
    depth = w_in.shape[0]
    past = cache_attn_k.shape[2]
    t_ctx, t_lat = nb * seq, db * dseq
    t_all = t_ctx + t_lat
    assert d == D_MODEL and 1 + db <= 8

    tm = _tile(math.gcd(t_ctx, dseq), 1024)
    tm_merge = _tile(math.gcd(t_ctx, dseq), 256)
    tb_peer = _tile(math.gcd(t_ctx, dseq), 256)

    def rows_of_tile_fn(tile):
        n_ctx = t_ctx // tile
        per = dseq // tile
        return lambda i: jnp.where(i < n_ctx, 0, 1 + (i - n_ctx) // per)

    cond8 = jnp.concatenate([c_ctx[None, :], c, jnp.zeros((8 - 1 - db, d), F32)], axis=0)
    mod = _modulation(cond8, w_mod, b_mod)

    lb = jnp.cumsum(jax.nn.softmax(hgrn_lb_logits.astype(F32), axis=1), axis=1)
    lb = lb - lb[:, :1]

    segs = jnp.split(w_in, np.cumsum([HGRN_WIDTH] * 5 + [ATTN_WIDTH, KV_WIDTH, KV_WIDTH, D_MODEL]).tolist(), axis=-1)
    hq_w, hff_w, hfb_w, hi_w, hg_w, aq_w, ak_w, av_w, ga_w, gb_w = segs
    w_in_bf = jnp.concatenate([hq_w, hff_w, hfb_w, hi_w, hg_w, aq_w, ga_w, gb_w, ak_w, av_w], axis=-1).astype(BF16)
    w_rec_bf, w_att_bf, w_out_bf = (w.astype(BF16) for w in (w_branch_rec, w_branch_att, w_out))
    wq_bf = peer_w_query.astype(BF16)
    keys_bf = peer_sub_keys.astype(BF16)
    ut_bf = (jnp.swapaxes(peer_u, 1, 2) * RSQRT2).astype(BF16)
    v_bf = peer_v.astype(BF16)

    cos_t, sin_t = _rope_tables(dseq)
    ones_t, zeros_t = jnp.ones((seq, HEAD_DIM), F32), jnp.zeros((seq, HEAD_DIM), F32)
    cache_k_bf = jnp.transpose(cache_attn_k, (0, 1, 3, 2, 4)).astype(BF16)
    cache_v_t = jnp.transpose(cache_attn_v, (0, 1, 3, 2, 4))
    ones_pad = jnp.zeros(cache_v_t.shape, F32).at[..., 0].set(1.0)
    cache_v_aug = jnp.concatenate([cache_v_t, ones_pad], axis=-1).astype(BF16)

    seq_rows = [(i * seq, seq) for i in range(nb)] + [(t_ctx + i * dseq, dseq) for i in range(db)]
    zero_state = jnp.zeros((nb, 2, HGRN_HEADS, HGRN_DK, HGRN_DK), F32)

    x2d = jnp.concatenate([x_prompt.reshape(t_ctx, d), x_sample.reshape(t_lat, d)], axis=0)
    new_k, new_v, new_s = [], [], []
    for l in range(depth):
        z2d = _in_proj(x2d, mod[l], w_in_bf[l], rows_of_tile_fn(tm), tm)

        s0_all = jnp.concatenate([zero_state, state_hgrn[:, l]], axis=0)
        o_f, o_b, s_fin = _hgrn_scan(z2d, lb[:, l], s0_all, seq_rows, HGRN_CHUNK)
        new_s.append(s_fin[:nb])

        q_c, k_c, v_c, k_norm = _qkv_prep(z2d, 0, nb, seq, q_norm_g[l], k_norm_g[l], ones_t, zeros_t, False, True)
        o_att = _attention(q_c, k_c, v_c, None, 0, t_all)
        new_k.append(k_norm.reshape(nb, seq, KV_HEADS, HEAD_DIM))
        new_v.append(z2d[:t_ctx, COL_AV:].reshape(nb, seq, KV_HEADS, HEAD_DIM))

        q_s, k_s, v_s = _qkv_prep(z2d, t_ctx, db, dseq, q_norm_g[l], k_norm_g[l], cos_t, sin_t, True, False)
        k_s = jnp.concatenate([k_s, cache_k_bf[:, l]], axis=2)
        v_s = jnp.concatenate([v_s, cache_v_aug[:, l]], axis=2)
        o_att = _attention(q_s, k_s, v_s, o_att, t_ctx, t_all)

        x1, h2 = _merge(o_f, o_b, z2d, o_att, x2d, mod[l], w_rec_bf[l], w_att_bf[l], w_out_bf[l],
                        hgrn_norm_g[l], ln1_g[l], ln1_b[l], rows_of_tile_fn(tm_merge), tm_merge)

        ida, idb, gate = _peer_topk(h2, wq_bf[l], keys_bf[l], tb_peer)
        x2d = _peer_mix(h2, ida.T, idb.T, gate.T, ut_bf[l], v_bf[l], x1, mod[l], ln2_g[l], ln2_b[l],
                        rows_of_tile_fn(tb_peer), tb_peer)

    y_prompt = x2d[:t_ctx].reshape(nb, seq, d)
    y_sample = x2d[t_ctx:].reshape(db, dseq, d)
    return (y_prompt, y_sample, jnp.stack(new_k, axis=1), jnp.stack(new_v, axis=1), jnp.stack(new_s, axis=1))
```

```python
import functools
import math

import numpy as np
import jax
import jax.numpy as jnp
from jax import lax
from jax.experimental import pallas as pl
from jax.experimental.pallas import tpu as pltpu

F32 = jnp.float32
BF16 = jnp.bfloat16

D_MODEL = 1024
GRID_W = 64
HGRN_HEADS = 8
HGRN_DK = 128
HGRN_WIDTH = HGRN_HEADS * HGRN_DK
HGRN_CHUNK = 128
ATTN_HEADS = 16
KV_HEADS = 4
HEAD_DIM = 64
GROUP = ATTN_HEADS // KV_HEADS
ATTN_WIDTH = ATTN_HEADS * HEAD_DIM
KV_WIDTH = KV_HEADS * HEAD_DIM
ATTN_SCALE = HEAD_DIM ** -0.5
ROPE_THETA = 10000.0
PEER_HEADS = 8
PEER_HALF = 128
N_KEYS = 128
N_EXPERTS = N_KEYS * N_KEYS
PEER_TOPK = 16
MODEL_DEPTH = 4
DEEPNORM_ALPHA = (2 * MODEL_DEPTH) ** 0.25
EPS = 1e-6
NEG_INF = float("-inf")

IN_COLS = 5 * HGRN_WIDTH + ATTN_WIDTH + 2 * D_MODEL + 2 * KV_WIDTH
COL_AQ = 5 * HGRN_WIDTH
COL_GA = COL_AQ + ATTN_WIDTH
COL_GB = COL_GA + D_MODEL
COL_AK = COL_GB + D_MODEL
COL_AV = COL_AK + KV_WIDTH

VMEM_LIMIT = 56 * 1024 * 1024
LOG2_E = math.log2(math.e)
ATTN_TQ = 256
ATTN_TS = 512
PEER_TB_MIX = 512
PEER_TE = 2048


def _params(*sem):
    return pltpu.CompilerParams(dimension_semantics=sem, vmem_limit_bytes=VMEM_LIMIT)


def _tile(n, pref):
    t = min(n, pref)
    while n % t or t % 8:
        t -= 1
    return t


def _sigmoid(x):
    return 1.0 / (1.0 + jnp.exp(-x))


def _sigmoid_t(x):
    return 0.5 + 0.5 * jnp.tanh(0.5 * x)


def _silu(x):
    h = 0.5 * x
    return h + h * jnp.tanh(h)


def _bf16_floor(x):
    bits = pltpu.bitcast(x, jnp.uint32) & jnp.uint32(0xFFFF0000)
    return pltpu.bitcast(bits, F32)


def _dot(a, b):
    return jnp.dot(a, b, preferred_element_type=F32)


def _dot_nt(a, b):
    return lax.dot_general(a, b, (((1,), (1,)), ((), ())), preferred_element_type=F32)


def _mod_kernel(cond_ref, w_ref, b_ref, o_ref):
    c = cond_ref[...]
    s = c * _sigmoid(c)
    o_ref[0] = jnp.dot(s, w_ref[0], precision=lax.Precision.HIGHEST, preferred_element_type=F32) + b_ref[0]


def _modulation(cond8, w_mod, b_mod):
    depth, d, n = w_mod.shape
    tn = _tile(n, 1536)
    return pl.pallas_call(
        _mod_kernel,
        grid=(depth, n // tn),
        in_specs=[
            pl.BlockSpec((8, d), lambda l, j: (0, 0)),
            pl.BlockSpec((1, d, tn), lambda l, j: (l, 0, j)),
            pl.BlockSpec((1, 1, tn), lambda l, j: (l, 0, j)),
        ],
        out_specs=pl.BlockSpec((1, 8, tn), lambda l, j: (l, 0, j)),
        out_shape=jax.ShapeDtypeStruct((depth, 8, n), F32),
        compiler_params=_params("parallel", "parallel"),
        name="modulation",
    )(cond8, w_mod, b_mod.reshape(depth, 1, n))


def _inproj_kernel(x_ref, sh_ref, sc_ref, w_ref, z_ref, h_scr):
    @pl.when(pl.program_id(1) == 0)
    def _():
        h_scr[...] = (x_ref[...] * (1.0 + sc_ref[0]) + sh_ref[0]).astype(BF16)

    z_ref[...] = _dot(h_scr[...], w_ref[...])


def _in_proj(x2d, mod_l, w_in_bf, rows_of_tile, tm):
    t, d = x2d.shape
    n = w_in_bf.shape[1]
    tn = 512
    mod3 = mod_l.reshape(8, 1, 6 * d)
    return pl.pallas_call(
        _inproj_kernel,
        grid=(t // tm, n // tn),
        in_specs=[
            pl.BlockSpec((tm, d), lambda i, j: (i, 0)),
            pl.BlockSpec((1, 1, d), lambda i, j: (rows_of_tile(i), 0, 0)),
            pl.BlockSpec((1, 1, d), lambda i, j: (rows_of_tile(i), 0, 1)),
            pl.BlockSpec((d, tn), lambda i, j: (0, j)),
        ],
        out_specs=pl.BlockSpec((tm, tn), lambda i, j: (i, j)),
        out_shape=jax.ShapeDtypeStruct((t, n), F32),
        scratch_shapes=[pltpu.VMEM((tm, d), BF16)],
        compiler_params=_params("parallel", "arbitrary"),
        name="in_proj",
    )(x2d, mod3, mod3, w_in_bf)


def _level_reference_rows(beta, c):
    width = beta.shape[1]
    sub = lax.broadcasted_iota(jnp.int32, (8, width), 0)

    def bc(i, n=8):
        return jnp.broadcast_to(beta[i:i + 1, :], (n, width))

    out = []
    m = 1
    while m < c:
        blocks = []
        if m >= 8:
            for blk in range(c // (2 * m)):
                blocks.append(bc(blk * 2 * m + m, 2 * m))
        elif m == 4:
            for v in range(c // 8):
                blocks.append(bc(8 * v + 4))
        elif m == 2:
            for v in range(c // 8):
                blocks.append(jnp.where(sub < 4, bc(8 * v + 2), bc(8 * v + 6)))
        else:
            for v in range(c // 8):
                lo = jnp.where(sub < 2, bc(8 * v + 1), bc(8 * v + 3))
                hi = jnp.where(sub < 6, bc(8 * v + 5), bc(8 * v + 7))
                blocks.append(jnp.where(sub < 4, lo, hi))
        out.append(blocks[0] if len(blocks) == 1 else jnp.concatenate(blocks, axis=0))
        m *= 2
    return out


def _hgrn_direction(zq, zf, v, lb, st_ref, o_ref, tri, eye, q_half, pair_masks, reverse, c):
    heads = [slice(h * HGRN_DK, (h + 1) * HGRN_DK) for h in range(HGRN_HEADS)]
    q = _silu(zq)
    e = jnp.exp(-zf)
    log1pe = jnp.log(1.0 + e)
    g = jnp.log(1.0 + lb * e) - log1pe
    k = (1.0 - lb) * jnp.exp(-zf - log1pe)
    g1 = _bf16_floor(g)
    r1 = g - g1
    g2 = _bf16_floor(r1)
    g3 = r1 - g2
    beta = (_dot(tri, g1.astype(BF16)) + _dot(tri, g2.astype(BF16))) + _dot(tri, g3.astype(BF16))
    last = 0 if reverse else c - 1
    beta_tot = beta[last:last + 1, :]

    q_bf, k_bf, v_bf = q.astype(BF16), k.astype(BF16), v.astype(BF16)
    a = [jnp.where(eye, _dot_nt(q_bf[:, hs], k_bf[:, hs]), 0.0) for hs in heads]
    for r, qh, pm in zip(_level_reference_rows(beta, c), q_half, pair_masks):
        u = (jnp.where(qh, q, k) * jnp.exp(-jnp.abs(beta - r))).astype(BF16)
        a = [ah + jnp.where(pm, _dot_nt(u[:, hs], u[:, hs]), 0.0) for ah, hs in zip(a, heads)]

    qb = (q * jnp.exp(beta)).astype(BF16)
    kb = (k * jnp.exp(beta_tot - beta)).astype(BF16)
    dec = jnp.exp(beta_tot)
    for h, hs in enumerate(heads):
        st = st_ref[h]
        o_ref[:, hs] = _dot(a[h].astype(BF16), v_bf[:, hs]) + _dot_nt(qb[:, hs], st.astype(BF16))
        st_ref[h] = st * dec[:, hs] + lax.dot_general(
            v_bf[:, hs], kb[:, hs], (((0,), (0,)), ((), ())), preferred_element_type=F32)


def _hgrn_kernel(rbf_ref, rbb_ref, first_ref, seq_ref,
                 qf_ref, ff_ref, vf_ref, qb_ref, fb_ref, vb_ref, lb_ref, s0_ref,
                 of_ref, ob_ref, sout_ref, st_scr, *, c, n_last):
    s = pl.program_id(0)

    @pl.when(first_ref[s] == 1)
    def _():
        for d in range(2):
            for h in range(HGRN_HEADS):
                st_scr[d, h] = s0_ref[0, d, h].T

    row = lax.broadcasted_iota(jnp.int32, (c, c), 0)
    col = lax.broadcasted_iota(jnp.int32, (c, c), 1)
    rowd = lax.broadcasted_iota(jnp.int32, (c, HGRN_WIDTH), 0)
    eye = row == col

    for d, (q_ref, f_ref, v_ref, o_ref) in enumerate(
            ((qf_ref, ff_ref, vf_ref, of_ref), (qb_ref, fb_ref, vb_ref, ob_ref))):
        reverse = d == 1
        tri = (col >= row if reverse else col <= row).astype(BF16)
        q_par = 0 if reverse else 1
        q_half, pair_masks = [], []
        m = 1
        while m < c:
            q_half.append(((rowd // m) % 2) == q_par)
            pair_masks.append(((row // (2 * m)) == (col // (2 * m)))
                              & (((row // m) % 2) == q_par) & (((col // m) % 2) == 1 - q_par))
            m *= 2
        _hgrn_direction(q_ref[...], f_ref[...], v_ref[...], lb_ref[pl.ds(d, 1), :], st_scr.at[d], o_ref,
                        tri, eye, q_half, pair_masks, reverse, c)

    @pl.when(first_ref[jnp.minimum(s + 1, n_last)] == 1)
    def _():
        for d in range(2):
            for h in range(HGRN_HEADS):
                sout_ref[0, d, h] = st_scr[d, h].T


def _hgrn_scan(z2d, lb2, s0_all, seq_rows, c):
    t = z2d.shape[0]
    rbf, rbb, first, seq = [], [], [], []
    for si, (r0, length) in enumerate(seq_rows):
        n = length // c
        for ci in range(n):
            rbf.append(r0 // c + ci)
            rbb.append(r0 // c + n - 1 - ci)
            first.append(1 if ci == 0 else 0)
            seq.append(si)
    n_steps = len(rbf)
    first.append(1)
    w = HGRN_WIDTH

    def zspec(which, colblk):
        if which == 0:
            return pl.BlockSpec((c, w), lambda s, rbf, rbb, fi, sq: (rbf[s], colblk))
        return pl.BlockSpec((c, w), lambda s, rbf, rbb, fi, sq: (rbb[s], colblk))

    n_seq = len(seq_rows)
    state_spec = pl.BlockSpec((1, 2, HGRN_HEADS, HGRN_DK, HGRN_DK),
                              lambda s, rbf, rbb, fi, sq: (sq[s], 0, 0, 0, 0))
    grid_spec = pltpu.PrefetchScalarGridSpec(
        num_scalar_prefetch=4,
        grid=(n_steps,),
        in_specs=[zspec(0, 0), zspec(0, 1), zspec(0, 3), zspec(1, 0), zspec(1, 2), zspec(1, 3),
                  pl.BlockSpec((2, w), lambda s, *_: (0, 0)), state_spec],
        out_specs=[zspec(0, 0), zspec(1, 0), state_spec],
        scratch_shapes=[pltpu.VMEM((2, HGRN_HEADS, HGRN_DK, HGRN_DK), F32)],
    )
    return pl.pallas_call(
        functools.partial(_hgrn_kernel, c=c, n_last=n_steps),
        grid_spec=grid_spec,
        out_shape=[jax.ShapeDtypeStruct((t, w), F32), jax.ShapeDtypeStruct((t, w), F32),
                   jax.ShapeDtypeStruct((n_seq, 2, HGRN_HEADS, HGRN_DK, HGRN_DK), F32)],
        compiler_params=_params("arbitrary"),
        name="hgrn_scan",
    )(jnp.asarray(rbf, jnp.int32), jnp.asarray(rbb, jnp.int32), jnp.asarray(first, jnp.int32),
      jnp.asarray(seq, jnp.int32), z2d, z2d, z2d, z2d, z2d, z2d, lb2, s0_all)


def _qkv_prep_kernel(aq_ref, ak_ref, av_ref, qg_ref, kg_ref, cos_ref, sin_ref, swap_ref, *out_refs,
                     rope, q_scale, keep_k):
    q_out, k_out, v_out = out_refs[:3]
    hd = HEAD_DIM

    def norm_rope(x, g):
        ms = jnp.mean(x * x, axis=-1, keepdims=True)
        y = x * lax.rsqrt(ms + EPS) * g
        if rope:
            y = y * cos_ref[...] + _dot(y.astype(BF16), swap_ref[...]) * sin_ref[...]
        return y

    for j in range(ATTN_WIDTH // 128):
        pair = aq_ref[:, j * 128:(j + 1) * 128]
        for i in range(2):
            y = norm_rope(pair[:, i * hd:(i + 1) * hd], qg_ref[...])
            q_out[0, 2 * j + i] = (y * q_scale).astype(BF16)
    tl = ak_ref.shape[0]
    ones_col = jnp.where(lax.broadcasted_iota(jnp.int32, (tl, hd), 1) == 0, 1.0, 0.0).astype(BF16)
    for j in range(KV_WIDTH // 128):
        kpair = ak_ref[:, j * 128:(j + 1) * 128]
        vpair = av_ref[:, j * 128:(j + 1) * 128]
        for i in range(2):
            h = 2 * j + i
            y = norm_rope(kpair[:, i * hd:(i + 1) * hd], kg_ref[...])
            k_out[0, h] = y.astype(BF16)
            if keep_k:
                out_refs[3][:, h * hd:(h + 1) * hd] = y
            v_out[0, h, :, :hd] = vpair[:, i * hd:(i + 1) * hd].astype(BF16)
            v_out[0, h, :, hd:] = ones_col


def _qkv_prep(z2d, row0, n_seq, length, q_gain, k_gain, cos, sin, rope, keep_k):
    hd = HEAD_DIM
    tl = _tile(length, 512)
    nt = length // tl
    blk0 = row0 // tl
    swap = np.zeros((hd, hd), np.float32)
    for i in range(hd):
        swap[(i + hd // 2) % hd, i] = 1.0

    def zspec(width, col):
        return pl.BlockSpec((tl, width), lambda b, j: (blk0 + b * nt + j, col // width))

    def full(shape):
        return pl.BlockSpec(shape, lambda b, j: (0,) * len(shape))

    out_specs = [pl.BlockSpec((1, ATTN_HEADS, tl, hd), lambda b, j: (b, 0, j, 0)),
                 pl.BlockSpec((1, KV_HEADS, tl, hd), lambda b, j: (b, 0, j, 0)),
                 pl.BlockSpec((1, KV_HEADS, tl, 2 * hd), lambda b, j: (b, 0, j, 0))]
    out_shape = [jax.ShapeDtypeStruct((n_seq, ATTN_HEADS, length, hd), BF16),
                 jax.ShapeDtypeStruct((n_seq, KV_HEADS, length, hd), BF16),
                 jax.ShapeDtypeStruct((n_seq, KV_HEADS, length, 2 * hd), BF16)]
    if keep_k:
        out_specs.append(pl.BlockSpec((tl, KV_WIDTH), lambda b, j: (b * nt + j, 0)))
        out_shape.append(jax.ShapeDtypeStruct((n_seq * length, KV_WIDTH), F32))
    return pl.pallas_call(
        functools.partial(_qkv_prep_kernel, rope=rope, q_scale=ATTN_SCALE * LOG2_E, keep_k=keep_k),
        grid=(n_seq, nt),
        in_specs=[zspec(ATTN_WIDTH, COL_AQ), zspec(KV_WIDTH, COL_AK), zspec(KV_WIDTH, COL_AV),
                  full((1, hd)), full((1, hd)),
                  pl.BlockSpec((tl, hd), lambda b, j: (j, 0)), pl.BlockSpec((tl, hd), lambda b, j: (j, 0)),
                  full((hd, hd))],
        out_specs=out_specs,
        out_shape=out_shape,
        compiler_params=_params("parallel", "parallel"),
        name="qkv_prep",
    )(z2d, z2d, z2d, q_gain.reshape(1, hd), k_gain.reshape(1, hd), cos, sin, jnp.asarray(swap, BF16))


def _attn_kernel(q_ref, k_ref, v_ref, *rest, ts, n_s, tq):
    o_ref = rest[-1]
    rows = GROUP * tq
    q = q_ref[0].reshape(rows, HEAD_DIM)

    def body(i, carry):
        m, acc = carry
        sl = pl.ds(pl.multiple_of(i * ts, ts), ts)
        s = _dot_nt(q, k_ref[0, 0, sl, :])
        m_new = jnp.maximum(m, jnp.max(s, axis=-1, keepdims=True))
        p = jnp.exp2(s - m_new)
        acc = jnp.exp2(m - m_new) * acc + _dot(p.astype(BF16), v_ref[0, 0, sl, :])
        return m_new, acc

    m0 = jnp.full((rows, 1), NEG_INF, F32)
    acc0 = jnp.zeros((rows, 2 * HEAD_DIM), F32)
    m, acc = lax.fori_loop(0, n_s, body, (m0, acc0), unroll=True)
    o = acc[:, :HEAD_DIM] / acc[:, HEAD_DIM:HEAD_DIM + 1]
    for g in range(GROUP):
        o_ref[:, g * HEAD_DIM:(g + 1) * HEAD_DIM] = o[g * tq:(g + 1) * tq]


def _attention(q, k, v_aug, o_prev, row0, t_all):
    b, _, length, hd = q.shape
    s_len = k.shape[2]
    tq = _tile(length, ATTN_TQ)
    ts = _tile(s_len, ATTN_TS)
    nq = length // tq
    blk0 = row0 // tq
    in_specs = [
        pl.BlockSpec((1, GROUP, tq, hd), lambda bi, hi, qi: (bi, hi, qi, 0)),
        pl.BlockSpec((1, 1, s_len, hd), lambda bi, hi, qi: (bi, hi, 0, 0)),
        pl.BlockSpec((1, 1, s_len, 2 * hd), lambda bi, hi, qi: (bi, hi, 0, 0)),
    ]
    args = [q, k, v_aug]
    aliases = {}
    if o_prev is not None:
        in_specs.append(pl.BlockSpec(memory_space=pl.ANY))
        args.append(o_prev)
        aliases = {3: 0}
    return pl.pallas_call(
        functools.partial(_attn_kernel, ts=ts, n_s=s_len // ts, tq=tq),
        grid=(b, KV_HEADS, nq),
        in_specs=in_specs,
        out_specs=pl.BlockSpec((tq, GROUP * hd), lambda bi, hi, qi: (blk0 + bi * nq + qi, hi)),
        out_shape=jax.ShapeDtypeStruct((t_all, ATTN_WIDTH), F32),
        input_output_aliases=aliases,
        compiler_params=_params("parallel", "parallel", "parallel"),
        name="attention",
    )(*args)


def _layer_norm(t, g, b):
    mu = jnp.mean(t, axis=-1, keepdims=True)
    tc = t - mu
    var = jnp.mean(tc * tc, axis=-1, keepdims=True)
    return tc * lax.rsqrt(var + EPS) * g + b


def _merge_kernel(of_ref, ob_ref, hg_ref, ga_ref, gb_ref, oatt_ref, x_ref, g1_ref, sh2_ref, sc2_ref,
                  wrec_ref, watt_ref, wout_ref, ng_ref, lng_ref, lnb_ref, x1_ref, h2_ref, orec_scr):
    for h in range(HGRN_HEADS):
        cs = slice(h * HGRN_DK, (h + 1) * HGRN_DK)
        o = of_ref[:, cs] + ob_ref[:, cs]
        ms = jnp.mean(o * o, axis=-1, keepdims=True)
        zg = hg_ref[:, cs]
        orec_scr[:, cs] = (o * lax.rsqrt(ms + EPS) * ng_ref[...] * _silu(zg)).astype(BF16)
    y = (_sigmoid_t(ga_ref[...]) * _dot(orec_scr[...], wrec_ref[...])
         + _sigmoid_t(gb_ref[...]) * _dot(oatt_ref[...].astype(BF16), watt_ref[...]))
    u = _dot(y.astype(BF16), wout_ref[...])
    x1 = _layer_norm(DEEPNORM_ALPHA * x_ref[...] + g1_ref[0] * u, lng_ref[...], lnb_ref[...])
    x1_ref[...] = x1
    h2_ref[...] = x1 * (1.0 + sc2_ref[0]) + sh2_ref[0]


def _merge(o_f, o_b, z2d, o_att, x2d, mod_l, w_rec, w_att, w_out, norm_g, ln_g, ln_b, rows_of_tile, tm):
    t, d = x2d.shape
    mod3 = mod_l.reshape(8, 1, 6 * d)

    def rows(colblk):
        return pl.BlockSpec((tm, d), lambda i: (i, colblk))

    def modspec(chunk):
        return pl.BlockSpec((1, 1, d), lambda i: (rows_of_tile(i), 0, chunk))

    def full(shape):
        return pl.BlockSpec(shape, lambda i: (0,) * len(shape))

    return pl.pallas_call(
        _merge_kernel,
        grid=(t // tm,),
        in_specs=[rows(0), rows(0), rows(4), rows(COL_GA // d), rows(COL_GB // d), rows(0), rows(0),
                  modspec(2), modspec(3), modspec(4),
                  full((d, d)), full((d, d)), full((d, d)), full((1, HGRN_DK)), full((1, d)), full((1, d))],
        out_specs=[rows(0), rows(0)],
        out_shape=[jax.ShapeDtypeStruct((t, d), F32), jax.ShapeDtypeStruct((t, d), F32)],
        scratch_shapes=[pltpu.VMEM((tm, d), BF16)],
        compiler_params=_params("parallel"),
        name="merge",
    )(o_f, o_b, z2d, z2d, z2d, o_att, x2d, mod3, mod3, mod3, w_rec, w_att, w_out,
      norm_g.reshape(1, HGRN_DK), ln_g.reshape(1, d), ln_b.reshape(1, d))


def _top16(s, n):
    iota = lax.broadcasted_iota(jnp.int32, s.shape, 0).astype(F32)
    vals, idxs = [], []
    for _ in range(PEER_TOPK):
        m = jnp.max(s, axis=0, keepdims=True)
        idx = jnp.min(jnp.where(s == m, iota, float(n)), axis=0, keepdims=True)
        vals.append(m)
        idxs.append(idx)
        s = jnp.where(iota == idx, NEG_INF, s)
    return vals, idxs


def _candidate_positions():
    rows = []
    for a in range(4):
        for b in range(16 if a == 0 else 8):
            rows.append((a, b, (a + 1) * (b + 1) <= PEER_TOPK))
    for b, n_a in ((0, 16), (1, 8), (2, 8)):
        for a in range(n_a):
            rows.append((a, b, a >= 4 and (a + 1) * (b + 1) <= PEER_TOPK))
    return np.array([[a * PEER_TOPK + b if ok else -1.0] for a, b, ok in rows], np.float32)


def _peer_topk_kernel(h2_ref, wq_ref, keys_ref, pos_ref, ida_ref, idb_ref, gate_ref):
    qf = _dot(h2_ref[...].astype(BF16), wq_ref[...])
    k = PEER_TOPK
    tb = qf.shape[0]
    posb = jnp.broadcast_to(pos_ref[...], (pos_ref.shape[0], tb))
    valid = posb >= 0.0
    for h in range(PEER_HEADS):
        tops = []
        for p in range(2):
            seg = (h * 2 + p) * PEER_HALF
            st = _dot_nt(keys_ref[h, p], qf[:, seg:seg + PEER_HALF].astype(BF16))
            tops.append(_top16(st, N_KEYS))
        (v0, i0), (v1, i1) = tops
        v0c = jnp.concatenate(v0, axis=0)
        v1c = jnp.concatenate(v1, axis=0)
        cand = jnp.concatenate([v0[0] + v1c, v0[1] + v1c[:8], v0[2] + v1c[:8], v0[3] + v1c[:8],
                                v0c + v1[0], v0c[:8] + v1[1], v0c[:8] + v1[2]], axis=0)
        cand = jnp.where(valid, cand, NEG_INF)
        best, poss = [], []
        for _ in range(k):
            m = jnp.max(cand, axis=0, keepdims=True)
            pos = jnp.min(jnp.where(cand == m, posb, float(k * k)), axis=0, keepdims=True)
            best.append(m)
            poss.append(pos)
            cand = jnp.where(posb == pos, NEG_INF, cand)
        best = jnp.concatenate(best, axis=0)
        posc = jnp.concatenate(poss, axis=0)
        a_rank = jnp.floor(posc * (1.0 / k))
        b_rank = posc - k * a_rank
        ea = jnp.zeros_like(posc)
        eb = jnp.zeros_like(posc)
        for r in range(k):
            ea = jnp.where(a_rank == float(r), i0[r], ea)
            eb = jnp.where(b_rank == float(r), i1[r], eb)
        e = jnp.exp(best - best[0:1, :])
        gate_ref[h * k:(h + 1) * k, :] = e / jnp.sum(e, axis=0, keepdims=True)
        ida_ref[h * k:(h + 1) * k, :] = ea
        idb_ref[h * k:(h + 1) * k, :] = eb


def _peer_topk(h2, wq_bf, keys_bf, tb):
    t, d = h2.shape
    nq = wq_bf.shape[1]
    rows = PEER_HEADS * PEER_TOPK
    out = jax.ShapeDtypeStruct((rows, t), F32)
    ospec = pl.BlockSpec((rows, tb), lambda i: (0, i))
    pos = _candidate_positions()
    return pl.pallas_call(
        _peer_topk_kernel,
        grid=(t // tb,),
        in_specs=[
            pl.BlockSpec((tb, d), lambda i: (i, 0)),
            pl.BlockSpec((d, nq), lambda i: (0, 0)),
            pl.BlockSpec((PEER_HEADS, 2, N_KEYS, PEER_HALF), lambda i: (0, 0, 0, 0)),
            pl.BlockSpec(pos.shape, lambda i: (0, 0)),
        ],
        out_specs=[ospec, ospec, ospec],
        out_shape=[out, out, out],
        compiler_params=_params("parallel"),
        name="peer_topk",
    )(h2, wq_bf, keys_bf, jnp.asarray(pos))


RSQRT2 = 1.0 / math.sqrt(2.0)


def _peer_mix_kernel(h2_ref, ida_ref, idb_ref, gate_ref, ut_ref, v_ref, x1_ref, g2_ref, lng_ref, lnb_ref,
                     x2_ref, w_scr, p_scr, acc_scr, h2_scr, *, tb, te, pitch):
    j = pl.program_id(1)
    rows_per_tile = te // N_KEYS
    half = N_KEYS // 2
    hi_mask = jnp.uint32(0xFFFF0000)

    @pl.when(j == 0)
    def _():
        acc_scr[...] = jnp.zeros_like(acc_scr)
        h2_scr[...] = h2_ref[...].astype(BF16)
        iota = lax.broadcasted_iota(jnp.int32, (N_KEYS, PEER_HEADS * PEER_TOPK), 0).astype(F32)

        def tok(t, carry):
            a_row = ida_ref[pl.ds(t, 1), :]
            b_row = idb_ref[pl.ds(t, 1), :]
            g_row = gate_ref[pl.ds(t, 1), :] * RSQRT2
            at = jnp.where(iota == a_row, 1.0, 0.0).astype(BF16)
            gbt = jnp.where(iota == b_row, g_row, 0.0).astype(BF16)
            w = pltpu.bitcast(_dot_nt(at, gbt), jnp.uint32)
            w_scr[pl.ds(t, half, stride=pitch), :] = (w[:half] & hi_mask) | (w[half:] >> 16)
            return carry

        lax.fori_loop(0, tb, tok, 0, unroll=32)

    a = _dot(h2_scr[...], ut_ref[...])
    for r in range(rows_per_tile):
        i1 = j * rows_per_tile + r
        packed = w_scr[pl.ds(pl.multiple_of((i1 % half) * pitch, 8), tb), :]
        shift = jnp.where(i1 >= half, 16, 0).astype(jnp.uint32)
        w = pltpu.bitcast((packed << shift) & hi_mask, F32)
        cs = slice(r * N_KEYS, (r + 1) * N_KEYS)
        ar = a[:, cs]
        p_scr[:, cs] = (w * (ar * (1.0 + lax.erf(ar)))).astype(BF16)
    acc_scr[...] += _dot(p_scr[...], v_ref[...])

    @pl.when(j == pl.num_programs(1) - 1)
    def _():
        x2_ref[...] = _layer_norm(DEEPNORM_ALPHA * x1_ref[...] + g2_ref[0] * acc_scr[...],
                                  lng_ref[...], lnb_ref[...])


def _peer_mix(h2, ida, idb, gate, ut_bf, v_bf, x1, mod_l, ln_g, ln_b, rows_of_tile, tb):
    t, d = x1.shape
    n_e = v_bf.shape[0]
    te = PEER_TE
    hk = PEER_HEADS * PEER_TOPK
    pitch = tb + 8
    mod3 = mod_l.reshape(8, 1, 6 * d)
    return pl.pallas_call(
        functools.partial(_peer_mix_kernel, tb=tb, te=te, pitch=pitch),
        grid=(t // tb, n_e // te),
        in_specs=[
            pl.BlockSpec((tb, d), lambda i, j: (i, 0)),
            pl.BlockSpec((tb, hk), lambda i, j: (i, 0)),
            pl.BlockSpec((tb, hk), lambda i, j: (i, 0)),
            pl.BlockSpec((tb, hk), lambda i, j: (i, 0)),
            pl.BlockSpec((d, te), lambda i, j: (0, j)),
            pl.BlockSpec((te, d), lambda i, j: (j, 0)),
            pl.BlockSpec((tb, d), lambda i, j: (i, 0)),
            pl.BlockSpec((1, 1, d), lambda i, j: (rows_of_tile(i), 0, 5)),
            pl.BlockSpec((1, d), lambda i, j: (0, 0)),
            pl.BlockSpec((1, d), lambda i, j: (0, 0)),
        ],
        out_specs=pl.BlockSpec((tb, d), lambda i, j: (i, 0)),
        out_shape=jax.ShapeDtypeStruct((t, d), F32),
        scratch_shapes=[pltpu.VMEM((pitch * N_KEYS // 2, N_KEYS), jnp.uint32), pltpu.VMEM((tb, te), BF16),
                        pltpu.VMEM((tb, d), F32), pltpu.VMEM((tb, d), BF16)],
        compiler_params=_params("parallel", "arbitrary"),
        name="peer_mix",
    )(h2, ida, idb, gate, ut_bf, v_bf, x1, mod3, ln_g.reshape(1, d), ln_b.reshape(1, d))


def _rope_tables(length):
    rows = length // GRID_W
    row = jnp.broadcast_to(jnp.arange(rows, dtype=F32)[:, None], (rows, GRID_W)).reshape(length)
    col = jnp.broadcast_to(jnp.arange(GRID_W, dtype=F32)[None, :], (rows, GRID_W)).reshape(length)
    n_freq = HEAD_DIM // 4
    inv = jnp.power(ROPE_THETA, -jnp.arange(n_freq, dtype=F32) / n_freq)
    ang = jnp.concatenate([row[:, None] * inv, col[:, None] * inv], axis=-1)
    cos, sin = jnp.cos(ang), jnp.sin(ang)
    return jnp.concatenate([cos, cos], axis=-1), jnp.concatenate([-sin, sin], axis=-1)


def kernel(x_prompt, x_sample, cache_attn_k, cache_attn_v, state_hgrn, c, c_ctx, w_mod, b_mod, w_in,
           hgrn_lb_logits, hgrn_norm_g, q_norm_g, k_norm_g, w_branch_rec, w_branch_att, w_out,
           ln1_g, ln1_b, ln2_g, ln2_b, peer_w_query, peer_sub_keys, peer_u, peer_v):
    nb, seq, d = x_prompt.shape
    db, dseq, _ = x_sample.shape
    depth = w_in.shape[0]
    past = cache_attn_k.shape[2]
    t_ctx, t_lat = nb * seq, db * dseq
    t_all = t_ctx + t_lat
    assert d == D_MODEL and 1 + db <= 8

    tm = _tile(math.gcd(t_ctx, dseq), 1024)
    tm_merge = _tile(math.gcd(t_ctx, dseq), 256)
    tb_peer = _tile(math.gcd(t_ctx, dseq), 256)
    tb_mix = _tile(math.gcd(t_ctx, dseq), PEER_TB_MIX)

    def rows_of_tile_fn(tile):
        n_ctx = t_ctx // tile
        per = dseq // tile
        return lambda i: jnp.where(i < n_ctx, 0, 1 + (i - n_ctx) // per)

    cond8 = jnp.concatenate([c_ctx[None, :], c, jnp.zeros((8 - 1 - db, d), F32)], axis=0)
    mod = _modulation(cond8, w_mod, b_mod)

    lb = jnp.cumsum(jax.nn.softmax(hgrn_lb_logits.astype(F32), axis=1), axis=1)
    lb = lb - lb[:, :1]

    segs = jnp.split(w_in, np.cumsum([HGRN_WIDTH] * 5 + [ATTN_WIDTH, KV_WIDTH, KV_WIDTH, D_MODEL]).tolist(), axis=-1)
    hq_w, hff_w, hfb_w, hi_w, hg_w, aq_w, ak_w, av_w, ga_w, gb_w = segs
    w_in_bf = jnp.concatenate([hq_w, hff_w, hfb_w, hi_w, hg_w, aq_w, ga_w, gb_w, ak_w, av_w], axis=-1).astype(BF16)
    w_rec_bf, w_att_bf, w_out_bf = (w.astype(BF16) for w in (w_branch_rec, w_branch_att, w_out))
    wq_bf = peer_w_query.astype(BF16)
    keys_bf = peer_sub_keys.astype(BF16)
    ut_bf = (jnp.swapaxes(peer_u, 1, 2) * RSQRT2).astype(BF16)
    v_bf = peer_v.astype(BF16)

    cos_t, sin_t = _rope_tables(dseq)
    ones_t, zeros_t = jnp.ones((seq, HEAD_DIM), F32), jnp.zeros((seq, HEAD_DIM), F32)
    cache_k_bf = jnp.transpose(cache_attn_k, (0, 1, 3, 2, 4)).astype(BF16)
    cache_v_t = jnp.transpose(cache_attn_v, (0, 1, 3, 2, 4))
    ones_pad = jnp.zeros(cache_v_t.shape, F32).at[..., 0].set(1.0)
    cache_v_aug = jnp.concatenate([cache_v_t, ones_pad], axis=-1).astype(BF16)

    seq_rows = [(i * seq, seq) for i in range(nb)] + [(t_ctx + i * dseq, dseq) for i in range(db)]
    zero_state = jnp.zeros((nb, 2, HGRN_HEADS, HGRN_DK, HGRN_DK), F32)

    x2d = jnp.concatenate([x_prompt.reshape(t_ctx, d), x_sample.reshape(t_lat, d)], axis=0)
    new_k, new_v, new_s = [], [], []
    for l in range(depth):
        z2d = _in_proj(x2d, mod[l], w_in_bf[l], rows_of_tile_fn(tm), tm)

        s0_all = jnp.concatenate([zero_state, state_hgrn[:, l]], axis=0)
        o_f, o_b, s_fin = _hgrn_scan(z2d, lb[:, l], s0_all, seq_rows, HGRN_CHUNK)
        new_s.append(s_fin[:nb])

        q_c, k_c, v_c, k_norm = _qkv_prep(z2d, 0, nb, seq, q_norm_g[l], k_norm_g[l], ones_t, zeros_t, False, True)
        o_att = _attention(q_c, k_c, v_c, None, 0, t_all)
        new_k.append(k_norm.reshape(nb, seq, KV_HEADS, HEAD_DIM))
        new_v.append(z2d[:t_ctx, COL_AV:].reshape(nb, seq, KV_HEADS, HEAD_DIM))

        q_s, k_s, v_s = _qkv_prep(z2d, t_ctx, db, dseq, q_norm_g[l], k_norm_g[l], cos_t, sin_t, True, False)
        k_s = jnp.concatenate([k_s, cache_k_bf[:, l]], axis=2)
        v_s = jnp.concatenate([v_s, cache_v_aug[:, l]], axis=2)
        o_att = _attention(q_s, k_s, v_s, o_att, t_ctx, t_all)

        x1, h2 = _merge(o_f, o_b, z2d, o_att, x2d, mod[l], w_rec_bf[l], w_att_bf[l], w_out_bf[l],
                        hgrn_norm_g[l], ln1_g[l], ln1_b[l], rows_of_tile_fn(tm_merge), tm_merge)

        ida, idb, gate = _peer_topk(h2, wq_bf[l], keys_bf[l], tb_peer)
        x2d = _peer_mix(h2, ida.T, idb.T, gate.T, ut_bf[l], v_bf[l], x1, mod[l], ln2_g[l], ln2_b[l],
                        rows_of_tile_fn(tb_mix), tb_mix)

    y_prompt = x2d[:t_ctx].reshape(nb, seq, d)
    y_sample = x2d[t_ctx:].reshape(db, dseq, d)
    return (y_prompt, y_sample, jnp.stack(new_k, axis=1), jnp.stack(new_v, axis=1), jnp.stack(new_s, axis=1))
```

```python
import functools
import math

import numpy as np
import jax
import jax.numpy as jnp
from jax import lax
from jax.experimental import pallas as pl
from jax.experimental.pallas import tpu as pltpu

F32 = jnp.float32
BF16 = jnp.bfloat16

D_MODEL = 1024
GRID_W = 64
HGRN_HEADS = 8
HGRN_DK = 128
HGRN_WIDTH = HGRN_HEADS * HGRN_DK
HGRN_CHUNK = 128
ATTN_HEADS = 16
KV_HEADS = 4
HEAD_DIM = 64
GROUP = ATTN_HEADS // KV_HEADS
ATTN_WIDTH = ATTN_HEADS * HEAD_DIM
KV_WIDTH = KV_HEADS * HEAD_DIM
ATTN_SCALE = HEAD_DIM ** -0.5
ROPE_THETA = 10000.0
PEER_HEADS = 8
PEER_HALF = 128
N_KEYS = 128
N_EXPERTS = N_KEYS * N_KEYS
PEER_TOPK = 16
MODEL_DEPTH = 4
DEEPNORM_ALPHA = (2 * MODEL_DEPTH) ** 0.25
EPS = 1e-6
NEG_INF = float("-inf")

IN_COLS = 5 * HGRN_WIDTH + ATTN_WIDTH + 2 * D_MODEL + 2 * KV_WIDTH
COL_AQ = 5 * HGRN_WIDTH
COL_GA = COL_AQ + ATTN_WIDTH
COL_GB = COL_GA + D_MODEL
COL_AK = COL_GB + D_MODEL
COL_AV = COL_AK + KV_WIDTH

VMEM_LIMIT = 56 * 1024 * 1024
PEER_VMEM_LIMIT = 60 * 1024 * 1024
LOG2_E = math.log2(math.e)
ATTN_TQ = 256
ATTN_TS = 512
PEER_TB_MIX = 512
PEER_TE = 2048


def _params(*sem):
    return pltpu.CompilerParams(dimension_semantics=sem, vmem_limit_bytes=VMEM_LIMIT)


def _tile(n, pref):
    t = min(n, pref)
    while n % t or t % 8:
        t -= 1
    return t


def _sigmoid(x):
    return 1.0 / (1.0 + jnp.exp(-x))


def _sigmoid_t(x):
    return 0.5 + 0.5 * jnp.tanh(0.5 * x)


def _silu(x):
    h = 0.5 * x
    return h + h * jnp.tanh(h)


def _bf16_floor(x):
    bits = pltpu.bitcast(x, jnp.uint32) & jnp.uint32(0xFFFF0000)
    return pltpu.bitcast(bits, F32)


def _dot(a, b):
    return jnp.dot(a, b, preferred_element_type=F32)


def _dot_nt(a, b):
    return lax.dot_general(a, b, (((1,), (1,)), ((), ())), preferred_element_type=F32)


def _mod_kernel(cond_ref, w_ref, b_ref, o_ref):
    c = cond_ref[...]
    s = c * _sigmoid(c)
    o_ref[0] = jnp.dot(s, w_ref[0], precision=lax.Precision.HIGHEST, preferred_element_type=F32) + b_ref[0]


def _modulation(cond8, w_mod, b_mod):
    depth, d, n = w_mod.shape
    tn = _tile(n, 1536)
    return pl.pallas_call(
        _mod_kernel,
        grid=(depth, n // tn),
        in_specs=[
            pl.BlockSpec((8, d), lambda l, j: (0, 0)),
            pl.BlockSpec((1, d, tn), lambda l, j: (l, 0, j)),
            pl.BlockSpec((1, 1, tn), lambda l, j: (l, 0, j)),
        ],
        out_specs=pl.BlockSpec((1, 8, tn), lambda l, j: (l, 0, j)),
        out_shape=jax.ShapeDtypeStruct((depth, 8, n), F32),
        compiler_params=_params("parallel", "parallel"),
        name="modulation",
    )(cond8, w_mod, b_mod.reshape(depth, 1, n))


def _inproj_kernel(x_ref, sh_ref, sc_ref, w_ref, z_ref, h_scr):
    @pl.when(pl.program_id(1) == 0)
    def _():
        h_scr[...] = (x_ref[...] * (1.0 + sc_ref[0]) + sh_ref[0]).astype(BF16)

    z_ref[...] = _dot(h_scr[...], w_ref[...])


def _in_proj(x2d, mod_l, w_in_bf, rows_of_tile, tm):
    t, d = x2d.shape
    n = w_in_bf.shape[1]
    tn = 512
    mod3 = mod_l.reshape(8, 1, 6 * d)
    return pl.pallas_call(
        _inproj_kernel,
        grid=(t // tm, n // tn),
        in_specs=[
            pl.BlockSpec((tm, d), lambda i, j: (i, 0)),
            pl.BlockSpec((1, 1, d), lambda i, j: (rows_of_tile(i), 0, 0)),
            pl.BlockSpec((1, 1, d), lambda i, j: (rows_of_tile(i), 0, 1)),
            pl.BlockSpec((d, tn), lambda i, j: (0, j)),
        ],
        out_specs=pl.BlockSpec((tm, tn), lambda i, j: (i, j)),
        out_shape=jax.ShapeDtypeStruct((t, n), F32),
        scratch_shapes=[pltpu.VMEM((tm, d), BF16)],
        compiler_params=_params("parallel", "arbitrary"),
        name="in_proj",
    )(x2d, mod3, mod3, w_in_bf)


def _level_reference_rows(beta, c):
    width = beta.shape[1]
    sub = lax.broadcasted_iota(jnp.int32, (8, width), 0)

    def bc(i, n=8):
        return jnp.broadcast_to(beta[i:i + 1, :], (n, width))

    out = []
    m = 1
    while m < c:
        blocks = []
        if m >= 8:
            for blk in range(c // (2 * m)):
                blocks.append(bc(blk * 2 * m + m, 2 * m))
        elif m == 4:
            for v in range(c // 8):
                blocks.append(bc(8 * v + 4))
        elif m == 2:
            for v in range(c // 8):
                blocks.append(jnp.where(sub < 4, bc(8 * v + 2), bc(8 * v + 6)))
        else:
            for v in range(c // 8):
                lo = jnp.where(sub < 2, bc(8 * v + 1), bc(8 * v + 3))
                hi = jnp.where(sub < 6, bc(8 * v + 5), bc(8 * v + 7))
                blocks.append(jnp.where(sub < 4, lo, hi))
        out.append(blocks[0] if len(blocks) == 1 else jnp.concatenate(blocks, axis=0))
        m *= 2
    return out


def _hgrn_direction(zq, zf, v, lb, st_ref, o_ref, tri, eye, q_half, pair_masks, reverse, c):
    heads = [slice(h * HGRN_DK, (h + 1) * HGRN_DK) for h in range(HGRN_HEADS)]
    q = _silu(zq)
    e = jnp.exp(-zf)
    log1pe = jnp.log(1.0 + e)
    g = jnp.log(1.0 + lb * e) - log1pe
    k = (1.0 - lb) * jnp.exp(-zf - log1pe)
    g1 = _bf16_floor(g)
    r1 = g - g1
    g2 = _bf16_floor(r1)
    g3 = r1 - g2
    beta = (_dot(tri, g1.astype(BF16)) + _dot(tri, g2.astype(BF16))) + _dot(tri, g3.astype(BF16))
    last = 0 if reverse else c - 1
    beta_tot = beta[last:last + 1, :]

    q_bf, k_bf, v_bf = q.astype(BF16), k.astype(BF16), v.astype(BF16)
    a = [jnp.where(eye, _dot_nt(q_bf[:, hs], k_bf[:, hs]), 0.0) for hs in heads]
    for r, qh, pm in zip(_level_reference_rows(beta, c), q_half, pair_masks):
        u = (jnp.where(qh, q, k) * jnp.exp(-jnp.abs(beta - r))).astype(BF16)
        a = [ah + jnp.where(pm, _dot_nt(u[:, hs], u[:, hs]), 0.0) for ah, hs in zip(a, heads)]

    qb = (q * jnp.exp(beta)).astype(BF16)
    kb = (k * jnp.exp(beta_tot - beta)).astype(BF16)
    dec = jnp.exp(beta_tot)
    for h, hs in enumerate(heads):
        st = st_ref[h]
        o_ref[:, hs] = _dot(a[h].astype(BF16), v_bf[:, hs]) + _dot_nt(qb[:, hs], st.astype(BF16))
        st_ref[h] = st * dec[:, hs] + lax.dot_general(
            v_bf[:, hs], kb[:, hs], (((0,), (0,)), ((), ())), preferred_element_type=F32)


def _hgrn_kernel(rbf_ref, rbb_ref, first_ref, seq_ref,
                 qf_ref, ff_ref, vf_ref, qb_ref, fb_ref, vb_ref, lb_ref, s0_ref,
                 of_ref, ob_ref, sout_ref, st_scr, *, c, n_last):
    s = pl.program_id(0)

    @pl.when(first_ref[s] == 1)
    def _():
        for d in range(2):
            for h in range(HGRN_HEADS):
                st_scr[d, h] = s0_ref[0, d, h].T

    row = lax.broadcasted_iota(jnp.int32, (c, c), 0)
    col = lax.broadcasted_iota(jnp.int32, (c, c), 1)
    rowd = lax.broadcasted_iota(jnp.int32, (c, HGRN_WIDTH), 0)
    eye = row == col

    for d, (q_ref, f_ref, v_ref, o_ref) in enumerate(
            ((qf_ref, ff_ref, vf_ref, of_ref), (qb_ref, fb_ref, vb_ref, ob_ref))):
        reverse = d == 1
        tri = (col >= row if reverse else col <= row).astype(BF16)
        q_par = 0 if reverse else 1
        q_half, pair_masks = [], []
        m = 1
        while m < c:
            q_half.append(((rowd // m) % 2) == q_par)
            pair_masks.append(((row // (2 * m)) == (col // (2 * m)))
                              & (((row // m) % 2) == q_par) & (((col // m) % 2) == 1 - q_par))
            m *= 2
        _hgrn_direction(q_ref[...], f_ref[...], v_ref[...], lb_ref[pl.ds(d, 1), :], st_scr.at[d], o_ref,
                        tri, eye, q_half, pair_masks, reverse, c)

    @pl.when(first_ref[jnp.minimum(s + 1, n_last)] == 1)
    def _():
        for d in range(2):
            for h in range(HGRN_HEADS):
                sout_ref[0, d, h] = st_scr[d, h].T


def _hgrn_scan(z2d, lb2, s0_all, seq_rows, c):
    t = z2d.shape[0]
    rbf, rbb, first, seq = [], [], [], []
    for si, (r0, length) in enumerate(seq_rows):
        n = length // c
        for ci in range(n):
            rbf.append(r0 // c + ci)
            rbb.append(r0 // c + n - 1 - ci)
            first.append(1 if ci == 0 else 0)
            seq.append(si)
    n_steps = len(rbf)
    first.append(1)
    w = HGRN_WIDTH

    def zspec(which, colblk):
        if which == 0:
            return pl.BlockSpec((c, w), lambda s, rbf, rbb, fi, sq: (rbf[s], colblk))
        return pl.BlockSpec((c, w), lambda s, rbf, rbb, fi, sq: (rbb[s], colblk))

    n_seq = len(seq_rows)
    state_spec = pl.BlockSpec((1, 2, HGRN_HEADS, HGRN_DK, HGRN_DK),
                              lambda s, rbf, rbb, fi, sq: (sq[s], 0, 0, 0, 0))
    grid_spec = pltpu.PrefetchScalarGridSpec(
        num_scalar_prefetch=4,
        grid=(n_steps,),
        in_specs=[zspec(0, 0), zspec(0, 1), zspec(0, 3), zspec(1, 0), zspec(1, 2), zspec(1, 3),
                  pl.BlockSpec((2, w), lambda s, *_: (0, 0)), state_spec],
        out_specs=[zspec(0, 0), zspec(1, 0), state_spec],
        scratch_shapes=[pltpu.VMEM((2, HGRN_HEADS, HGRN_DK, HGRN_DK), F32)],
    )
    return pl.pallas_call(
        functools.partial(_hgrn_kernel, c=c, n_last=n_steps),
        grid_spec=grid_spec,
        out_shape=[jax.ShapeDtypeStruct((t, w), F32), jax.ShapeDtypeStruct((t, w), F32),
                   jax.ShapeDtypeStruct((n_seq, 2, HGRN_HEADS, HGRN_DK, HGRN_DK), F32)],
        compiler_params=_params("arbitrary"),
        name="hgrn_scan",
    )(jnp.asarray(rbf, jnp.int32), jnp.asarray(rbb, jnp.int32), jnp.asarray(first, jnp.int32),
      jnp.asarray(seq, jnp.int32), z2d, z2d, z2d, z2d, z2d, z2d, lb2, s0_all)


def _qkv_prep_kernel(aq_ref, ak_ref, av_ref, qg_ref, kg_ref, cos_ref, sin_ref, swap_ref, *out_refs,
                     rope, q_scale, keep_k):
    q_out, k_out, v_out = out_refs[:3]
    hd = HEAD_DIM

    def norm_rope(x, g):
        ms = jnp.mean(x * x, axis=-1, keepdims=True)
        y = x * lax.rsqrt(ms + EPS) * g
        if rope:
            y = y * cos_ref[...] + _dot(y.astype(BF16), swap_ref[...]) * sin_ref[...]
        return y

    for j in range(ATTN_WIDTH // 128):
        pair = aq_ref[:, j * 128:(j + 1) * 128]
        for i in range(2):
            y = norm_rope(pair[:, i * hd:(i + 1) * hd], qg_ref[...])
            q_out[0, 2 * j + i] = (y * q_scale).astype(BF16)
    tl = ak_ref.shape[0]
    ones_col = jnp.where(lax.broadcasted_iota(jnp.int32, (tl, hd), 1) == 0, 1.0, 0.0).astype(BF16)
    for j in range(KV_WIDTH // 128):
        kpair = ak_ref[:, j * 128:(j + 1) * 128]
        vpair = av_ref[:, j * 128:(j + 1) * 128]
        for i in range(2):
            h = 2 * j + i
            y = norm_rope(kpair[:, i * hd:(i + 1) * hd], kg_ref[...])
            k_out[0, h] = y.astype(BF16)
            if keep_k:
                out_refs[3][:, h * hd:(h + 1) * hd] = y
            v_out[0, h, :, :hd] = vpair[:, i * hd:(i + 1) * hd].astype(BF16)
            v_out[0, h, :, hd:] = ones_col


def _qkv_prep(z2d, row0, n_seq, length, q_gain, k_gain, cos, sin, rope, keep_k):
    hd = HEAD_DIM
    tl = _tile(length, 512)
    nt = length // tl
    blk0 = row0 // tl
    swap = np.zeros((hd, hd), np.float32)
    for i in range(hd):
        swap[(i + hd // 2) % hd, i] = 1.0

    def zspec(width, col):
        return pl.BlockSpec((tl, width), lambda b, j: (blk0 + b * nt + j, col // width))

    def full(shape):
        return pl.BlockSpec(shape, lambda b, j: (0,) * len(shape))

    out_specs = [pl.BlockSpec((1, ATTN_HEADS, tl, hd), lambda b, j: (b, 0, j, 0)),
                 pl.BlockSpec((1, KV_HEADS, tl, hd), lambda b, j: (b, 0, j, 0)),
                 pl.BlockSpec((1, KV_HEADS, tl, 2 * hd), lambda b, j: (b, 0, j, 0))]
    out_shape = [jax.ShapeDtypeStruct((n_seq, ATTN_HEADS, length, hd), BF16),
                 jax.ShapeDtypeStruct((n_seq, KV_HEADS, length, hd), BF16),
                 jax.ShapeDtypeStruct((n_seq, KV_HEADS, length, 2 * hd), BF16)]
    if keep_k:
        out_specs.append(pl.BlockSpec((tl, KV_WIDTH), lambda b, j: (b * nt + j, 0)))
        out_shape.append(jax.ShapeDtypeStruct((n_seq * length, KV_WIDTH), F32))
    return pl.pallas_call(
        functools.partial(_qkv_prep_kernel, rope=rope, q_scale=ATTN_SCALE * LOG2_E, keep_k=keep_k),
        grid=(n_seq, nt),
        in_specs=[zspec(ATTN_WIDTH, COL_AQ), zspec(KV_WIDTH, COL_AK), zspec(KV_WIDTH, COL_AV),
                  full((1, hd)), full((1, hd)),
                  pl.BlockSpec((tl, hd), lambda b, j: (j, 0)), pl.BlockSpec((tl, hd), lambda b, j: (j, 0)),
                  full((hd, hd))],
        out_specs=out_specs,
        out_shape=out_shape,
        compiler_params=_params("parallel", "parallel"),
        name="qkv_prep",
    )(z2d, z2d, z2d, q_gain.reshape(1, hd), k_gain.reshape(1, hd), cos, sin, jnp.asarray(swap, BF16))


def _attn_kernel(q_ref, k_ref, v_ref, *rest, ts, n_s, tq):
    o_ref = rest[-1]
    rows = GROUP * tq
    q = q_ref[0].reshape(rows, HEAD_DIM)

    def body(i, carry):
        m, acc = carry
        sl = pl.ds(pl.multiple_of(i * ts, ts), ts)
        s = _dot_nt(q, k_ref[0, 0, sl, :])
        m_new = jnp.maximum(m, jnp.max(s, axis=-1, keepdims=True))
        p = jnp.exp2(s - m_new)
        acc = jnp.exp2(m - m_new) * acc + _dot(p.astype(BF16), v_ref[0, 0, sl, :])
        return m_new, acc

    m0 = jnp.full((rows, 1), NEG_INF, F32)
    acc0 = jnp.zeros((rows, 2 * HEAD_DIM), F32)
    m, acc = lax.fori_loop(0, n_s, body, (m0, acc0), unroll=True)
    o = acc[:, :HEAD_DIM] / acc[:, HEAD_DIM:HEAD_DIM + 1]
    for g in range(GROUP):
        o_ref[:, g * HEAD_DIM:(g + 1) * HEAD_DIM] = o[g * tq:(g + 1) * tq]


def _attention(q, k, v_aug, o_prev, row0, t_all):
    b, _, length, hd = q.shape
    s_len = k.shape[2]
    tq = _tile(length, ATTN_TQ)
    ts = _tile(s_len, ATTN_TS)
    nq = length // tq
    blk0 = row0 // tq
    in_specs = [
        pl.BlockSpec((1, GROUP, tq, hd), lambda bi, hi, qi: (bi, hi, qi, 0)),
        pl.BlockSpec((1, 1, s_len, hd), lambda bi, hi, qi: (bi, hi, 0, 0)),
        pl.BlockSpec((1, 1, s_len, 2 * hd), lambda bi, hi, qi: (bi, hi, 0, 0)),
    ]
    args = [q, k, v_aug]
    aliases = {}
    if o_prev is not None:
        in_specs.append(pl.BlockSpec(memory_space=pl.ANY))
        args.append(o_prev)
        aliases = {3: 0}
    return pl.pallas_call(
        functools.partial(_attn_kernel, ts=ts, n_s=s_len // ts, tq=tq),
        grid=(b, KV_HEADS, nq),
        in_specs=in_specs,
        out_specs=pl.BlockSpec((tq, GROUP * hd), lambda bi, hi, qi: (blk0 + bi * nq + qi, hi)),
        out_shape=jax.ShapeDtypeStruct((t_all, ATTN_WIDTH), F32),
        input_output_aliases=aliases,
        compiler_params=_params("parallel", "parallel", "parallel"),
        name="attention",
    )(*args)


def _layer_norm(t, g, b):
    mu = jnp.mean(t, axis=-1, keepdims=True)
    tc = t - mu
    var = jnp.mean(tc * tc, axis=-1, keepdims=True)
    return tc * lax.rsqrt(var + EPS) * g + b


def _merge_kernel(of_ref, ob_ref, hg_ref, ga_ref, gb_ref, oatt_ref, x_ref, g1_ref, sh2_ref, sc2_ref,
                  wrec_ref, watt_ref, wout_ref, ng_ref, lng_ref, lnb_ref, x1_ref, h2_ref, orec_scr):
    for h in range(HGRN_HEADS):
        cs = slice(h * HGRN_DK, (h + 1) * HGRN_DK)
        o = of_ref[:, cs] + ob_ref[:, cs]
        ms = jnp.mean(o * o, axis=-1, keepdims=True)
        zg = hg_ref[:, cs]
        orec_scr[:, cs] = (o * lax.rsqrt(ms + EPS) * ng_ref[...] * _silu(zg)).astype(BF16)
    y = (_sigmoid_t(ga_ref[...]) * _dot(orec_scr[...], wrec_ref[...])
         + _sigmoid_t(gb_ref[...]) * _dot(oatt_ref[...].astype(BF16), watt_ref[...]))
    u = _dot(y.astype(BF16), wout_ref[...])
    x1 = _layer_norm(DEEPNORM_ALPHA * x_ref[...] + g1_ref[0] * u, lng_ref[...], lnb_ref[...])
    x1_ref[...] = x1
    h2_ref[...] = x1 * (1.0 + sc2_ref[0]) + sh2_ref[0]


def _merge(o_f, o_b, z2d, o_att, x2d, mod_l, w_rec, w_att, w_out, norm_g, ln_g, ln_b, rows_of_tile, tm):
    t, d = x2d.shape
    mod3 = mod_l.reshape(8, 1, 6 * d)

    def rows(colblk):
        return pl.BlockSpec((tm, d), lambda i: (i, colblk))

    def modspec(chunk):
        return pl.BlockSpec((1, 1, d), lambda i: (rows_of_tile(i), 0, chunk))

    def full(shape):
        return pl.BlockSpec(shape, lambda i: (0,) * len(shape))

    return pl.pallas_call(
        _merge_kernel,
        grid=(t // tm,),
        in_specs=[rows(0), rows(0), rows(4), rows(COL_GA // d), rows(COL_GB // d), rows(0), rows(0),
                  modspec(2), modspec(3), modspec(4),
                  full((d, d)), full((d, d)), full((d, d)), full((1, HGRN_DK)), full((1, d)), full((1, d))],
        out_specs=[rows(0), rows(0)],
        out_shape=[jax.ShapeDtypeStruct((t, d), F32), jax.ShapeDtypeStruct((t, d), F32)],
        scratch_shapes=[pltpu.VMEM((tm, d), BF16)],
        compiler_params=_params("parallel"),
        name="merge",
    )(o_f, o_b, z2d, z2d, z2d, o_att, x2d, mod3, mod3, mod3, w_rec, w_att, w_out,
      norm_g.reshape(1, HGRN_DK), ln_g.reshape(1, d), ln_b.reshape(1, d))


def _top16(s, n):
    iota = lax.broadcasted_iota(jnp.int32, s.shape, 0).astype(F32)
    vals, idxs = [], []
    for _ in range(PEER_TOPK):
        m = jnp.max(s, axis=0, keepdims=True)
        idx = jnp.min(jnp.where(s == m, iota, float(n)), axis=0, keepdims=True)
        vals.append(m)
        idxs.append(idx)
        s = jnp.where(iota == idx, NEG_INF, s)
    return vals, idxs


def _candidate_positions():
    rows = []
    for a in range(4):
        for b in range(16 if a == 0 else 8):
            rows.append((a, b, (a + 1) * (b + 1) <= PEER_TOPK))
    for b, n_a in ((0, 16), (1, 8), (2, 8)):
        for a in range(n_a):
            rows.append((a, b, a >= 4 and (a + 1) * (b + 1) <= PEER_TOPK))
    return np.array([[a * PEER_TOPK + b if ok else -1.0] for a, b, ok in rows], np.float32)


def _peer_head_topk(qh, keys0, keys1, posb):
    k = PEER_TOPK
    tops = []
    for p, keys in enumerate((keys0, keys1)):
        st = _dot_nt(keys, qh[:, p * PEER_HALF:(p + 1) * PEER_HALF].astype(BF16))
        tops.append(_top16(st, N_KEYS))
    (v0, i0), (v1, i1) = tops
    v0c = jnp.concatenate(v0, axis=0)
    v1c = jnp.concatenate(v1, axis=0)
    cand = jnp.concatenate([v0[0] + v1c, v0[1] + v1c[:8], v0[2] + v1c[:8], v0[3] + v1c[:8],
                            v0c + v1[0], v0c[:8] + v1[1], v0c[:8] + v1[2]], axis=0)
    cand = jnp.where(posb >= 0.0, cand, NEG_INF)
    best, poss = [], []
    for _ in range(k):
        m = jnp.max(cand, axis=0, keepdims=True)
        pos = jnp.min(jnp.where(cand == m, posb, float(k * k)), axis=0, keepdims=True)
        best.append(m)
        poss.append(pos)
        cand = jnp.where(posb == pos, NEG_INF, cand)
    best = jnp.concatenate(best, axis=0)
    posc = jnp.concatenate(poss, axis=0)
    a_rank = jnp.floor(posc * (1.0 / k))
    b_rank = posc - k * a_rank
    ea = jnp.zeros_like(posc)
    eb = jnp.zeros_like(posc)
    for r in range(k):
        ea = jnp.where(a_rank == float(r), i0[r], ea)
        eb = jnp.where(b_rank == float(r), i1[r], eb)
    e = jnp.exp(best - best[0:1, :])
    return ea, eb, e / jnp.sum(e, axis=0, keepdims=True)


RSQRT2 = 1.0 / math.sqrt(2.0)


def _peer_kernel(h2n_ref, wq_ref, keys_ref, pos_ref, ut_ref, v_ref, x1_ref, g2_ref, lng_ref, lnb_ref,
                 x2_ref, ids_scr, idt_scr, w_scr, p_scr, acc_scr, h2n_scr, h2_scr, *, tb, te, pitch):
    i = pl.program_id(0)
    j = pl.program_id(1)
    slot = i % 2
    k = PEER_TOPK
    rows_per_tile = te // N_KEYS
    half = N_KEYS // 2
    hi_mask = jnp.uint32(0xFFFF0000)

    @pl.when(j == 0)
    def _():
        @pl.when(i == 0)
        def _():
            ids_scr[...] = jnp.zeros_like(ids_scr)
            h2n_scr[...] = jnp.zeros_like(h2n_scr)

        h2_scr[...] = h2n_scr[...]
        h2n_scr[...] = h2n_ref[...].astype(BF16)
        for n in range(3):
            idt_scr[n] = ids_scr[1 - slot, n].T
        acc_scr[...] = jnp.zeros_like(acc_scr)
        iota = lax.broadcasted_iota(jnp.int32, (N_KEYS, PEER_HEADS * k), 0).astype(F32)

        def tok(t, carry):
            a_row = idt_scr[0, pl.ds(t, 1), :]
            b_row = idt_scr[1, pl.ds(t, 1), :]
            g_row = idt_scr[2, pl.ds(t, 1), :] * RSQRT2
            at = jnp.where(iota == a_row, 1.0, 0.0).astype(BF16)
            gbt = jnp.where(iota == b_row, g_row, 0.0).astype(BF16)
            w = pltpu.bitcast(_dot_nt(at, gbt), jnp.uint32)
            w_scr[pl.ds(t, half, stride=pitch), :] = (w[:half] & hi_mask) | (w[half:] >> 16)
            return carry

        lax.fori_loop(0, tb, tok, 0, unroll=32)

    posb = jnp.broadcast_to(pos_ref[...], (pos_ref.shape[0], tb))
    ea, eb, gate = _peer_head_topk(_dot(h2n_scr[...], wq_ref[...]), keys_ref[0, 0], keys_ref[0, 1], posb)
    rows = pl.ds(pl.multiple_of(j * k, k), k)
    ids_scr[slot, 0, rows, :] = ea
    ids_scr[slot, 1, rows, :] = eb
    ids_scr[slot, 2, rows, :] = gate

    a = _dot(h2_scr[...], ut_ref[...])
    for r in range(rows_per_tile):
        i1 = j * rows_per_tile + r
        packed = w_scr[pl.ds(pl.multiple_of((i1 % half) * pitch, 8), tb), :]
        shift = jnp.where(i1 >= half, 16, 0).astype(jnp.uint32)
        w = pltpu.bitcast((packed << shift) & hi_mask, F32)
        cs = slice(r * N_KEYS, (r + 1) * N_KEYS)
        ar = a[:, cs]
        p_scr[:, cs] = (w * (ar * (1.0 + lax.erf(ar)))).astype(BF16)
    acc_scr[...] += _dot(p_scr[...], v_ref[...])

    @pl.when(j == pl.num_programs(1) - 1)
    def _():
        x2_ref[...] = _layer_norm(DEEPNORM_ALPHA * x1_ref[...] + g2_ref[0] * acc_scr[...],
                                  lng_ref[...], lnb_ref[...])


def _peer(h2, wq_bf, keys_bf, ut_bf, v_bf, x1, mod_l, ln_g, ln_b, rows_of_tile, tb):
    t, d = x1.shape
    n_e = v_bf.shape[0]
    te = PEER_TE
    assert n_e // te == PEER_HEADS
    hk = PEER_HEADS * PEER_TOPK
    pitch = tb + 8
    nblk = t // tb
    mod3 = mod_l.reshape(8, 1, 6 * d)
    pos = _candidate_positions()

    def cur(i):
        return jnp.maximum(i - 1, 0)

    def full(shape):
        return pl.BlockSpec(shape, lambda i, j: (0,) * len(shape))

    return pl.pallas_call(
        functools.partial(_peer_kernel, tb=tb, te=te, pitch=pitch),
        grid=(nblk + 1, PEER_HEADS),
        in_specs=[
            pl.BlockSpec((tb, d), lambda i, j: (jnp.minimum(i, nblk - 1), 0)),
            pl.BlockSpec((d, 2 * PEER_HALF), lambda i, j: (0, j)),
            pl.BlockSpec((1, 2, N_KEYS, PEER_HALF), lambda i, j: (j, 0, 0, 0)),
            full(pos.shape),
            pl.BlockSpec((d, te), lambda i, j: (0, j)),
            pl.BlockSpec((te, d), lambda i, j: (j, 0)),
            pl.BlockSpec((tb, d), lambda i, j: (cur(i), 0), pipeline_mode=pl.Buffered(1)),
            pl.BlockSpec((1, 1, d), lambda i, j: (rows_of_tile(cur(i)), 0, 5)),
            full((1, d)), full((1, d)),
        ],
        out_specs=pl.BlockSpec((tb, d), lambda i, j: (cur(i), 0)),
        out_shape=jax.ShapeDtypeStruct((t, d), F32),
        scratch_shapes=[pltpu.VMEM((2, 3, hk, tb), F32), pltpu.VMEM((3, tb, hk), F32),
                        pltpu.VMEM((pitch * N_KEYS // 2, N_KEYS), jnp.uint32), pltpu.VMEM((tb, te), BF16),
                        pltpu.VMEM((tb, d), F32), pltpu.VMEM((tb, d), BF16), pltpu.VMEM((tb, d), BF16)],
        compiler_params=pltpu.CompilerParams(dimension_semantics=("arbitrary", "arbitrary"),
                                             vmem_limit_bytes=PEER_VMEM_LIMIT),
        name="peer",
    )(h2, wq_bf, keys_bf, jnp.asarray(pos), ut_bf, v_bf, x1, mod3, ln_g.reshape(1, d), ln_b.reshape(1, d))


def _rope_tables(length):
    rows = length // GRID_W
    row = jnp.broadcast_to(jnp.arange(rows, dtype=F32)[:, None], (rows, GRID_W)).reshape(length)
    col = jnp.broadcast_to(jnp.arange(GRID_W, dtype=F32)[None, :], (rows, GRID_W)).reshape(length)
    n_freq = HEAD_DIM // 4
    inv = jnp.power(ROPE_THETA, -jnp.arange(n_freq, dtype=F32) / n_freq)
    ang = jnp.concatenate([row[:, None] * inv, col[:, None] * inv], axis=-1)
    cos, sin = jnp.cos(ang), jnp.sin(ang)
    return jnp.concatenate([cos, cos], axis=-1), jnp.concatenate([-sin, sin], axis=-1)


def kernel(x_prompt, x_sample, cache_attn_k, cache_attn_v, state_hgrn, c, c_ctx, w_mod, b_mod, w_in,
           hgrn_lb_logits, hgrn_norm_g, q_norm_g, k_norm_g, w_branch_rec, w_branch_att, w_out,
           ln1_g, ln1_b, ln2_g, ln2_b, peer_w_query, peer_sub_keys, peer_u, peer_v):
    nb, seq, d = x_prompt.shape
    db, dseq, _ = x_sample.shape
    depth = w_in.shape[0]
    past = cache_attn_k.shape[2]
    t_ctx, t_lat = nb * seq, db * dseq
    t_all = t_ctx + t_lat
    assert d == D_MODEL and 1 + db <= 8

    tm = _tile(math.gcd(t_ctx, dseq), 1024)
    tm_merge = _tile(math.gcd(t_ctx, dseq), 256)
    tb_mix = _tile(math.gcd(t_ctx, dseq), PEER_TB_MIX)

    def rows_of_tile_fn(tile):
        n_ctx = t_ctx // tile
        per = dseq // tile
        return lambda i: jnp.where(i < n_ctx, 0, 1 + (i - n_ctx) // per)

    cond8 = jnp.concatenate([c_ctx[None, :], c, jnp.zeros((8 - 1 - db, d), F32)], axis=0)
    mod = _modulation(cond8, w_mod, b_mod)

    lb = jnp.cumsum(jax.nn.softmax(hgrn_lb_logits.astype(F32), axis=1), axis=1)
    lb = lb - lb[:, :1]

    segs = jnp.split(w_in, np.cumsum([HGRN_WIDTH] * 5 + [ATTN_WIDTH, KV_WIDTH, KV_WIDTH, D_MODEL]).tolist(), axis=-1)
    hq_w, hff_w, hfb_w, hi_w, hg_w, aq_w, ak_w, av_w, ga_w, gb_w = segs
    w_in_bf = jnp.concatenate([hq_w, hff_w, hfb_w, hi_w, hg_w, aq_w, ga_w, gb_w, ak_w, av_w], axis=-1).astype(BF16)
    w_rec_bf, w_att_bf, w_out_bf = (w.astype(BF16) for w in (w_branch_rec, w_branch_att, w_out))
    wq_bf = peer_w_query.astype(BF16)
    keys_bf = peer_sub_keys.astype(BF16)
    ut_bf = (jnp.swapaxes(peer_u, 1, 2) * RSQRT2).astype(BF16)
    v_bf = peer_v.astype(BF16)

    cos_t, sin_t = _rope_tables(dseq)
    ones_t, zeros_t = jnp.ones((seq, HEAD_DIM), F32), jnp.zeros((seq, HEAD_DIM), F32)
    cache_k_bf = jnp.transpose(cache_attn_k, (0, 1, 3, 2, 4)).astype(BF16)
    cache_v_t = jnp.transpose(cache_attn_v, (0, 1, 3, 2, 4))
    ones_pad = jnp.zeros(cache_v_t.shape, F32).at[..., 0].set(1.0)
    cache_v_aug = jnp.concatenate([cache_v_t, ones_pad], axis=-1).astype(BF16)

    seq_rows = [(i * seq, seq) for i in range(nb)] + [(t_ctx + i * dseq, dseq) for i in range(db)]
    zero_state = jnp.zeros((nb, 2, HGRN_HEADS, HGRN_DK, HGRN_DK), F32)

    x2d = jnp.concatenate([x_prompt.reshape(t_ctx, d), x_sample.reshape(t_lat, d)], axis=0)
    new_k, new_v, new_s = [], [], []
    for l in range(depth):
        z2d = _in_proj(x2d, mod[l], w_in_bf[l], rows_of_tile_fn(tm), tm)

        s0_all = jnp.concatenate([zero_state, state_hgrn[:, l]], axis=0)
        o_f, o_b, s_fin = _hgrn_scan(z2d, lb[:, l], s0_all, seq_rows, HGRN_CHUNK)
        new_s.append(s_fin[:nb])

        q_c, k_c, v_c, k_norm = _qkv_prep(z2d, 0, nb, seq, q_norm_g[l], k_norm_g[l], ones_t, zeros_t, False, True)
        o_att = _attention(q_c, k_c, v_c, None, 0, t_all)
        new_k.append(k_norm.reshape(nb, seq, KV_HEADS, HEAD_DIM))
        new_v.append(z2d[:t_ctx, COL_AV:].reshape(nb, seq, KV_HEADS, HEAD_DIM))

        q_s, k_s, v_s = _qkv_prep(z2d, t_ctx, db, dseq, q_norm_g[l], k_norm_g[l], cos_t, sin_t, True, False)
        k_s = jnp.concatenate([k_s, cache_k_bf[:, l]], axis=2)
        v_s = jnp.concatenate([v_s, cache_v_aug[:, l]], axis=2)
        o_att = _attention(q_s, k_s, v_s, o_att, t_ctx, t_all)

        x1, h2 = _merge(o_f, o_b, z2d, o_att, x2d, mod[l], w_rec_bf[l], w_att_bf[l], w_out_bf[l],
                        hgrn_norm_g[l], ln1_g[l], ln1_b[l], rows_of_tile_fn(tm_merge), tm_merge)

        x2d = _peer(h2, wq_bf[l], keys_bf[l], ut_bf[l], v_bf[l], x1, mod[l], ln2_g[l], ln2_b[l],
                    rows_of_tile_fn(tb_mix), tb_mix)

    y_prompt = x2d[:t_ctx].reshape(nb, seq, d)
    y_sample = x2d[t_ctx:].reshape(db, dseq, d)
    return (y_prompt, y_sample, jnp.stack(new_k, axis=1), jnp.stack(new_v, axis=1), jnp.stack(new_s, axis=1))
```

```python
import functools
import math

import numpy as np
import jax
import jax.numpy as jnp
from jax import lax
from jax.experimental import pallas as pl
from jax.experimental.pallas import tpu as pltpu

F32 = jnp.float32
BF16 = jnp.bfloat16

D_MODEL = 1024
GRID_W = 64
HGRN_HEADS = 8
HGRN_DK = 128
HGRN_WIDTH = HGRN_HEADS * HGRN_DK
HGRN_CHUNK = 128
ATTN_HEADS = 16
KV_HEADS = 4
HEAD_DIM = 64
GROUP = ATTN_HEADS // KV_HEADS
ATTN_WIDTH = ATTN_HEADS * HEAD_DIM
KV_WIDTH = KV_HEADS * HEAD_DIM
ATTN_SCALE = HEAD_DIM ** -0.5
ROPE_THETA = 10000.0
PEER_HEADS = 8
PEER_HALF = 128
N_KEYS = 128
N_EXPERTS = N_KEYS * N_KEYS
PEER_TOPK = 16
MODEL_DEPTH = 4
DEEPNORM_ALPHA = (2 * MODEL_DEPTH) ** 0.25
EPS = 1e-6
NEG_INF = float("-inf")

IN_COLS = 5 * HGRN_WIDTH + ATTN_WIDTH + 2 * D_MODEL + 2 * KV_WIDTH
COL_AQ = 5 * HGRN_WIDTH
COL_GA = COL_AQ + ATTN_WIDTH
COL_GB = COL_GA + D_MODEL
COL_AK = COL_GB + D_MODEL
COL_AV = COL_AK + KV_WIDTH

VMEM_LIMIT = 56 * 1024 * 1024
PEER_VMEM_LIMIT = 60 * 1024 * 1024
LOG2_E = math.log2(math.e)
ATTN_TQ = 256
ATTN_TS = 512
PEER_TB_MIX = 512
PEER_TE = 2048


def _params(*sem):
    return pltpu.CompilerParams(dimension_semantics=sem, vmem_limit_bytes=VMEM_LIMIT)


def _tile(n, pref):
    t = min(n, pref)
    while n % t or t % 8:
        t -= 1
    return t


def _sigmoid(x):
    return 1.0 / (1.0 + jnp.exp(-x))


def _sigmoid_t(x):
    return 0.5 + 0.5 * jnp.tanh(0.5 * x)


def _silu(x):
    h = 0.5 * x
    return h + h * jnp.tanh(h)


def _bf16_floor(x):
    bits = pltpu.bitcast(x, jnp.uint32) & jnp.uint32(0xFFFF0000)
    return pltpu.bitcast(bits, F32)


def _dot(a, b):
    return jnp.dot(a, b, preferred_element_type=F32)


def _dot_nt(a, b):
    return lax.dot_general(a, b, (((1,), (1,)), ((), ())), preferred_element_type=F32)


def _mod_kernel(cond_ref, w_ref, b_ref, o_ref):
    c = cond_ref[...]
    s = c * _sigmoid(c)
    o_ref[0] = jnp.dot(s, w_ref[0], precision=lax.Precision.HIGHEST, preferred_element_type=F32) + b_ref[0]


def _modulation(cond8, w_mod, b_mod):
    depth, d, n = w_mod.shape
    tn = _tile(n, 1536)
    return pl.pallas_call(
        _mod_kernel,
        grid=(depth, n // tn),
        in_specs=[
            pl.BlockSpec((8, d), lambda l, j: (0, 0)),
            pl.BlockSpec((1, d, tn), lambda l, j: (l, 0, j)),
            pl.BlockSpec((1, 1, tn), lambda l, j: (l, 0, j)),
        ],
        out_specs=pl.BlockSpec((1, 8, tn), lambda l, j: (l, 0, j)),
        out_shape=jax.ShapeDtypeStruct((depth, 8, n), F32),
        compiler_params=_params("parallel", "parallel"),
        name="modulation",
    )(cond8, w_mod, b_mod.reshape(depth, 1, n))


def _inproj_kernel(x_ref, sh_ref, sc_ref, w_ref, z_ref, h_scr):
    @pl.when(pl.program_id(1) == 0)
    def _():
        h_scr[...] = (x_ref[...] * (1.0 + sc_ref[0]) + sh_ref[0]).astype(BF16)

    z_ref[...] = _dot(h_scr[...], w_ref[...])


def _in_proj(x2d, mod_l, w_in_bf, rows_of_tile, tm):
    t, d = x2d.shape
    n = w_in_bf.shape[1]
    tn = 512
    mod3 = mod_l.reshape(8, 1, 6 * d)
    return pl.pallas_call(
        _inproj_kernel,
        grid=(t // tm, n // tn),
        in_specs=[
            pl.BlockSpec((tm, d), lambda i, j: (i, 0)),
            pl.BlockSpec((1, 1, d), lambda i, j: (rows_of_tile(i), 0, 0)),
            pl.BlockSpec((1, 1, d), lambda i, j: (rows_of_tile(i), 0, 1)),
            pl.BlockSpec((d, tn), lambda i, j: (0, j)),
        ],
        out_specs=pl.BlockSpec((tm, tn), lambda i, j: (i, j)),
        out_shape=jax.ShapeDtypeStruct((t, n), F32),
        scratch_shapes=[pltpu.VMEM((tm, d), BF16)],
        compiler_params=_params("parallel", "arbitrary"),
        name="in_proj",
    )(x2d, mod3, mod3, w_in_bf)


def _level_reference_rows(beta, c):
    width = beta.shape[1]
    sub = lax.broadcasted_iota(jnp.int32, (8, width), 0)

    def bc(i, n=8):
        return jnp.broadcast_to(beta[i:i + 1, :], (n, width))

    out = []
    m = 1
    while m < c:
        blocks = []
        if m >= 8:
            for blk in range(c // (2 * m)):
                blocks.append(bc(blk * 2 * m + m, 2 * m))
        elif m == 4:
            for v in range(c // 8):
                blocks.append(bc(8 * v + 4))
        elif m == 2:
            for v in range(c // 8):
                blocks.append(jnp.where(sub < 4, bc(8 * v + 2), bc(8 * v + 6)))
        else:
            for v in range(c // 8):
                lo = jnp.where(sub < 2, bc(8 * v + 1), bc(8 * v + 3))
                hi = jnp.where(sub < 6, bc(8 * v + 5), bc(8 * v + 7))
                blocks.append(jnp.where(sub < 4, lo, hi))
        out.append(blocks[0] if len(blocks) == 1 else jnp.concatenate(blocks, axis=0))
        m *= 2
    return out


def _hgrn_direction(zq, zf, v, lb, st_ref, o_ref, tri, eye, q_half, pair_masks, reverse, c):
    heads = [slice(h * HGRN_DK, (h + 1) * HGRN_DK) for h in range(HGRN_HEADS)]
    q = zq * pl.reciprocal(1.0 + jnp.exp2(zq * -LOG2_E), approx=True)
    zl = zf * -LOG2_E
    e = jnp.exp2(zl)
    l2 = jnp.log(1.0 + e) * LOG2_E
    g = jnp.log(1.0 + lb * e) * LOG2_E - l2
    k = (1.0 - lb) * jnp.exp2(zl - l2)
    g1 = _bf16_floor(g)
    r1 = g - g1
    g2 = _bf16_floor(r1)
    g3 = r1 - g2
    beta = (_dot(tri, g1.astype(BF16)) + _dot(tri, g2.astype(BF16))) + _dot(tri, g3.astype(BF16))
    last = 0 if reverse else c - 1
    beta_tot = beta[last:last + 1, :]

    q_bf, k_bf, v_bf = q.astype(BF16), k.astype(BF16), v.astype(BF16)
    a = [jnp.where(eye, _dot_nt(q_bf[:, hs], k_bf[:, hs]), 0.0) for hs in heads]
    for r, qh, pm in zip(_level_reference_rows(beta, c), q_half, pair_masks):
        neg_abs = pltpu.bitcast(pltpu.bitcast(beta - r, jnp.uint32) | jnp.uint32(0x80000000), F32)
        u = (jnp.where(qh, q, k) * jnp.exp2(neg_abs)).astype(BF16)
        a = [ah + jnp.where(pm, _dot_nt(u[:, hs], u[:, hs]), 0.0) for ah, hs in zip(a, heads)]

    qb = (q * jnp.exp2(beta)).astype(BF16)
    kb = (k * jnp.exp2(beta_tot - beta)).astype(BF16)
    dec = jnp.exp2(beta_tot)
    for h, hs in enumerate(heads):
        st = st_ref[h]
        o_ref[:, hs] = _dot(a[h].astype(BF16), v_bf[:, hs]) + _dot_nt(qb[:, hs], st.astype(BF16))
        st_ref[h] = st * dec[:, hs] + lax.dot_general(
            v_bf[:, hs], kb[:, hs], (((0,), (0,)), ((), ())), preferred_element_type=F32)


def _hgrn_kernel(rbf_ref, rbb_ref, first_ref, seq_ref,
                 qf_ref, ff_ref, vf_ref, qb_ref, fb_ref, vb_ref, lb_ref, s0_ref,
                 of_ref, ob_ref, sout_ref, st_scr, *, c, n_last):
    s = pl.program_id(0)

    @pl.when(first_ref[s] == 1)
    def _():
        for d in range(2):
            for h in range(HGRN_HEADS):
                st_scr[d, h] = s0_ref[0, d, h].T

    row = lax.broadcasted_iota(jnp.int32, (c, c), 0)
    col = lax.broadcasted_iota(jnp.int32, (c, c), 1)
    rowd = lax.broadcasted_iota(jnp.int32, (c, HGRN_WIDTH), 0)
    eye = row == col

    for d, (q_ref, f_ref, v_ref, o_ref) in enumerate(
            ((qf_ref, ff_ref, vf_ref, of_ref), (qb_ref, fb_ref, vb_ref, ob_ref))):
        reverse = d == 1
        tri = (col >= row if reverse else col <= row).astype(BF16)
        q_par = 0 if reverse else 1
        q_half, pair_masks = [], []
        m = 1
        while m < c:
            q_half.append(((rowd // m) % 2) == q_par)
            pair_masks.append(((row // (2 * m)) == (col // (2 * m)))
                              & (((row // m) % 2) == q_par) & (((col // m) % 2) == 1 - q_par))
            m *= 2
        _hgrn_direction(q_ref[...], f_ref[...], v_ref[...], lb_ref[pl.ds(d, 1), :], st_scr.at[d], o_ref,
                        tri, eye, q_half, pair_masks, reverse, c)

    @pl.when(first_ref[jnp.minimum(s + 1, n_last)] == 1)
    def _():
        for d in range(2):
            for h in range(HGRN_HEADS):
                sout_ref[0, d, h] = st_scr[d, h].T


def _hgrn_scan(z2d, lb2, s0_all, seq_rows, c):
    t = z2d.shape[0]
    rbf, rbb, first, seq = [], [], [], []
    for si, (r0, length) in enumerate(seq_rows):
        n = length // c
        for ci in range(n):
            rbf.append(r0 // c + ci)
            rbb.append(r0 // c + n - 1 - ci)
            first.append(1 if ci == 0 else 0)
            seq.append(si)
    n_steps = len(rbf)
    first.append(1)
    w = HGRN_WIDTH

    def zspec(which, colblk):
        if which == 0:
            return pl.BlockSpec((c, w), lambda s, rbf, rbb, fi, sq: (rbf[s], colblk))
        return pl.BlockSpec((c, w), lambda s, rbf, rbb, fi, sq: (rbb[s], colblk))

    n_seq = len(seq_rows)
    state_spec = pl.BlockSpec((1, 2, HGRN_HEADS, HGRN_DK, HGRN_DK),
                              lambda s, rbf, rbb, fi, sq: (sq[s], 0, 0, 0, 0))
    grid_spec = pltpu.PrefetchScalarGridSpec(
        num_scalar_prefetch=4,
        grid=(n_steps,),
        in_specs=[zspec(0, 0), zspec(0, 1), zspec(0, 3), zspec(1, 0), zspec(1, 2), zspec(1, 3),
                  pl.BlockSpec((2, w), lambda s, *_: (0, 0)), state_spec],
        out_specs=[zspec(0, 0), zspec(1, 0), state_spec],
        scratch_shapes=[pltpu.VMEM((2, HGRN_HEADS, HGRN_DK, HGRN_DK), F32)],
    )
    return pl.pallas_call(
        functools.partial(_hgrn_kernel, c=c, n_last=n_steps),
        grid_spec=grid_spec,
        out_shape=[jax.ShapeDtypeStruct((t, w), F32), jax.ShapeDtypeStruct((t, w), F32),
                   jax.ShapeDtypeStruct((n_seq, 2, HGRN_HEADS, HGRN_DK, HGRN_DK), F32)],
        compiler_params=_params("arbitrary"),
        name="hgrn_scan",
    )(jnp.asarray(rbf, jnp.int32), jnp.asarray(rbb, jnp.int32), jnp.asarray(first, jnp.int32),
      jnp.asarray(seq, jnp.int32), z2d, z2d, z2d, z2d, z2d, z2d, lb2, s0_all)


def _qkv_prep_kernel(aq_ref, ak_ref, av_ref, qg_ref, kg_ref, cos_ref, sin_ref, swap_ref, *out_refs,
                     rope, q_scale, keep_k):
    q_out, k_out, v_out = out_refs[:3]
    hd = HEAD_DIM

    def norm_rope(x, g):
        ms = jnp.mean(x * x, axis=-1, keepdims=True)
        y = x * lax.rsqrt(ms + EPS) * g
        if rope:
            y = y * cos_ref[...] + _dot(y.astype(BF16), swap_ref[...]) * sin_ref[...]
        return y

    for j in range(ATTN_WIDTH // 128):
        pair = aq_ref[:, j * 128:(j + 1) * 128]
        for i in range(2):
            y = norm_rope(pair[:, i * hd:(i + 1) * hd], qg_ref[...])
            q_out[0, 2 * j + i] = (y * q_scale).astype(BF16)
    tl = ak_ref.shape[0]
    ones_col = jnp.where(lax.broadcasted_iota(jnp.int32, (tl, hd), 1) == 0, 1.0, 0.0).astype(BF16)
    for j in range(KV_WIDTH // 128):
        kpair = ak_ref[:, j * 128:(j + 1) * 128]
        vpair = av_ref[:, j * 128:(j + 1) * 128]
        for i in range(2):
            h = 2 * j + i
            y = norm_rope(kpair[:, i * hd:(i + 1) * hd], kg_ref[...])
            k_out[0, h] = y.astype(BF16)
            if keep_k:
                out_refs[3][:, h * hd:(h + 1) * hd] = y
            v_out[0, h, :, :hd] = vpair[:, i * hd:(i + 1) * hd].astype(BF16)
            v_out[0, h, :, hd:] = ones_col


def _qkv_prep(z2d, row0, n_seq, length, q_gain, k_gain, cos, sin, rope, keep_k):
    hd = HEAD_DIM
    tl = _tile(length, 512)
    nt = length // tl
    blk0 = row0 // tl
    swap = np.zeros((hd, hd), np.float32)
    for i in range(hd):
        swap[(i + hd // 2) % hd, i] = 1.0

    def zspec(width, col):
        return pl.BlockSpec((tl, width), lambda b, j: (blk0 + b * nt + j, col // width))

    def full(shape):
        return pl.BlockSpec(shape, lambda b, j: (0,) * len(shape))

    out_specs = [pl.BlockSpec((1, ATTN_HEADS, tl, hd), lambda b, j: (b, 0, j, 0)),
                 pl.BlockSpec((1, KV_HEADS, tl, hd), lambda b, j: (b, 0, j, 0)),
                 pl.BlockSpec((1, KV_HEADS, tl, 2 * hd), lambda b, j: (b, 0, j, 0))]
    out_shape = [jax.ShapeDtypeStruct((n_seq, ATTN_HEADS, length, hd), BF16),
                 jax.ShapeDtypeStruct((n_seq, KV_HEADS, length, hd), BF16),
                 jax.ShapeDtypeStruct((n_seq, KV_HEADS, length, 2 * hd), BF16)]
    if keep_k:
        out_specs.append(pl.BlockSpec((tl, KV_WIDTH), lambda b, j: (b * nt + j, 0)))
        out_shape.append(jax.ShapeDtypeStruct((n_seq * length, KV_WIDTH), F32))
    return pl.pallas_call(
        functools.partial(_qkv_prep_kernel, rope=rope, q_scale=ATTN_SCALE * LOG2_E, keep_k=keep_k),
        grid=(n_seq, nt),
        in_specs=[zspec(ATTN_WIDTH, COL_AQ), zspec(KV_WIDTH, COL_AK), zspec(KV_WIDTH, COL_AV),
                  full((1, hd)), full((1, hd)),
                  pl.BlockSpec((tl, hd), lambda b, j: (j, 0)), pl.BlockSpec((tl, hd), lambda b, j: (j, 0)),
                  full((hd, hd))],
        out_specs=out_specs,
        out_shape=out_shape,
        compiler_params=_params("parallel", "parallel"),
        name="qkv_prep",
    )(z2d, z2d, z2d, q_gain.reshape(1, hd), k_gain.reshape(1, hd), cos, sin, jnp.asarray(swap, BF16))


def _attn_kernel(q_ref, k_ref, v_ref, *rest, ts, n_s, tq):
    o_ref = rest[-1]
    rows = GROUP * tq
    q = q_ref[0].reshape(rows, HEAD_DIM)

    def body(i, carry):
        m, acc = carry
        sl = pl.ds(pl.multiple_of(i * ts, ts), ts)
        s = _dot_nt(q, k_ref[0, 0, sl, :])
        m_new = jnp.maximum(m, jnp.max(s, axis=-1, keepdims=True))
        p = jnp.exp2(s - m_new)
        acc = jnp.exp2(m - m_new) * acc + _dot(p.astype(BF16), v_ref[0, 0, sl, :])
        return m_new, acc

    m0 = jnp.full((rows, 1), NEG_INF, F32)
    acc0 = jnp.zeros((rows, 2 * HEAD_DIM), F32)
    m, acc = lax.fori_loop(0, n_s, body, (m0, acc0), unroll=True)
    o = acc[:, :HEAD_DIM] / acc[:, HEAD_DIM:HEAD_DIM + 1]
    for g in range(GROUP):
        o_ref[:, g * HEAD_DIM:(g + 1) * HEAD_DIM] = o[g * tq:(g + 1) * tq]


def _attention(q, k, v_aug, o_prev, row0, t_all):
    b, _, length, hd = q.shape
    s_len = k.shape[2]
    tq = _tile(length, ATTN_TQ)
    ts = _tile(s_len, ATTN_TS)
    nq = length // tq
    blk0 = row0 // tq
    in_specs = [
        pl.BlockSpec((1, GROUP, tq, hd), lambda bi, hi, qi: (bi, hi, qi, 0)),
        pl.BlockSpec((1, 1, s_len, hd), lambda bi, hi, qi: (bi, hi, 0, 0)),
        pl.BlockSpec((1, 1, s_len, 2 * hd), lambda bi, hi, qi: (bi, hi, 0, 0)),
    ]
    args = [q, k, v_aug]
    aliases = {}
    if o_prev is not None:
        in_specs.append(pl.BlockSpec(memory_space=pl.ANY))
        args.append(o_prev)
        aliases = {3: 0}
    return pl.pallas_call(
        functools.partial(_attn_kernel, ts=ts, n_s=s_len // ts, tq=tq),
        grid=(b, KV_HEADS, nq),
        in_specs=in_specs,
        out_specs=pl.BlockSpec((tq, GROUP * hd), lambda bi, hi, qi: (blk0 + bi * nq + qi, hi)),
        out_shape=jax.ShapeDtypeStruct((t_all, ATTN_WIDTH), F32),
        input_output_aliases=aliases,
        compiler_params=_params("parallel", "parallel", "parallel"),
        name="attention",
    )(*args)


def _layer_norm(t, g, b):
    mu = jnp.mean(t, axis=-1, keepdims=True)
    tc = t - mu
    var = jnp.mean(tc * tc, axis=-1, keepdims=True)
    return tc * lax.rsqrt(var + EPS) * g + b


def _merge_kernel(of_ref, ob_ref, hg_ref, ga_ref, gb_ref, oatt_ref, x_ref, g1_ref, sh2_ref, sc2_ref,
                  wrec_ref, watt_ref, wout_ref, ng_ref, lng_ref, lnb_ref, x1_ref, h2_ref, orec_scr):
    for h in range(HGRN_HEADS):
        cs = slice(h * HGRN_DK, (h + 1) * HGRN_DK)
        o = of_ref[:, cs] + ob_ref[:, cs]
        ms = jnp.mean(o * o, axis=-1, keepdims=True)
        zg = hg_ref[:, cs]
        orec_scr[:, cs] = (o * lax.rsqrt(ms + EPS) * ng_ref[...] * _silu(zg)).astype(BF16)
    y = (_sigmoid_t(ga_ref[...]) * _dot(orec_scr[...], wrec_ref[...])
         + _sigmoid_t(gb_ref[...]) * _dot(oatt_ref[...].astype(BF16), watt_ref[...]))
    u = _dot(y.astype(BF16), wout_ref[...])
    x1 = _layer_norm(DEEPNORM_ALPHA * x_ref[...] + g1_ref[0] * u, lng_ref[...], lnb_ref[...])
    x1_ref[...] = x1
    h2_ref[...] = x1 * (1.0 + sc2_ref[0]) + sh2_ref[0]


def _merge(o_f, o_b, z2d, o_att, x2d, mod_l, w_rec, w_att, w_out, norm_g, ln_g, ln_b, rows_of_tile, tm):
    t, d = x2d.shape
    mod3 = mod_l.reshape(8, 1, 6 * d)

    def rows(colblk):
        return pl.BlockSpec((tm, d), lambda i: (i, colblk))

    def modspec(chunk):
        return pl.BlockSpec((1, 1, d), lambda i: (rows_of_tile(i), 0, chunk))

    def full(shape):
        return pl.BlockSpec(shape, lambda i: (0,) * len(shape))

    return pl.pallas_call(
        _merge_kernel,
        grid=(t // tm,),
        in_specs=[rows(0), rows(0), rows(4), rows(COL_GA // d), rows(COL_GB // d), rows(0), rows(0),
                  modspec(2), modspec(3), modspec(4),
                  full((d, d)), full((d, d)), full((d, d)), full((1, HGRN_DK)), full((1, d)), full((1, d))],
        out_specs=[rows(0), rows(0)],
        out_shape=[jax.ShapeDtypeStruct((t, d), F32), jax.ShapeDtypeStruct((t, d), F32)],
        scratch_shapes=[pltpu.VMEM((tm, d), BF16)],
        compiler_params=_params("parallel"),
        name="merge",
    )(o_f, o_b, z2d, z2d, z2d, o_att, x2d, mod3, mod3, mod3, w_rec, w_att, w_out,
      norm_g.reshape(1, HGRN_DK), ln_g.reshape(1, d), ln_b.reshape(1, d))


def _top16(s, n):
    iota = lax.broadcasted_iota(jnp.int32, s.shape, 0).astype(F32)
    vals, idxs = [], []
    for _ in range(PEER_TOPK):
        m = jnp.max(s, axis=0, keepdims=True)
        idx = jnp.min(jnp.where(s == m, iota, float(n)), axis=0, keepdims=True)
        vals.append(m)
        idxs.append(idx)
        s = jnp.where(iota == idx, NEG_INF, s)
    return vals, idxs


def _candidate_positions():
    rows = []
    for a in range(4):
        for b in range(16 if a == 0 else 8):
            rows.append((a, b, (a + 1) * (b + 1) <= PEER_TOPK))
    for b, n_a in ((0, 16), (1, 8), (2, 8)):
        for a in range(n_a):
            rows.append((a, b, a >= 4 and (a + 1) * (b + 1) <= PEER_TOPK))
    return np.array([[a * PEER_TOPK + b if ok else -1.0] for a, b, ok in rows], np.float32)


def _peer_head_topk(qh, keys0, keys1, posb):
    k = PEER_TOPK
    tops = []
    for p, keys in enumerate((keys0, keys1)):
        st = _dot_nt(keys, qh[:, p * PEER_HALF:(p + 1) * PEER_HALF].astype(BF16))
        tops.append(_top16(st, N_KEYS))
    (v0, i0), (v1, i1) = tops
    v0c = jnp.concatenate(v0, axis=0)
    v1c = jnp.concatenate(v1, axis=0)
    cand = jnp.concatenate([v0[0] + v1c, v0[1] + v1c[:8], v0[2] + v1c[:8], v0[3] + v1c[:8],
                            v0c + v1[0], v0c[:8] + v1[1], v0c[:8] + v1[2]], axis=0)
    cand = jnp.where(posb >= 0.0, cand, NEG_INF)
    best, poss = [], []
    for _ in range(k):
        m = jnp.max(cand, axis=0, keepdims=True)
        pos = jnp.min(jnp.where(cand == m, posb, float(k * k)), axis=0, keepdims=True)
        best.append(m)
        poss.append(pos)
        cand = jnp.where(posb == pos, NEG_INF, cand)
    best = jnp.concatenate(best, axis=0)
    posc = jnp.concatenate(poss, axis=0)
    a_rank = jnp.floor(posc * (1.0 / k))
    b_rank = posc - k * a_rank
    ea = jnp.zeros_like(posc)
    eb = jnp.zeros_like(posc)
    for r in range(k):
        ea = jnp.where(a_rank == float(r), i0[r], ea)
        eb = jnp.where(b_rank == float(r), i1[r], eb)
    e = jnp.exp(best - best[0:1, :])
    return ea, eb, e / jnp.sum(e, axis=0, keepdims=True)


RSQRT2 = 1.0 / math.sqrt(2.0)


def _peer_kernel(h2n_ref, wq_ref, keys_ref, pos_ref, u_ref, v_ref, x1_ref, g2_ref, lng_ref, lnb_ref,
                 x2_ref, ids_scr, idt_scr, w_scr, p_scr, acc_scr, h2n_scr, h2_scr, *, tb, te, pitch):
    i = pl.program_id(0)
    j = pl.program_id(1)
    slot = i % 2
    k = PEER_TOPK
    rows_per_tile = te // N_KEYS
    half = N_KEYS // 2
    hi_mask = jnp.uint32(0xFFFF0000)

    @pl.when(j == 0)
    def _():
        @pl.when(i == 0)
        def _():
            ids_scr[...] = jnp.zeros_like(ids_scr)
            h2n_scr[...] = jnp.zeros_like(h2n_scr)

        h2_scr[...] = h2n_scr[...]
        h2n_scr[...] = h2n_ref[...].astype(BF16)
        for n in range(3):
            idt_scr[n] = ids_scr[1 - slot, n].T
        acc_scr[...] = jnp.zeros_like(acc_scr)
        iota = lax.broadcasted_iota(jnp.int32, (N_KEYS, PEER_HEADS * k), 0).astype(F32)

        def tok(t, carry):
            a_row = idt_scr[0, pl.ds(t, 1), :]
            b_row = idt_scr[1, pl.ds(t, 1), :]
            g_row = idt_scr[2, pl.ds(t, 1), :] * RSQRT2
            at = jnp.where(iota == a_row, 1.0, 0.0).astype(BF16)
            gbt = jnp.where(iota == b_row, g_row, 0.0).astype(BF16)
            w = pltpu.bitcast(_dot_nt(at, gbt), jnp.uint32)
            w_scr[pl.ds(t, half, stride=pitch), :] = (w[:half] & hi_mask) | (w[half:] >> 16)
            return carry

        lax.fori_loop(0, tb, tok, 0, unroll=32)

    posb = jnp.broadcast_to(pos_ref[...], (pos_ref.shape[0], tb))
    ea, eb, gate = _peer_head_topk(_dot(h2n_scr[...], wq_ref[...]), keys_ref[0, 0], keys_ref[0, 1], posb)
    rows = pl.ds(pl.multiple_of(j * k, k), k)
    ids_scr[slot, 0, rows, :] = ea
    ids_scr[slot, 1, rows, :] = eb
    ids_scr[slot, 2, rows, :] = gate

    a = _dot_nt(h2_scr[...], u_ref[...])
    for r in range(rows_per_tile):
        i1 = j * rows_per_tile + r
        packed = w_scr[pl.ds(pl.multiple_of((i1 % half) * pitch, 8), tb), :]
        shift = jnp.where(i1 >= half, 16, 0).astype(jnp.uint32)
        w = pltpu.bitcast((packed << shift) & hi_mask, F32)
        cs = slice(r * N_KEYS, (r + 1) * N_KEYS)
        ar = a[:, cs]
        p_scr[:, cs] = (w * (ar * (1.0 + lax.erf(ar)))).astype(BF16)
    acc_scr[...] += _dot(p_scr[...], v_ref[...])

    @pl.when(j == pl.num_programs(1) - 1)
    def _():
        x2_ref[...] = _layer_norm(DEEPNORM_ALPHA * x1_ref[...] + g2_ref[0] * acc_scr[...],
                                  lng_ref[...], lnb_ref[...])


def _peer(h2, wq_bf, keys_bf, u_bf, v_bf, x1, mod_l, ln_g, ln_b, rows_of_tile, tb):
    t, d = x1.shape
    n_e = v_bf.shape[0]
    te = PEER_TE
    assert n_e // te == PEER_HEADS
    hk = PEER_HEADS * PEER_TOPK
    pitch = tb + 8
    nblk = t // tb
    mod3 = mod_l.reshape(8, 1, 6 * d)
    pos = _candidate_positions()

    def cur(i):
        return jnp.maximum(i - 1, 0)

    def full(shape):
        return pl.BlockSpec(shape, lambda i, j: (0,) * len(shape))

    return pl.pallas_call(
        functools.partial(_peer_kernel, tb=tb, te=te, pitch=pitch),
        grid=(nblk + 1, PEER_HEADS),
        in_specs=[
            pl.BlockSpec((tb, d), lambda i, j: (jnp.minimum(i, nblk - 1), 0)),
            pl.BlockSpec((d, 2 * PEER_HALF), lambda i, j: (0, j)),
            pl.BlockSpec((1, 2, N_KEYS, PEER_HALF), lambda i, j: (j, 0, 0, 0)),
            full(pos.shape),
            pl.BlockSpec((te, d), lambda i, j: (j, 0)),
            pl.BlockSpec((te, d), lambda i, j: (j, 0)),
            pl.BlockSpec((tb, d), lambda i, j: (cur(i), 0), pipeline_mode=pl.Buffered(1)),
            pl.BlockSpec((1, 1, d), lambda i, j: (rows_of_tile(cur(i)), 0, 5)),
            full((1, d)), full((1, d)),
        ],
        out_specs=pl.BlockSpec((tb, d), lambda i, j: (cur(i), 0)),
        out_shape=jax.ShapeDtypeStruct((t, d), F32),
        scratch_shapes=[pltpu.VMEM((2, 3, hk, tb), F32), pltpu.VMEM((3, tb, hk), F32),
                        pltpu.VMEM((pitch * N_KEYS // 2, N_KEYS), jnp.uint32), pltpu.VMEM((tb, te), BF16),
                        pltpu.VMEM((tb, d), F32), pltpu.VMEM((tb, d), BF16), pltpu.VMEM((tb, d), BF16)],
        compiler_params=pltpu.CompilerParams(dimension_semantics=("arbitrary", "arbitrary"),
                                             vmem_limit_bytes=PEER_VMEM_LIMIT),
        name="peer",
    )(h2, wq_bf, keys_bf, jnp.asarray(pos), u_bf, v_bf, x1, mod3, ln_g.reshape(1, d), ln_b.reshape(1, d))


def _rope_tables(length):
    rows = length // GRID_W
    row = jnp.broadcast_to(jnp.arange(rows, dtype=F32)[:, None], (rows, GRID_W)).reshape(length)
    col = jnp.broadcast_to(jnp.arange(GRID_W, dtype=F32)[None, :], (rows, GRID_W)).reshape(length)
    n_freq = HEAD_DIM // 4
    inv = jnp.power(ROPE_THETA, -jnp.arange(n_freq, dtype=F32) / n_freq)
    ang = jnp.concatenate([row[:, None] * inv, col[:, None] * inv], axis=-1)
    cos, sin = jnp.cos(ang), jnp.sin(ang)
    return jnp.concatenate([cos, cos], axis=-1), jnp.concatenate([-sin, sin], axis=-1)


def kernel(x_prompt, x_sample, cache_attn_k, cache_attn_v, state_hgrn, c, c_ctx, w_mod, b_mod, w_in,
           hgrn_lb_logits, hgrn_norm_g, q_norm_g, k_norm_g, w_branch_rec, w_branch_att, w_out,
           ln1_g, ln1_b, ln2_g, ln2_b, peer_w_query, peer_sub_keys, peer_u, peer_v):
    nb, seq, d = x_prompt.shape
    db, dseq, _ = x_sample.shape
    depth = w_in.shape[0]
    past = cache_attn_k.shape[2]
    t_ctx, t_lat = nb * seq, db * dseq
    t_all = t_ctx + t_lat
    assert d == D_MODEL and 1 + db <= 8

    tm = _tile(math.gcd(t_ctx, dseq), 2048)
    tm_merge = _tile(math.gcd(t_ctx, dseq), 256)
    tb_mix = _tile(math.gcd(t_ctx, dseq), PEER_TB_MIX)

    def rows_of_tile_fn(tile):
        n_ctx = t_ctx // tile
        per = dseq // tile
        return lambda i: jnp.where(i < n_ctx, 0, 1 + (i - n_ctx) // per)

    cond8 = jnp.concatenate([c_ctx[None, :], c, jnp.zeros((8 - 1 - db, d), F32)], axis=0)
    mod = _modulation(cond8, w_mod, b_mod)

    lb = jnp.cumsum(jax.nn.softmax(hgrn_lb_logits.astype(F32), axis=1), axis=1)
    lb = lb - lb[:, :1]

    segs = jnp.split(w_in, np.cumsum([HGRN_WIDTH] * 5 + [ATTN_WIDTH, KV_WIDTH, KV_WIDTH, D_MODEL]).tolist(), axis=-1)
    hq_w, hff_w, hfb_w, hi_w, hg_w, aq_w, ak_w, av_w, ga_w, gb_w = segs
    w_in_bf = jnp.concatenate([hq_w, hff_w, hfb_w, hi_w, hg_w, aq_w, ga_w, gb_w, ak_w, av_w], axis=-1).astype(BF16)
    w_rec_bf, w_att_bf, w_out_bf = (w.astype(BF16) for w in (w_branch_rec, w_branch_att, w_out))
    wq_bf = peer_w_query.astype(BF16)
    keys_bf = peer_sub_keys.astype(BF16)
    u_bf = (peer_u * RSQRT2).astype(BF16)
    v_bf = peer_v.astype(BF16)

    cos_t, sin_t = _rope_tables(dseq)
    ones_t, zeros_t = jnp.ones((seq, HEAD_DIM), F32), jnp.zeros((seq, HEAD_DIM), F32)
    cache_k_bf = jnp.transpose(cache_attn_k, (0, 1, 3, 2, 4)).astype(BF16)
    cache_v_t = jnp.transpose(cache_attn_v, (0, 1, 3, 2, 4))
    ones_pad = jnp.zeros(cache_v_t.shape, F32).at[..., 0].set(1.0)
    cache_v_aug = jnp.concatenate([cache_v_t, ones_pad], axis=-1).astype(BF16)

    seq_rows = [(i * seq, seq) for i in range(nb)] + [(t_ctx + i * dseq, dseq) for i in range(db)]
    zero_state = jnp.zeros((nb, 2, HGRN_HEADS, HGRN_DK, HGRN_DK), F32)

    x2d = jnp.concatenate([x_prompt.reshape(t_ctx, d), x_sample.reshape(t_lat, d)], axis=0)
    new_k, new_v, new_s = [], [], []
    for l in range(depth):
        z2d = _in_proj(x2d, mod[l], w_in_bf[l], rows_of_tile_fn(tm), tm)

        s0_all = jnp.concatenate([zero_state, state_hgrn[:, l]], axis=0)
        o_f, o_b, s_fin = _hgrn_scan(z2d, lb[:, l], s0_all, seq_rows, HGRN_CHUNK)
        new_s.append(s_fin[:nb])

        q_c, k_c, v_c, k_norm = _qkv_prep(z2d, 0, nb, seq, q_norm_g[l], k_norm_g[l], ones_t, zeros_t, False, True)
        o_att = _attention(q_c, k_c, v_c, None, 0, t_all)
        new_k.append(k_norm.reshape(nb, seq, KV_HEADS, HEAD_DIM))
        new_v.append(z2d[:t_ctx, COL_AV:].reshape(nb, seq, KV_HEADS, HEAD_DIM))

        q_s, k_s, v_s = _qkv_prep(z2d, t_ctx, db, dseq, q_norm_g[l], k_norm_g[l], cos_t, sin_t, True, False)
        k_s = jnp.concatenate([k_s, cache_k_bf[:, l]], axis=2)
        v_s = jnp.concatenate([v_s, cache_v_aug[:, l]], axis=2)
        o_att = _attention(q_s, k_s, v_s, o_att, t_ctx, t_all)

        x1, h2 = _merge(o_f, o_b, z2d, o_att, x2d, mod[l], w_rec_bf[l], w_att_bf[l], w_out_bf[l],
                        hgrn_norm_g[l], ln1_g[l], ln1_b[l], rows_of_tile_fn(tm_merge), tm_merge)

        x2d = _peer(h2, wq_bf[l], keys_bf[l], u_bf[l], v_bf[l], x1, mod[l], ln2_g[l], ln2_b[l],
                    rows_of_tile_fn(tb_mix), tb_mix)

    y_prompt = x2d[:t_ctx].reshape(nb, seq, d)
    y_sample = x2d[t_ctx:].reshape(db, dseq, d)
    return (y_prompt, y_sample, jnp.stack(new_k, axis=1), jnp.stack(new_v, axis=1), jnp.stack(new_s, axis=1))
```

```python
import functools
import math

import numpy as np
import jax
import jax.numpy as jnp
from jax import lax
from jax.experimental import pallas as pl
from jax.experimental.pallas import tpu as pltpu

F32 = jnp.float32
BF16 = jnp.bfloat16

D_MODEL = 1024
GRID_W = 64
HGRN_HEADS = 8
HGRN_DK = 128
HGRN_WIDTH = HGRN_HEADS * HGRN_DK
HGRN_CHUNK = 128
ATTN_HEADS = 16
KV_HEADS = 4
HEAD_DIM = 64
GROUP = ATTN_HEADS // KV_HEADS
ATTN_WIDTH = ATTN_HEADS * HEAD_DIM
KV_WIDTH = KV_HEADS * HEAD_DIM
ATTN_SCALE = HEAD_DIM ** -0.5
ROPE_THETA = 10000.0
PEER_HEADS = 8
PEER_HALF = 128
N_KEYS = 128
N_EXPERTS = N_KEYS * N_KEYS
PEER_TOPK = 16
MODEL_DEPTH = 4
DEEPNORM_ALPHA = (2 * MODEL_DEPTH) ** 0.25
EPS = 1e-6
NEG_INF = float("-inf")

IN_COLS = 5 * HGRN_WIDTH + ATTN_WIDTH + 2 * D_MODEL + 2 * KV_WIDTH
COL_AQ = 5 * HGRN_WIDTH
COL_GA = COL_AQ + ATTN_WIDTH
COL_GB = COL_GA + D_MODEL
COL_AK = COL_GB + D_MODEL
COL_AV = COL_AK + KV_WIDTH

VMEM_LIMIT = 56 * 1024 * 1024
PEER_VMEM_LIMIT = 60 * 1024 * 1024
LOG2_E = math.log2(math.e)
ATTN_TQ = 512
ATTN_TS = 512
PEER_TB_MIX = 512
PEER_TE = 2048


def _params(*sem):
    return pltpu.CompilerParams(dimension_semantics=sem, vmem_limit_bytes=VMEM_LIMIT)


def _tile(n, pref):
    t = min(n, pref)
    while n % t or t % 8:
        t -= 1
    return t


def _sigmoid(x):
    return 1.0 / (1.0 + jnp.exp(-x))


def _sigmoid_t(x):
    return 0.5 + 0.5 * jnp.tanh(0.5 * x)


def _silu(x):
    h = 0.5 * x
    return h + h * jnp.tanh(h)


def _bf16_floor(x):
    bits = pltpu.bitcast(x, jnp.uint32) & jnp.uint32(0xFFFF0000)
    return pltpu.bitcast(bits, F32)


def _dot(a, b):
    return jnp.dot(a, b, preferred_element_type=F32)


def _dot_nt(a, b):
    return lax.dot_general(a, b, (((1,), (1,)), ((), ())), preferred_element_type=F32)


def _mod_kernel(cond_ref, w_ref, b_ref, o_ref):
    c = cond_ref[...]
    s = c * _sigmoid(c)
    o_ref[0] = jnp.dot(s, w_ref[0], precision=lax.Precision.HIGHEST, preferred_element_type=F32) + b_ref[0]


def _modulation(cond8, w_mod, b_mod):
    depth, d, n = w_mod.shape
    tn = _tile(n, 1536)
    return pl.pallas_call(
        _mod_kernel,
        grid=(depth, n // tn),
        in_specs=[
            pl.BlockSpec((8, d), lambda l, j: (0, 0)),
            pl.BlockSpec((1, d, tn), lambda l, j: (l, 0, j)),
            pl.BlockSpec((1, 1, tn), lambda l, j: (l, 0, j)),
        ],
        out_specs=pl.BlockSpec((1, 8, tn), lambda l, j: (l, 0, j)),
        out_shape=jax.ShapeDtypeStruct((depth, 8, n), F32),
        compiler_params=_params("parallel", "parallel"),
        name="modulation",
    )(cond8, w_mod, b_mod.reshape(depth, 1, n))


def _inproj_kernel(x_ref, sh_ref, sc_ref, w_ref, z_ref, h_scr):
    @pl.when(pl.program_id(1) == 0)
    def _():
        h_scr[...] = (x_ref[...] * (1.0 + sc_ref[0]) + sh_ref[0]).astype(BF16)

    z_ref[...] = _dot(h_scr[...], w_ref[...])


def _in_proj(x2d, mod_l, w_in_bf, rows_of_tile, tm):
    t, d = x2d.shape
    n = w_in_bf.shape[1]
    tn = 512
    mod3 = mod_l.reshape(8, 1, 6 * d)
    return pl.pallas_call(
        _inproj_kernel,
        grid=(t // tm, n // tn),
        in_specs=[
            pl.BlockSpec((tm, d), lambda i, j: (i, 0)),
            pl.BlockSpec((1, 1, d), lambda i, j: (rows_of_tile(i), 0, 0)),
            pl.BlockSpec((1, 1, d), lambda i, j: (rows_of_tile(i), 0, 1)),
            pl.BlockSpec((d, tn), lambda i, j: (0, j)),
        ],
        out_specs=pl.BlockSpec((tm, tn), lambda i, j: (i, j)),
        out_shape=jax.ShapeDtypeStruct((t, n), F32),
        scratch_shapes=[pltpu.VMEM((tm, d), BF16)],
        compiler_params=_params("parallel", "arbitrary"),
        name="in_proj",
    )(x2d, mod3, mod3, w_in_bf)


def _level_reference_rows(beta, c):
    width = beta.shape[1]
    sub = lax.broadcasted_iota(jnp.int32, (8, width), 0)

    def bc(i, n=8):
        return jnp.broadcast_to(beta[i:i + 1, :], (n, width))

    out = []
    m = 1
    while m < c:
        blocks = []
        if m >= 8:
            for blk in range(c // (2 * m)):
                blocks.append(bc(blk * 2 * m + m, 2 * m))
        elif m == 4:
            for v in range(c // 8):
                blocks.append(bc(8 * v + 4))
        elif m == 2:
            for v in range(c // 8):
                blocks.append(jnp.where(sub < 4, bc(8 * v + 2), bc(8 * v + 6)))
        else:
            for v in range(c // 8):
                lo = jnp.where(sub < 2, bc(8 * v + 1), bc(8 * v + 3))
                hi = jnp.where(sub < 6, bc(8 * v + 5), bc(8 * v + 7))
                blocks.append(jnp.where(sub < 4, lo, hi))
        out.append(blocks[0] if len(blocks) == 1 else jnp.concatenate(blocks, axis=0))
        m *= 2
    return out


def _hgrn_direction(zq, zf, v, lb, st_ref, o_ref, tri, eye, q_half, pair_masks, reverse, c):
    heads = [slice(h * HGRN_DK, (h + 1) * HGRN_DK) for h in range(HGRN_HEADS)]
    q = zq * pl.reciprocal(1.0 + jnp.exp2(zq * -LOG2_E), approx=True)
    zl = zf * -LOG2_E
    e = jnp.exp2(zl)
    l2 = jnp.log(1.0 + e) * LOG2_E
    g = jnp.log(1.0 + lb * e) * LOG2_E - l2
    k = (1.0 - lb) * jnp.exp2(zl - l2)
    g1 = _bf16_floor(g)
    r1 = g - g1
    g2 = _bf16_floor(r1)
    g3 = r1 - g2
    beta = (_dot(tri, g1.astype(BF16)) + _dot(tri, g2.astype(BF16))) + _dot(tri, g3.astype(BF16))
    last = 0 if reverse else c - 1
    beta_tot = beta[last:last + 1, :]

    q_bf, k_bf, v_bf = q.astype(BF16), k.astype(BF16), v.astype(BF16)
    a = [jnp.where(eye, _dot_nt(q_bf[:, hs], k_bf[:, hs]), 0.0) for hs in heads]
    for r, qh, pm in zip(_level_reference_rows(beta, c), q_half, pair_masks):
        neg_abs = pltpu.bitcast(pltpu.bitcast(beta - r, jnp.uint32) | jnp.uint32(0x80000000), F32)
        qk = jnp.concatenate([jnp.where(qh, q[:, hs], k[:, hs]) for hs in heads], axis=1)
        u = (qk * jnp.exp2(neg_abs)).astype(BF16)
        a = [ah + jnp.where(pm, _dot_nt(u[:, hs], u[:, hs]), 0.0) for ah, hs in zip(a, heads)]

    qb = (q * jnp.exp2(beta)).astype(BF16)
    kb = (k * jnp.exp2(beta_tot - beta)).astype(BF16)
    dec = jnp.exp2(beta_tot)
    for h, hs in enumerate(heads):
        st = st_ref[h]
        o_ref[:, hs] = _dot(a[h].astype(BF16), v_bf[:, hs]) + _dot_nt(qb[:, hs], st.astype(BF16))
        st_ref[h] = st * dec[:, hs] + lax.dot_general(
            v_bf[:, hs], kb[:, hs], (((0,), (0,)), ((), ())), preferred_element_type=F32)


def _hgrn_kernel(rbf_ref, rbb_ref, first_ref, seq_ref,
                 qf_ref, ff_ref, vf_ref, qb_ref, fb_ref, vb_ref, lb_ref, s0_ref,
                 of_ref, ob_ref, sout_ref, st_scr, *, c, n_last):
    s = pl.program_id(0)

    @pl.when(first_ref[s] == 1)
    def _():
        for d in range(2):
            for h in range(HGRN_HEADS):
                st_scr[d, h] = s0_ref[0, d, h].T

    row = lax.broadcasted_iota(jnp.int32, (c, c), 0)
    col = lax.broadcasted_iota(jnp.int32, (c, c), 1)
    rowd = lax.broadcasted_iota(jnp.int32, (c, HGRN_DK), 0)
    eye = row == col

    for d, (q_ref, f_ref, v_ref, o_ref) in enumerate(
            ((qf_ref, ff_ref, vf_ref, of_ref), (qb_ref, fb_ref, vb_ref, ob_ref))):
        reverse = d == 1
        tri = (col >= row if reverse else col <= row).astype(BF16)
        q_par = 0 if reverse else 1
        q_half, pair_masks = [], []
        m = 1
        while m < c:
            q_half.append(((rowd // m) % 2) == q_par)
            pair_masks.append(((row // (2 * m)) == (col // (2 * m)))
                              & (((row // m) % 2) == q_par) & (((col // m) % 2) == 1 - q_par))
            m *= 2
        _hgrn_direction(q_ref[...], f_ref[...], v_ref[...], lb_ref[pl.ds(d, 1), :], st_scr.at[d], o_ref,
                        tri, eye, q_half, pair_masks, reverse, c)

    @pl.when(first_ref[jnp.minimum(s + 1, n_last)] == 1)
    def _():
        for d in range(2):
            for h in range(HGRN_HEADS):
                sout_ref[0, d, h] = st_scr[d, h].T


def _hgrn_scan(z2d, lb2, s0_all, seq_rows, c):
    t = z2d.shape[0]
    rbf, rbb, first, seq = [], [], [], []
    for si, (r0, length) in enumerate(seq_rows):
        n = length // c
        for ci in range(n):
            rbf.append(r0 // c + ci)
            rbb.append(r0 // c + n - 1 - ci)
            first.append(1 if ci == 0 else 0)
            seq.append(si)
    n_steps = len(rbf)
    first.append(1)
    w = HGRN_WIDTH

    def zspec(which, colblk):
        if which == 0:
            return pl.BlockSpec((c, w), lambda s, rbf, rbb, fi, sq: (rbf[s], colblk))
        return pl.BlockSpec((c, w), lambda s, rbf, rbb, fi, sq: (rbb[s], colblk))

    n_seq = len(seq_rows)
    state_spec = pl.BlockSpec((1, 2, HGRN_HEADS, HGRN_DK, HGRN_DK),
                              lambda s, rbf, rbb, fi, sq: (sq[s], 0, 0, 0, 0))
    grid_spec = pltpu.PrefetchScalarGridSpec(
        num_scalar_prefetch=4,
        grid=(n_steps,),
        in_specs=[zspec(0, 0), zspec(0, 1), zspec(0, 3), zspec(1, 0), zspec(1, 2), zspec(1, 3),
                  pl.BlockSpec((2, w), lambda s, *_: (0, 0)), state_spec],
        out_specs=[zspec(0, 0), zspec(1, 0), state_spec],
        scratch_shapes=[pltpu.VMEM((2, HGRN_HEADS, HGRN_DK, HGRN_DK), F32)],
    )
    return pl.pallas_call(
        functools.partial(_hgrn_kernel, c=c, n_last=n_steps),
        grid_spec=grid_spec,
        out_shape=[jax.ShapeDtypeStruct((t, w), F32), jax.ShapeDtypeStruct((t, w), F32),
                   jax.ShapeDtypeStruct((n_seq, 2, HGRN_HEADS, HGRN_DK, HGRN_DK), F32)],
        compiler_params=_params("arbitrary"),
        name="hgrn_scan",
    )(jnp.asarray(rbf, jnp.int32), jnp.asarray(rbb, jnp.int32), jnp.asarray(first, jnp.int32),
      jnp.asarray(seq, jnp.int32), z2d, z2d, z2d, z2d, z2d, z2d, lb2, s0_all)


def _qkv_prep_kernel(aq_ref, ak_ref, av_ref, qg_ref, kg_ref, cos_ref, sin_ref, swap_ref, *out_refs,
                     rope, q_scale, keep_k):
    q_out, k_out, v_out = out_refs[:3]
    hd = HEAD_DIM

    def norm_rope(x, g):
        ms = jnp.mean(x * x, axis=-1, keepdims=True)
        y = x * lax.rsqrt(ms + EPS) * g
        if rope:
            y = y * cos_ref[...] + _dot(y.astype(BF16), swap_ref[...]) * sin_ref[...]
        return y

    for j in range(ATTN_WIDTH // 128):
        pair = aq_ref[:, j * 128:(j + 1) * 128]
        for i in range(2):
            y = norm_rope(pair[:, i * hd:(i + 1) * hd], qg_ref[...])
            q_out[0, 2 * j + i] = (y * q_scale).astype(BF16)
    tl = ak_ref.shape[0]
    ones_col = jnp.where(lax.broadcasted_iota(jnp.int32, (tl, hd), 1) == 0, 1.0, 0.0).astype(BF16)
    for j in range(KV_WIDTH // 128):
        kpair = ak_ref[:, j * 128:(j + 1) * 128]
        vpair = av_ref[:, j * 128:(j + 1) * 128]
        for i in range(2):
            h = 2 * j + i
            y = norm_rope(kpair[:, i * hd:(i + 1) * hd], kg_ref[...])
            k_out[0, h] = y.astype(BF16)
            if keep_k:
                out_refs[3][:, h * hd:(h + 1) * hd] = y
            v_out[0, h, :, :hd] = vpair[:, i * hd:(i + 1) * hd].astype(BF16)
            v_out[0, h, :, hd:] = ones_col


def _qkv_prep(z2d, row0, n_seq, length, q_gain, k_gain, cos, sin, rope, keep_k):
    hd = HEAD_DIM
    tl = _tile(length, 512)
    nt = length // tl
    blk0 = row0 // tl
    swap = np.zeros((hd, hd), np.float32)
    for i in range(hd):
        swap[(i + hd // 2) % hd, i] = 1.0

    def zspec(width, col):
        return pl.BlockSpec((tl, width), lambda b, j: (blk0 + b * nt + j, col // width))

    def full(shape):
        return pl.BlockSpec(shape, lambda b, j: (0,) * len(shape))

    out_specs = [pl.BlockSpec((1, ATTN_HEADS, tl, hd), lambda b, j: (b, 0, j, 0)),
                 pl.BlockSpec((1, KV_HEADS, tl, hd), lambda b, j: (b, 0, j, 0)),
                 pl.BlockSpec((1, KV_HEADS, tl, 2 * hd), lambda b, j: (b, 0, j, 0))]
    out_shape = [jax.ShapeDtypeStruct((n_seq, ATTN_HEADS, length, hd), BF16),
                 jax.ShapeDtypeStruct((n_seq, KV_HEADS, length, hd), BF16),
                 jax.ShapeDtypeStruct((n_seq, KV_HEADS, length, 2 * hd), BF16)]
    if keep_k:
        out_specs.append(pl.BlockSpec((tl, KV_WIDTH), lambda b, j: (b * nt + j, 0)))
        out_shape.append(jax.ShapeDtypeStruct((n_seq * length, KV_WIDTH), F32))
    return pl.pallas_call(
        functools.partial(_qkv_prep_kernel, rope=rope, q_scale=ATTN_SCALE * LOG2_E, keep_k=keep_k),
        grid=(n_seq, nt),
        in_specs=[zspec(ATTN_WIDTH, COL_AQ), zspec(KV_WIDTH, COL_AK), zspec(KV_WIDTH, COL_AV),
                  full((1, hd)), full((1, hd)),
                  pl.BlockSpec((tl, hd), lambda b, j: (j, 0)), pl.BlockSpec((tl, hd), lambda b, j: (j, 0)),
                  full((hd, hd))],
        out_specs=out_specs,
        out_shape=out_shape,
        compiler_params=_params("parallel", "parallel"),
        name="qkv_prep",
    )(z2d, z2d, z2d, q_gain.reshape(1, hd), k_gain.reshape(1, hd), cos, sin, jnp.asarray(swap, BF16))


def _attn_kernel(q_ref, k_ref, v_ref, *rest, ts, n_s, tq):
    o_ref = rest[-1]
    rows = GROUP * tq
    q = q_ref[0].reshape(rows, HEAD_DIM)

    def body(i, carry):
        m, acc = carry
        sl = pl.ds(pl.multiple_of(i * ts, ts), ts)
        s = _dot_nt(q, k_ref[0, 0, sl, :])
        m_new = jnp.maximum(m, jnp.max(s, axis=-1, keepdims=True))
        p = jnp.exp2(s - m_new)
        acc = jnp.exp2(m - m_new) * acc + _dot(p.astype(BF16), v_ref[0, 0, sl, :])
        return m_new, acc

    m0 = jnp.full((rows, 1), NEG_INF, F32)
    acc0 = jnp.zeros((rows, 2 * HEAD_DIM), F32)
    m, acc = lax.fori_loop(0, n_s, body, (m0, acc0), unroll=True)
    o = acc[:, :HEAD_DIM] / acc[:, HEAD_DIM:HEAD_DIM + 1]
    for g in range(GROUP):
        o_ref[:, g * HEAD_DIM:(g + 1) * HEAD_DIM] = o[g * tq:(g + 1) * tq]


def _attention(q, k, v_aug, o_prev, row0, t_all):
    b, _, length, hd = q.shape
    s_len = k.shape[2]
    tq = _tile(length, ATTN_TQ)
    ts = _tile(s_len, ATTN_TS)
    nq = length // tq
    blk0 = row0 // tq
    in_specs = [
        pl.BlockSpec((1, GROUP, tq, hd), lambda bi, hi, qi: (bi, hi, qi, 0)),
        pl.BlockSpec((1, 1, s_len, hd), lambda bi, hi, qi: (bi, hi, 0, 0)),
        pl.BlockSpec((1, 1, s_len, 2 * hd), lambda bi, hi, qi: (bi, hi, 0, 0)),
    ]
    args = [q, k, v_aug]
    aliases = {}
    if o_prev is not None:
        in_specs.append(pl.BlockSpec(memory_space=pl.ANY))
        args.append(o_prev)
        aliases = {3: 0}
    return pl.pallas_call(
        functools.partial(_attn_kernel, ts=ts, n_s=s_len // ts, tq=tq),
        grid=(b, KV_HEADS, nq),
        in_specs=in_specs,
        out_specs=pl.BlockSpec((tq, GROUP * hd), lambda bi, hi, qi: (blk0 + bi * nq + qi, hi)),
        out_shape=jax.ShapeDtypeStruct((t_all, ATTN_WIDTH), F32),
        input_output_aliases=aliases,
        compiler_params=_params("parallel", "parallel", "parallel"),
        name="attention",
    )(*args)


def _layer_norm(t, g, b):
    mu = jnp.mean(t, axis=-1, keepdims=True)
    tc = t - mu
    var = jnp.mean(tc * tc, axis=-1, keepdims=True)
    return tc * lax.rsqrt(var + EPS) * g + b


def _merge_kernel(of_ref, ob_ref, hg_ref, ga_ref, gb_ref, oatt_ref, x_ref, g1_ref, sh2_ref, sc2_ref,
                  wrec_ref, watt_ref, wout_ref, ng_ref, lng_ref, lnb_ref, x1_ref, h2_ref, orec_scr):
    for h in range(HGRN_HEADS):
        cs = slice(h * HGRN_DK, (h + 1) * HGRN_DK)
        o = of_ref[:, cs] + ob_ref[:, cs]
        ms = jnp.mean(o * o, axis=-1, keepdims=True)
        zg = hg_ref[:, cs]
        orec_scr[:, cs] = (o * lax.rsqrt(ms + EPS) * ng_ref[...] * _silu(zg)).astype(BF16)
    y = (_sigmoid_t(ga_ref[...]) * _dot(orec_scr[...], wrec_ref[...])
         + _sigmoid_t(gb_ref[...]) * _dot(oatt_ref[...].astype(BF16), watt_ref[...]))
    u = _dot(y.astype(BF16), wout_ref[...])
    x1 = _layer_norm(DEEPNORM_ALPHA * x_ref[...] + g1_ref[0] * u, lng_ref[...], lnb_ref[...])
    x1_ref[...] = x1
    h2_ref[...] = x1 * (1.0 + sc2_ref[0]) + sh2_ref[0]


def _merge(o_f, o_b, z2d, o_att, x2d, mod_l, w_rec, w_att, w_out, norm_g, ln_g, ln_b, rows_of_tile, tm):
    t, d = x2d.shape
    mod3 = mod_l.reshape(8, 1, 6 * d)

    def rows(colblk):
        return pl.BlockSpec((tm, d), lambda i: (i, colblk))

    def modspec(chunk):
        return pl.BlockSpec((1, 1, d), lambda i: (rows_of_tile(i), 0, chunk))

    def full(shape):
        return pl.BlockSpec(shape, lambda i: (0,) * len(shape))

    def weight():
        return pl.BlockSpec((d, d), lambda i: (0, 0), pipeline_mode=pl.Buffered(1))

    return pl.pallas_call(
        _merge_kernel,
        grid=(t // tm,),
        in_specs=[rows(0), rows(0), rows(4), rows(COL_GA // d), rows(COL_GB // d), rows(0), rows(0),
                  modspec(2), modspec(3), modspec(4),
                  weight(), weight(), weight(), full((1, HGRN_DK)), full((1, d)), full((1, d))],
        out_specs=[rows(0), rows(0)],
        out_shape=[jax.ShapeDtypeStruct((t, d), F32), jax.ShapeDtypeStruct((t, d), F32)],
        scratch_shapes=[pltpu.VMEM((tm, d), BF16)],
        compiler_params=_params("parallel"),
        name="merge",
    )(o_f, o_b, z2d, z2d, z2d, o_att, x2d, mod3, mod3, mod3, w_rec, w_att, w_out,
      norm_g.reshape(1, HGRN_DK), ln_g.reshape(1, d), ln_b.reshape(1, d))


def _top16(s, n):
    iota = lax.broadcasted_iota(jnp.int32, s.shape, 0).astype(F32)
    vals, idxs = [], []
    for _ in range(PEER_TOPK):
        m = jnp.max(s, axis=0, keepdims=True)
        idx = jnp.min(jnp.where(s == m, iota, float(n)), axis=0, keepdims=True)
        vals.append(m)
        idxs.append(idx)
        s = jnp.where(iota == idx, NEG_INF, s)
    return vals, idxs


def _candidate_positions():
    rows = []
    for a in range(4):
        for b in range(16 if a == 0 else 8):
            rows.append((a, b, (a + 1) * (b + 1) <= PEER_TOPK))
    for b, n_a in ((0, 16), (1, 8), (2, 8)):
        for a in range(n_a):
            rows.append((a, b, a >= 4 and (a + 1) * (b + 1) <= PEER_TOPK))
    return np.array([[a * PEER_TOPK + b if ok else -1.0] for a, b, ok in rows], np.float32)


def _peer_head_topk(qh, keys0, keys1, posb):
    k = PEER_TOPK
    tops = []
    for p, keys in enumerate((keys0, keys1)):
        st = _dot_nt(keys, qh[:, p * PEER_HALF:(p + 1) * PEER_HALF].astype(BF16))
        tops.append(_top16(st, N_KEYS))
    (v0, i0), (v1, i1) = tops
    v0c = jnp.concatenate(v0, axis=0)
    v1c = jnp.concatenate(v1, axis=0)
    cand = jnp.concatenate([v0[0] + v1c, v0[1] + v1c[:8], v0[2] + v1c[:8], v0[3] + v1c[:8],
                            v0c + v1[0], v0c[:8] + v1[1], v0c[:8] + v1[2]], axis=0)
    cand = jnp.where(posb >= 0.0, cand, NEG_INF)
    best, poss = [], []
    for _ in range(k):
        m = jnp.max(cand, axis=0, keepdims=True)
        pos = jnp.min(jnp.where(cand == m, posb, float(k * k)), axis=0, keepdims=True)
        best.append(m)
        poss.append(pos)
        cand = jnp.where(posb == pos, NEG_INF, cand)
    best = jnp.concatenate(best, axis=0)
    posc = jnp.concatenate(poss, axis=0)
    a_rank = jnp.floor(posc * (1.0 / k))
    b_rank = posc - k * a_rank
    ea = jnp.zeros_like(posc)
    eb = jnp.zeros_like(posc)
    for r in range(k):
        ea = jnp.where(a_rank == float(r), i0[r], ea)
        eb = jnp.where(b_rank == float(r), i1[r], eb)
    e = jnp.exp(best - best[0:1, :])
    return ea, eb, e / jnp.sum(e, axis=0, keepdims=True)


RSQRT2 = 1.0 / math.sqrt(2.0)


def _peer_kernel(h2n_ref, wq_ref, keys_ref, pos_ref, u_ref, v_ref, x1_ref, g2_ref, lng_ref, lnb_ref,
                 x2_ref, ids_scr, idt_scr, w_scr, p_scr, acc_scr, h2n_scr, h2_scr, *, tb, te, pitch):
    i = pl.program_id(0)
    j = pl.program_id(1)
    slot = i % 2
    k = PEER_TOPK
    rows_per_tile = te // N_KEYS
    half = N_KEYS // 2
    hi_mask = jnp.uint32(0xFFFF0000)

    @pl.when(j == 0)
    def _():
        @pl.when(i == 0)
        def _():
            ids_scr[...] = jnp.zeros_like(ids_scr)
            h2n_scr[...] = jnp.zeros_like(h2n_scr)

        h2_scr[...] = h2n_scr[...]
        h2n_scr[...] = h2n_ref[...].astype(BF16)
        for n in range(3):
            idt_scr[n] = ids_scr[1 - slot, n].T
        acc_scr[...] = jnp.zeros_like(acc_scr)
        iota = lax.broadcasted_iota(jnp.int32, (N_KEYS, PEER_HEADS * k), 0).astype(F32)

        def tok(t, carry):
            a_row = idt_scr[0, pl.ds(t, 1), :]
            b_row = idt_scr[1, pl.ds(t, 1), :]
            g_row = idt_scr[2, pl.ds(t, 1), :] * RSQRT2
            at = jnp.where(iota == a_row, 1.0, 0.0).astype(BF16)
            gbt = jnp.where(iota == b_row, g_row, 0.0).astype(BF16)
            w = pltpu.bitcast(_dot_nt(at, gbt), jnp.uint32)
            w_scr[pl.ds(t, half, stride=pitch), :] = (w[:half] & hi_mask) | (w[half:] >> 16)
            return carry

        lax.fori_loop(0, tb, tok, 0, unroll=32)

    posb = jnp.broadcast_to(pos_ref[...], (pos_ref.shape[0], tb))
    ea, eb, gate = _peer_head_topk(_dot(h2n_scr[...], wq_ref[...]), keys_ref[0, 0], keys_ref[0, 1], posb)
    rows = pl.ds(pl.multiple_of(j * k, k), k)
    ids_scr[slot, 0, rows, :] = ea
    ids_scr[slot, 1, rows, :] = eb
    ids_scr[slot, 2, rows, :] = gate

    a = _dot_nt(h2_scr[...], u_ref[...])
    for r in range(rows_per_tile):
        i1 = j * rows_per_tile + r
        packed = w_scr[pl.ds(pl.multiple_of((i1 % half) * pitch, 8), tb), :]
        shift = jnp.where(i1 >= half, 16, 0).astype(jnp.uint32)
        w = pltpu.bitcast((packed << shift) & hi_mask, F32)
        cs = slice(r * N_KEYS, (r + 1) * N_KEYS)
        ar = a[:, cs]
        p_scr[:, cs] = (w * (ar * (1.0 + lax.erf(ar)))).astype(BF16)
    acc_scr[...] += _dot(p_scr[...], v_ref[...])

    @pl.when(j == pl.num_programs(1) - 1)
    def _():
        x2_ref[...] = _layer_norm(DEEPNORM_ALPHA * x1_ref[...] + g2_ref[0] * acc_scr[...],
                                  lng_ref[...], lnb_ref[...])


def _peer(h2, wq_bf, keys_bf, u_bf, v_bf, x1, mod_l, ln_g, ln_b, rows_of_tile, tb):
    t, d = x1.shape
    n_e = v_bf.shape[0]
    te = PEER_TE
    assert n_e // te == PEER_HEADS
    hk = PEER_HEADS * PEER_TOPK
    pitch = tb + 8
    nblk = t // tb
    mod3 = mod_l.reshape(8, 1, 6 * d)
    pos = _candidate_positions()

    def cur(i):
        return jnp.maximum(i - 1, 0)

    def full(shape):
        return pl.BlockSpec(shape, lambda i, j: (0,) * len(shape))

    return pl.pallas_call(
        functools.partial(_peer_kernel, tb=tb, te=te, pitch=pitch),
        grid=(nblk + 1, PEER_HEADS),
        in_specs=[
            pl.BlockSpec((tb, d), lambda i, j: (jnp.minimum(i, nblk - 1), 0)),
            pl.BlockSpec((d, 2 * PEER_HALF), lambda i, j: (0, j)),
            pl.BlockSpec((1, 2, N_KEYS, PEER_HALF), lambda i, j: (j, 0, 0, 0)),
            full(pos.shape),
            pl.BlockSpec((te, d), lambda i, j: (j, 0)),
            pl.BlockSpec((te, d), lambda i, j: (j, 0)),
            pl.BlockSpec((tb, d), lambda i, j: (cur(i), 0), pipeline_mode=pl.Buffered(1)),
            pl.BlockSpec((1, 1, d), lambda i, j: (rows_of_tile(cur(i)), 0, 5)),
            full((1, d)), full((1, d)),
        ],
        out_specs=pl.BlockSpec((tb, d), lambda i, j: (cur(i), 0)),
        out_shape=jax.ShapeDtypeStruct((t, d), F32),
        scratch_shapes=[pltpu.VMEM((2, 3, hk, tb), F32), pltpu.VMEM((3, tb, hk), F32),
                        pltpu.VMEM((pitch * N_KEYS // 2, N_KEYS), jnp.uint32), pltpu.VMEM((tb, te), BF16),
                        pltpu.VMEM((tb, d), F32), pltpu.VMEM((tb, d), BF16), pltpu.VMEM((tb, d), BF16)],
        compiler_params=pltpu.CompilerParams(dimension_semantics=("arbitrary", "arbitrary"),
                                             vmem_limit_bytes=PEER_VMEM_LIMIT),
        name="peer",
    )(h2, wq_bf, keys_bf, jnp.asarray(pos), u_bf, v_bf, x1, mod3, ln_g.reshape(1, d), ln_b.reshape(1, d))


def _rope_tables(length):
    rows = length // GRID_W
    row = jnp.broadcast_to(jnp.arange(rows, dtype=F32)[:, None], (rows, GRID_W)).reshape(length)
    col = jnp.broadcast_to(jnp.arange(GRID_W, dtype=F32)[None, :], (rows, GRID_W)).reshape(length)
    n_freq = HEAD_DIM // 4
    inv = jnp.power(ROPE_THETA, -jnp.arange(n_freq, dtype=F32) / n_freq)
    ang = jnp.concatenate([row[:, None] * inv, col[:, None] * inv], axis=-1)
    cos, sin = jnp.cos(ang), jnp.sin(ang)
    return jnp.concatenate([cos, cos], axis=-1), jnp.concatenate([-sin, sin], axis=-1)


def kernel(x_prompt, x_sample, cache_attn_k, cache_attn_v, state_hgrn, c, c_ctx, w_mod, b_mod, w_in,
           hgrn_lb_logits, hgrn_norm_g, q_norm_g, k_norm_g, w_branch_rec, w_branch_att, w_out,
           ln1_g, ln1_b, ln2_g, ln2_b, peer_w_query, peer_sub_keys, peer_u, peer_v):
    nb, seq, d = x_prompt.shape
    db, dseq, _ = x_sample.shape
    depth = w_in.shape[0]
    past = cache_attn_k.shape[2]
    t_ctx, t_lat = nb * seq, db * dseq
    t_all = t_ctx + t_lat
    assert d == D_MODEL and 1 + db <= 8

    tm = _tile(math.gcd(t_ctx, dseq), 2048)
    tm_merge = _tile(math.gcd(t_ctx, dseq), 512)
    tb_mix = _tile(math.gcd(t_ctx, dseq), PEER_TB_MIX)

    def rows_of_tile_fn(tile):
        n_ctx = t_ctx // tile
        per = dseq // tile
        return lambda i: jnp.where(i < n_ctx, 0, 1 + (i - n_ctx) // per)

    cond8 = jnp.concatenate([c_ctx[None, :], c, jnp.zeros((8 - 1 - db, d), F32)], axis=0)
    mod = _modulation(cond8, w_mod, b_mod)

    lb = jnp.cumsum(jax.nn.softmax(hgrn_lb_logits.astype(F32), axis=1), axis=1)
    lb = lb - lb[:, :1]

    segs = jnp.split(w_in, np.cumsum([HGRN_WIDTH] * 5 + [ATTN_WIDTH, KV_WIDTH, KV_WIDTH, D_MODEL]).tolist(), axis=-1)
    hq_w, hff_w, hfb_w, hi_w, hg_w, aq_w, ak_w, av_w, ga_w, gb_w = segs
    w_in_bf = jnp.concatenate([hq_w, hff_w, hfb_w, hi_w, hg_w, aq_w, ga_w, gb_w, ak_w, av_w], axis=-1).astype(BF16)
    w_rec_bf, w_att_bf, w_out_bf = (w.astype(BF16) for w in (w_branch_rec, w_branch_att, w_out))
    wq_bf = peer_w_query.astype(BF16)
    keys_bf = peer_sub_keys.astype(BF16)
    u_bf = (peer_u * RSQRT2).astype(BF16)
    v_bf = peer_v.astype(BF16)

    cos_t, sin_t = _rope_tables(dseq)
    ones_t, zeros_t = jnp.ones((seq, HEAD_DIM), F32), jnp.zeros((seq, HEAD_DIM), F32)
    cache_k_bf = jnp.transpose(cache_attn_k, (0, 1, 3, 2, 4)).astype(BF16)
    cache_v_t = jnp.transpose(cache_attn_v, (0, 1, 3, 2, 4))
    ones_pad = jnp.zeros(cache_v_t.shape, F32).at[..., 0].set(1.0)
    cache_v_aug = jnp.concatenate([cache_v_t, ones_pad], axis=-1).astype(BF16)

    seq_rows = [(i * seq, seq) for i in range(nb)] + [(t_ctx + i * dseq, dseq) for i in range(db)]
    zero_state = jnp.zeros((nb, 2, HGRN_HEADS, HGRN_DK, HGRN_DK), F32)

    x2d = jnp.concatenate([x_prompt.reshape(t_ctx, d), x_sample.reshape(t_lat, d)], axis=0)
    new_k, new_v, new_s = [], [], []
    for l in range(depth):
        z2d = _in_proj(x2d, mod[l], w_in_bf[l], rows_of_tile_fn(tm), tm)

        s0_all = jnp.concatenate([zero_state, state_hgrn[:, l]], axis=0)
        o_f, o_b, s_fin = _hgrn_scan(z2d, lb[:, l], s0_all, seq_rows, HGRN_CHUNK)
        new_s.append(s_fin[:nb])

        q_c, k_c, v_c, k_norm = _qkv_prep(z2d, 0, nb, seq, q_norm_g[l], k_norm_g[l], ones_t, zeros_t, False, True)
        o_att = _attention(q_c, k_c, v_c, None, 0, t_all)
        new_k.append(k_norm.reshape(nb, seq, KV_HEADS, HEAD_DIM))
        new_v.append(z2d[:t_ctx, COL_AV:].reshape(nb, seq, KV_HEADS, HEAD_DIM))

        q_s, k_s, v_s = _qkv_prep(z2d, t_ctx, db, dseq, q_norm_g[l], k_norm_g[l], cos_t, sin_t, True, False)
        k_s = jnp.concatenate([k_s, cache_k_bf[:, l]], axis=2)
        v_s = jnp.concatenate([v_s, cache_v_aug[:, l]], axis=2)
        o_att = _attention(q_s, k_s, v_s, o_att, t_ctx, t_all)

        x1, h2 = _merge(o_f, o_b, z2d, o_att, x2d, mod[l], w_rec_bf[l], w_att_bf[l], w_out_bf[l],
                        hgrn_norm_g[l], ln1_g[l], ln1_b[l], rows_of_tile_fn(tm_merge), tm_merge)

        x2d = _peer(h2, wq_bf[l], keys_bf[l], u_bf[l], v_bf[l], x1, mod[l], ln2_g[l], ln2_b[l],
                    rows_of_tile_fn(tb_mix), tb_mix)

    y_prompt = x2d[:t_ctx].reshape(nb, seq, d)
    y_sample = x2d[t_ctx:].reshape(db, dseq, d)
    return (y_prompt, y_sample, jnp.stack(new_k, axis=1), jnp.stack(new_v, axis=1), jnp.stack(new_s, axis=1))
```

```python
import functools
import math

import numpy as np
import jax
import jax.numpy as jnp
from jax import lax
from jax.experimental import pallas as pl
from jax.experimental.pallas import tpu as pltpu

F32 = jnp.float32
BF16 = jnp.bfloat16

D_MODEL = 1024
GRID_W = 64
HGRN_HEADS = 8
HGRN_DK = 128
HGRN_WIDTH = HGRN_HEADS * HGRN_DK
HGRN_CHUNK = 128
ATTN_HEADS = 16
KV_HEADS = 4
HEAD_DIM = 64
GROUP = ATTN_HEADS // KV_HEADS
ATTN_WIDTH = ATTN_HEADS * HEAD_DIM
KV_WIDTH = KV_HEADS * HEAD_DIM
ATTN_SCALE = HEAD_DIM ** -0.5
ROPE_THETA = 10000.0
PEER_HEADS = 8
PEER_HALF = 128
N_KEYS = 128
N_EXPERTS = N_KEYS * N_KEYS
PEER_TOPK = 16
MODEL_DEPTH = 4
DEEPNORM_ALPHA = (2 * MODEL_DEPTH) ** 0.25
EPS = 1e-6
NEG_INF = float("-inf")

IN_COLS = 5 * HGRN_WIDTH + ATTN_WIDTH + 2 * D_MODEL + 2 * KV_WIDTH
COL_AQ = 5 * HGRN_WIDTH
COL_GA = COL_AQ + ATTN_WIDTH
COL_GB = COL_GA + D_MODEL
COL_AK = COL_GB + D_MODEL
COL_AV = COL_AK + KV_WIDTH

VMEM_LIMIT = 56 * 1024 * 1024
PEER_VMEM_LIMIT = 60 * 1024 * 1024
LOG2_E = math.log2(math.e)
ATTN_TQ = 512
ATTN_TS = 512
PEER_TB_MIX = 512
PEER_TE = 2048


def _params(*sem):
    return pltpu.CompilerParams(dimension_semantics=sem, vmem_limit_bytes=VMEM_LIMIT)


def _tile(n, pref):
    t = min(n, pref)
    while n % t or t % 8:
        t -= 1
    return t


def _sigmoid(x):
    return 1.0 / (1.0 + jnp.exp(-x))


def _sigmoid_t(x):
    return 0.5 + 0.5 * jnp.tanh(0.5 * x)


def _silu(x):
    h = 0.5 * x
    return h + h * jnp.tanh(h)


def _bf16_floor(x):
    bits = pltpu.bitcast(x, jnp.uint32) & jnp.uint32(0xFFFF0000)
    return pltpu.bitcast(bits, F32)


def _dot(a, b):
    return jnp.dot(a, b, preferred_element_type=F32)


def _dot_nt(a, b):
    return lax.dot_general(a, b, (((1,), (1,)), ((), ())), preferred_element_type=F32)


def _mod_kernel(cond_ref, w_ref, b_ref, o_ref):
    c = cond_ref[...]
    s = c * _sigmoid(c)
    o_ref[0] = jnp.dot(s, w_ref[0], precision=lax.Precision.HIGHEST, preferred_element_type=F32) + b_ref[0]


def _modulation(cond8, w_mod, b_mod):
    depth, d, n = w_mod.shape
    tn = _tile(n, 1536)
    return pl.pallas_call(
        _mod_kernel,
        grid=(depth, n // tn),
        in_specs=[
            pl.BlockSpec((8, d), lambda l, j: (0, 0)),
            pl.BlockSpec((1, d, tn), lambda l, j: (l, 0, j)),
            pl.BlockSpec((1, 1, tn), lambda l, j: (l, 0, j)),
        ],
        out_specs=pl.BlockSpec((1, 8, tn), lambda l, j: (l, 0, j)),
        out_shape=jax.ShapeDtypeStruct((depth, 8, n), F32),
        compiler_params=_params("parallel", "parallel"),
        name="modulation",
    )(cond8, w_mod, b_mod.reshape(depth, 1, n))


def _inproj_kernel(x_ref, sh_ref, sc_ref, w_ref, z_ref, h_scr):
    @pl.when(pl.program_id(1) == 0)
    def _():
        h_scr[...] = (x_ref[...] * (1.0 + sc_ref[0]) + sh_ref[0]).astype(BF16)

    z_ref[...] = _dot(h_scr[...], w_ref[...])


def _in_proj(x2d, mod_l, w_in_bf, rows_of_tile, tm):
    t, d = x2d.shape
    n = w_in_bf.shape[1]
    tn = 512
    mod3 = mod_l.reshape(8, 1, 6 * d)
    return pl.pallas_call(
        _inproj_kernel,
        grid=(t // tm, n // tn),
        in_specs=[
            pl.BlockSpec((tm, d), lambda i, j: (i, 0)),
            pl.BlockSpec((1, 1, d), lambda i, j: (rows_of_tile(i), 0, 0)),
            pl.BlockSpec((1, 1, d), lambda i, j: (rows_of_tile(i), 0, 1)),
            pl.BlockSpec((d, tn), lambda i, j: (0, j)),
        ],
        out_specs=pl.BlockSpec((tm, tn), lambda i, j: (i, j)),
        out_shape=jax.ShapeDtypeStruct((t, n), F32),
        scratch_shapes=[pltpu.VMEM((tm, d), BF16)],
        compiler_params=_params("parallel", "arbitrary"),
        name="in_proj",
    )(x2d, mod3, mod3, w_in_bf)


def _level_reference_rows(beta, c):
    width = beta.shape[1]
    sub = lax.broadcasted_iota(jnp.int32, (8, width), 0)

    def bc(i, n=8):
        return jnp.broadcast_to(beta[i:i + 1, :], (n, width))

    out = []
    m = 1
    while m < c:
        blocks = []
        if m >= 8:
            for blk in range(c // (2 * m)):
                blocks.append(bc(blk * 2 * m + m, 2 * m))
        elif m == 4:
            for v in range(c // 8):
                blocks.append(bc(8 * v + 4))
        elif m == 2:
            for v in range(c // 8):
                blocks.append(jnp.where(sub < 4, bc(8 * v + 2), bc(8 * v + 6)))
        else:
            for v in range(c // 8):
                lo = jnp.where(sub < 2, bc(8 * v + 1), bc(8 * v + 3))
                hi = jnp.where(sub < 6, bc(8 * v + 5), bc(8 * v + 7))
                blocks.append(jnp.where(sub < 4, lo, hi))
        out.append(blocks[0] if len(blocks) == 1 else jnp.concatenate(blocks, axis=0))
        m *= 2
    return out


def _hgrn_direction(zq, zf, v, lb, st_ref, o_ref, tri, eye, q_half, pair_masks, reverse, c):
    heads = [slice(h * HGRN_DK, (h + 1) * HGRN_DK) for h in range(HGRN_HEADS)]
    q = zq * pl.reciprocal(1.0 + jnp.exp2(zq * -LOG2_E), approx=True)
    zl = zf * -LOG2_E
    e = jnp.exp2(zl)
    l2 = jnp.log(1.0 + e) * LOG2_E
    g = jnp.log(1.0 + lb * e) * LOG2_E - l2
    k = (1.0 - lb) * jnp.exp2(zl - l2)
    g1 = _bf16_floor(g)
    r1 = g - g1
    g2 = _bf16_floor(r1)
    g3 = r1 - g2
    beta = (_dot(tri, g1.astype(BF16)) + _dot(tri, g2.astype(BF16))) + _dot(tri, g3.astype(BF16))
    last = 0 if reverse else c - 1
    beta_tot = beta[last:last + 1, :]

    q_bf, k_bf, v_bf = q.astype(BF16), k.astype(BF16), v.astype(BF16)
    a = [jnp.where(eye, _dot_nt(q_bf[:, hs], k_bf[:, hs]), 0.0) for hs in heads]
    for r, qh, pm in zip(_level_reference_rows(beta, c), q_half, pair_masks):
        neg_abs = pltpu.bitcast(pltpu.bitcast(beta - r, jnp.uint32) | jnp.uint32(0x80000000), F32)
        qk = jnp.concatenate([jnp.where(qh, q[:, hs], k[:, hs]) for hs in heads], axis=1)
        u = (qk * jnp.exp2(neg_abs)).astype(BF16)
        a = [ah + jnp.where(pm, _dot_nt(u[:, hs], u[:, hs]), 0.0) for ah, hs in zip(a, heads)]

    qb = (q * jnp.exp2(beta)).astype(BF16)
    kb = (k * jnp.exp2(beta_tot - beta)).astype(BF16)
    dec = jnp.exp2(beta_tot)
    for h, hs in enumerate(heads):
        st = st_ref[h]
        o_ref[:, hs] = _dot(a[h].astype(BF16), v_bf[:, hs]) + _dot_nt(qb[:, hs], st.astype(BF16))
        st_ref[h] = st * dec[:, hs] + lax.dot_general(
            v_bf[:, hs], kb[:, hs], (((0,), (0,)), ((), ())), preferred_element_type=F32)


def _hgrn_kernel(rbf_ref, rbb_ref, first_ref, seq_ref,
                 qf_ref, ff_ref, vf_ref, qb_ref, fb_ref, vb_ref, lb_ref, s0_ref,
                 of_ref, ob_ref, sout_ref, st_scr, *, c, n_last):
    s = pl.program_id(0)

    @pl.when(first_ref[s] == 1)
    def _():
        for d in range(2):
            for h in range(HGRN_HEADS):
                st_scr[d, h] = s0_ref[0, d, h].T

    row = lax.broadcasted_iota(jnp.int32, (c, c), 0)
    col = lax.broadcasted_iota(jnp.int32, (c, c), 1)
    rowd = lax.broadcasted_iota(jnp.int32, (c, HGRN_DK), 0)
    eye = row == col

    for d, (q_ref, f_ref, v_ref, o_ref) in enumerate(
            ((qf_ref, ff_ref, vf_ref, of_ref), (qb_ref, fb_ref, vb_ref, ob_ref))):
        reverse = d == 1
        tri = (col >= row if reverse else col <= row).astype(BF16)
        q_par = 0 if reverse else 1
        q_half, pair_masks = [], []
        m = 1
        while m < c:
            q_half.append(((rowd // m) % 2) == q_par)
            pair_masks.append(((row // (2 * m)) == (col // (2 * m)))
                              & (((row // m) % 2) == q_par) & (((col // m) % 2) == 1 - q_par))
            m *= 2
        _hgrn_direction(q_ref[...], f_ref[...], v_ref[...], lb_ref[pl.ds(d, 1), :], st_scr.at[d], o_ref,
                        tri, eye, q_half, pair_masks, reverse, c)

    @pl.when(first_ref[jnp.minimum(s + 1, n_last)] == 1)
    def _():
        for d in range(2):
            for h in range(HGRN_HEADS):
                sout_ref[0, d, h] = st_scr[d, h].T


def _hgrn_scan(z2d, lb2, s0_all, seq_rows, c):
    t = z2d.shape[0]
    rbf, rbb, first, seq = [], [], [], []
    for si, (r0, length) in enumerate(seq_rows):
        n = length // c
        for ci in range(n):
            rbf.append(r0 // c + ci)
            rbb.append(r0 // c + n - 1 - ci)
            first.append(1 if ci == 0 else 0)
            seq.append(si)
    n_steps = len(rbf)
    first.append(1)
    w = HGRN_WIDTH

    def zspec(which, colblk):
        if which == 0:
            return pl.BlockSpec((c, w), lambda s, rbf, rbb, fi, sq: (rbf[s], colblk))
        return pl.BlockSpec((c, w), lambda s, rbf, rbb, fi, sq: (rbb[s], colblk))

    n_seq = len(seq_rows)
    state_spec = pl.BlockSpec((1, 2, HGRN_HEADS, HGRN_DK, HGRN_DK),
                              lambda s, rbf, rbb, fi, sq: (sq[s], 0, 0, 0, 0))
    grid_spec = pltpu.PrefetchScalarGridSpec(
        num_scalar_prefetch=4,
        grid=(n_steps,),
        in_specs=[zspec(0, 0), zspec(0, 1), zspec(0, 3), zspec(1, 0), zspec(1, 2), zspec(1, 3),
                  pl.BlockSpec((2, w), lambda s, *_: (0, 0)), state_spec],
        out_specs=[zspec(0, 0), zspec(1, 0), state_spec],
        scratch_shapes=[pltpu.VMEM((2, HGRN_HEADS, HGRN_DK, HGRN_DK), F32)],
    )
    return pl.pallas_call(
        functools.partial(_hgrn_kernel, c=c, n_last=n_steps),
        grid_spec=grid_spec,
        out_shape=[jax.ShapeDtypeStruct((t, w), F32), jax.ShapeDtypeStruct((t, w), F32),
                   jax.ShapeDtypeStruct((n_seq, 2, HGRN_HEADS, HGRN_DK, HGRN_DK), F32)],
        compiler_params=_params("arbitrary"),
        name="hgrn_scan",
    )(jnp.asarray(rbf, jnp.int32), jnp.asarray(rbb, jnp.int32), jnp.asarray(first, jnp.int32),
      jnp.asarray(seq, jnp.int32), z2d, z2d, z2d, z2d, z2d, z2d, lb2, s0_all)


def _qkv_prep_kernel(aq_ref, ak_ref, av_ref, qg_ref, kg_ref, cos_ref, sin_ref, swap_ref, avg_ref, *out_refs,
                     rope, q_scale, keep_k):
    q_out, k_out, v_out = out_refs[:3]
    hd = HEAD_DIM
    tl = aq_ref.shape[0]
    lane = lax.broadcasted_iota(jnp.int32, (tl, 2 * hd), 1)
    low = lane < hd
    ones_col = jnp.where(lane == hd, 1.0, 0.0)

    def norm_rope(pair, g):
        sq = pair * pair
        hi = _bf16_floor(sq)
        ms = _dot(hi.astype(BF16), avg_ref[...]) + _dot((sq - hi).astype(BF16), avg_ref[...])
        y = pair * lax.rsqrt(ms + EPS) * g
        if rope:
            y = y * cos_ref[...] + _dot(y.astype(BF16), swap_ref[...]) * sin_ref[...]
        return y

    for j in range(ATTN_WIDTH // (2 * hd)):
        y = norm_rope(aq_ref[:, j * 2 * hd:(j + 1) * 2 * hd], qg_ref[...]) * q_scale
        q_out[0, 2 * j] = jnp.where(low, y, 0.0).astype(BF16)
        q_out[0, 2 * j + 1] = jnp.where(low, 0.0, y).astype(BF16)
    for j in range(KV_WIDTH // (2 * hd)):
        cols = slice(j * 2 * hd, (j + 1) * 2 * hd)
        y = norm_rope(ak_ref[:, cols], kg_ref[...])
        y_other = pltpu.roll(y, hd, axis=1)
        k_out[0, 2 * j] = jnp.where(low, y, y_other).astype(BF16)
        k_out[0, 2 * j + 1] = jnp.where(low, y_other, y).astype(BF16)
        if keep_k:
            out_refs[3][:, cols] = y
        vp = av_ref[:, cols]
        v_out[0, 2 * j] = jnp.where(low, vp, ones_col).astype(BF16)
        v_out[0, 2 * j + 1] = jnp.where(low, pltpu.roll(vp, hd, axis=1), ones_col).astype(BF16)


def _qkv_prep(z2d, row0, n_seq, length, q_gain, k_gain, cos, sin, rope, keep_k):
    hd = HEAD_DIM
    tl = _tile(length, 512)
    nt = length // tl
    blk0 = row0 // tl
    swap = np.zeros((2 * hd, 2 * hd), np.float32)
    avg = np.zeros((2 * hd, 2 * hd), np.float32)
    for h0 in (0, hd):
        avg[h0:h0 + hd, h0:h0 + hd] = 1.0 / hd
        for i in range(hd):
            swap[h0 + (i + hd // 2) % hd, h0 + i] = 1.0

    def zspec(width, col):
        return pl.BlockSpec((tl, width), lambda b, j: (blk0 + b * nt + j, col // width))

    def full(shape):
        return pl.BlockSpec(shape, lambda b, j: (0,) * len(shape))

    def head_major(n_heads):
        return (pl.BlockSpec((1, n_heads, tl, 2 * hd), lambda b, j: (b, 0, j, 0)),
                jax.ShapeDtypeStruct((n_seq, n_heads, length, 2 * hd), BF16))

    out_specs, out_shape = map(list, zip(head_major(ATTN_HEADS), head_major(KV_HEADS), head_major(KV_HEADS)))
    if keep_k:
        out_specs.append(pl.BlockSpec((tl, KV_WIDTH), lambda b, j: (b * nt + j, 0)))
        out_shape.append(jax.ShapeDtypeStruct((n_seq * length, KV_WIDTH), F32))

    def both_halves(g):
        return jnp.concatenate([g, g]).reshape(1, 2 * hd)

    return pl.pallas_call(
        functools.partial(_qkv_prep_kernel, rope=rope, q_scale=ATTN_SCALE * LOG2_E, keep_k=keep_k),
        grid=(n_seq, nt),
        in_specs=[zspec(ATTN_WIDTH, COL_AQ), zspec(KV_WIDTH, COL_AK), zspec(KV_WIDTH, COL_AV),
                  full((1, 2 * hd)), full((1, 2 * hd)),
                  pl.BlockSpec((tl, 2 * hd), lambda b, j: (j, 0)), pl.BlockSpec((tl, 2 * hd), lambda b, j: (j, 0)),
                  full((2 * hd, 2 * hd)), full((2 * hd, 2 * hd))],
        out_specs=out_specs,
        out_shape=out_shape,
        compiler_params=_params("parallel", "parallel"),
        name="qkv_prep",
    )(z2d, z2d, z2d, both_halves(q_gain), both_halves(k_gain), cos, sin,
      jnp.asarray(swap, BF16), jnp.asarray(avg, BF16))


def _attn_kernel(q_ref, k_ref, v_ref, *rest, ts, n_s, tq):
    o_ref = rest[-1]
    rows = GROUP * tq
    q = q_ref[0].reshape(rows, 2 * HEAD_DIM)

    def body(i, carry):
        m, acc = carry
        sl = pl.ds(pl.multiple_of(i * ts, ts), ts)
        s = _dot_nt(q, k_ref[0, 0, sl, :])
        m_new = jnp.maximum(m, jnp.max(s, axis=-1, keepdims=True))
        p = jnp.exp2(s - m_new)
        acc = jnp.exp2(m - m_new) * acc + _dot(p.astype(BF16), v_ref[0, 0, sl, :])
        return m_new, acc

    m0 = jnp.full((rows, 1), NEG_INF, F32)
    acc0 = jnp.zeros((rows, 2 * HEAD_DIM), F32)
    m, acc = lax.fori_loop(0, n_s, body, (m0, acc0), unroll=True)
    o = acc[:, :HEAD_DIM] / acc[:, HEAD_DIM:HEAD_DIM + 1]
    for g in range(GROUP):
        o_ref[:, g * HEAD_DIM:(g + 1) * HEAD_DIM] = o[g * tq:(g + 1) * tq]


def _attention(q, k, v_aug, o_prev, row0, t_all):
    b, _, length, _ = q.shape
    hd = HEAD_DIM
    s_len = k.shape[2]
    tq = _tile(length, ATTN_TQ)
    ts = _tile(s_len, ATTN_TS)
    nq = length // tq
    blk0 = row0 // tq
    in_specs = [
        pl.BlockSpec((1, GROUP, tq, 2 * hd), lambda bi, hi, qi: (bi, hi, qi, 0)),
        pl.BlockSpec((1, 1, s_len, 2 * hd), lambda bi, hi, qi: (bi, hi, 0, 0)),
        pl.BlockSpec((1, 1, s_len, 2 * hd), lambda bi, hi, qi: (bi, hi, 0, 0)),
    ]
    args = [q, k, v_aug]
    aliases = {}
    if o_prev is not None:
        in_specs.append(pl.BlockSpec(memory_space=pl.ANY))
        args.append(o_prev)
        aliases = {3: 0}
    return pl.pallas_call(
        functools.partial(_attn_kernel, ts=ts, n_s=s_len // ts, tq=tq),
        grid=(b, KV_HEADS, nq),
        in_specs=in_specs,
        out_specs=pl.BlockSpec((tq, GROUP * hd), lambda bi, hi, qi: (blk0 + bi * nq + qi, hi)),
        out_shape=jax.ShapeDtypeStruct((t_all, ATTN_WIDTH), F32),
        input_output_aliases=aliases,
        compiler_params=_params("parallel", "parallel", "parallel"),
        name="attention",
    )(*args)


def _layer_norm(t, g, b):
    mu = jnp.mean(t, axis=-1, keepdims=True)
    tc = t - mu
    var = jnp.mean(tc * tc, axis=-1, keepdims=True)
    return tc * lax.rsqrt(var + EPS) * g + b


def _merge_kernel(of_ref, ob_ref, hg_ref, ga_ref, gb_ref, oatt_ref, x_ref, g1_ref, sh2_ref, sc2_ref,
                  wrec_ref, watt_ref, wout_ref, ng_ref, lng_ref, lnb_ref, x1_ref, h2_ref, orec_scr):
    for h in range(HGRN_HEADS):
        cs = slice(h * HGRN_DK, (h + 1) * HGRN_DK)
        o = of_ref[:, cs] + ob_ref[:, cs]
        ms = jnp.mean(o * o, axis=-1, keepdims=True)
        zg = hg_ref[:, cs]
        orec_scr[:, cs] = (o * lax.rsqrt(ms + EPS) * ng_ref[...] * _silu(zg)).astype(BF16)
    y = (_sigmoid_t(ga_ref[...]) * _dot(orec_scr[...], wrec_ref[...])
         + _sigmoid_t(gb_ref[...]) * _dot(oatt_ref[...].astype(BF16), watt_ref[...]))
    u = _dot(y.astype(BF16), wout_ref[...])
    x1 = _layer_norm(DEEPNORM_ALPHA * x_ref[...] + g1_ref[0] * u, lng_ref[...], lnb_ref[...])
    x1_ref[...] = x1
    h2_ref[...] = x1 * (1.0 + sc2_ref[0]) + sh2_ref[0]


def _merge(o_f, o_b, z2d, o_att, x2d, mod_l, w_rec, w_att, w_out, norm_g, ln_g, ln_b, rows_of_tile, tm):
    t, d = x2d.shape
    mod3 = mod_l.reshape(8, 1, 6 * d)

    def rows(colblk):
        return pl.BlockSpec((tm, d), lambda i: (i, colblk))

    def modspec(chunk):
        return pl.BlockSpec((1, 1, d), lambda i: (rows_of_tile(i), 0, chunk))

    def full(shape):
        return pl.BlockSpec(shape, lambda i: (0,) * len(shape))

    def weight():
        return pl.BlockSpec((d, d), lambda i: (0, 0), pipeline_mode=pl.Buffered(1))

    return pl.pallas_call(
        _merge_kernel,
        grid=(t // tm,),
        in_specs=[rows(0), rows(0), rows(4), rows(COL_GA // d), rows(COL_GB // d), rows(0), rows(0),
                  modspec(2), modspec(3), modspec(4),
                  weight(), weight(), weight(), full((1, HGRN_DK)), full((1, d)), full((1, d))],
        out_specs=[rows(0), rows(0)],
        out_shape=[jax.ShapeDtypeStruct((t, d), F32), jax.ShapeDtypeStruct((t, d), F32)],
        scratch_shapes=[pltpu.VMEM((tm, d), BF16)],
        compiler_params=_params("parallel"),
        name="merge",
    )(o_f, o_b, z2d, z2d, z2d, o_att, x2d, mod3, mod3, mod3, w_rec, w_att, w_out,
      norm_g.reshape(1, HGRN_DK), ln_g.reshape(1, d), ln_b.reshape(1, d))


def _top16(s, n):
    iota = lax.broadcasted_iota(jnp.int32, s.shape, 0).astype(F32)
    vals, idxs = [], []
    for _ in range(PEER_TOPK):
        m = jnp.max(s, axis=0, keepdims=True)
        idx = jnp.min(jnp.where(s == m, iota, float(n)), axis=0, keepdims=True)
        vals.append(m)
        idxs.append(idx)
        s = jnp.where(iota == idx, NEG_INF, s)
    return vals, idxs


def _candidate_positions():
    rows = []
    for a in range(4):
        for b in range(16 if a == 0 else 8):
            rows.append((a, b, (a + 1) * (b + 1) <= PEER_TOPK))
    for b, n_a in ((0, 16), (1, 8), (2, 8)):
        for a in range(n_a):
            rows.append((a, b, a >= 4 and (a + 1) * (b + 1) <= PEER_TOPK))
    return np.array([[a * PEER_TOPK + b if ok else -1.0] for a, b, ok in rows], np.float32)


def _peer_head_topk(qh, keys0, keys1, posb):
    k = PEER_TOPK
    tops = []
    for p, keys in enumerate((keys0, keys1)):
        st = _dot_nt(keys, qh[:, p * PEER_HALF:(p + 1) * PEER_HALF].astype(BF16))
        tops.append(_top16(st, N_KEYS))
    (v0, i0), (v1, i1) = tops
    v0c = jnp.concatenate(v0, axis=0)
    v1c = jnp.concatenate(v1, axis=0)
    cand = jnp.concatenate([v0[0] + v1c, v0[1] + v1c[:8], v0[2] + v1c[:8], v0[3] + v1c[:8],
                            v0c + v1[0], v0c[:8] + v1[1], v0c[:8] + v1[2]], axis=0)
    cand = jnp.where(posb >= 0.0, cand, NEG_INF)
    best, poss = [], []
    for _ in range(k):
        m = jnp.max(cand, axis=0, keepdims=True)
        pos = jnp.min(jnp.where(cand == m, posb, float(k * k)), axis=0, keepdims=True)
        best.append(m)
        poss.append(pos)
        cand = jnp.where(posb == pos, NEG_INF, cand)
    best = jnp.concatenate(best, axis=0)
    posc = jnp.concatenate(poss, axis=0)
    a_rank = jnp.floor(posc * (1.0 / k))
    b_rank = posc - k * a_rank
    ea = jnp.zeros_like(posc)
    eb = jnp.zeros_like(posc)
    for r in range(k):
        ea = jnp.where(a_rank == float(r), i0[r], ea)
        eb = jnp.where(b_rank == float(r), i1[r], eb)
    e = jnp.exp(best - best[0:1, :])
    return ea, eb, e / jnp.sum(e, axis=0, keepdims=True)


RSQRT2 = 1.0 / math.sqrt(2.0)


def _peer_kernel(h2n_ref, wq_ref, keys_ref, pos_ref, u_ref, v_ref, x1_ref, g2_ref, lng_ref, lnb_ref,
                 x2_ref, ids_scr, idt_scr, w_scr, p_scr, acc_scr, h2n_scr, h2_scr, *, tb, te, pitch):
    i = pl.program_id(0)
    j = pl.program_id(1)
    slot = i % 2
    k = PEER_TOPK
    rows_per_tile = te // N_KEYS
    half = N_KEYS // 2
    hi_mask = jnp.uint32(0xFFFF0000)

    @pl.when(j == 0)
    def _():
        @pl.when(i == 0)
        def _():
            ids_scr[...] = jnp.zeros_like(ids_scr)
            h2n_scr[...] = jnp.zeros_like(h2n_scr)

        h2_scr[...] = h2n_scr[...]
        h2n_scr[...] = h2n_ref[...].astype(BF16)
        for n in range(3):
            idt_scr[n] = ids_scr[1 - slot, n].T
        acc_scr[...] = jnp.zeros_like(acc_scr)
        iota = lax.broadcasted_iota(jnp.int32, (N_KEYS, PEER_HEADS * k), 0).astype(F32)

        def tok(t, carry):
            a_row = idt_scr[0, pl.ds(t, 1), :]
            b_row = idt_scr[1, pl.ds(t, 1), :]
            g_row = idt_scr[2, pl.ds(t, 1), :] * RSQRT2
            at = jnp.where(iota == a_row, 1.0, 0.0).astype(BF16)
            gbt = jnp.where(iota == b_row, g_row, 0.0).astype(BF16)
            w = pltpu.bitcast(_dot_nt(at, gbt), jnp.uint32)
            w_scr[pl.ds(t, half, stride=pitch), :] = (w[:half] & hi_mask) | (w[half:] >> 16)
            return carry

        lax.fori_loop(0, tb, tok, 0, unroll=32)

    posb = jnp.broadcast_to(pos_ref[...], (pos_ref.shape[0], tb))
    ea, eb, gate = _peer_head_topk(_dot(h2n_scr[...], wq_ref[...]), keys_ref[0, 0], keys_ref[0, 1], posb)
    rows = pl.ds(pl.multiple_of(j * k, k), k)
    ids_scr[slot, 0, rows, :] = ea
    ids_scr[slot, 1, rows, :] = eb
    ids_scr[slot, 2, rows, :] = gate

    a = _dot_nt(h2_scr[...], u_ref[...])
    for r in range(rows_per_tile):
        i1 = j * rows_per_tile + r
        packed = w_scr[pl.ds(pl.multiple_of((i1 % half) * pitch, 8), tb), :]
        shift = jnp.where(i1 >= half, 16, 0).astype(jnp.uint32)
        w = pltpu.bitcast((packed << shift) & hi_mask, F32)
        cs = slice(r * N_KEYS, (r + 1) * N_KEYS)
        ar = a[:, cs]
        p_scr[:, cs] = (w * (ar * (1.0 + lax.erf(ar)))).astype(BF16)
    acc_scr[...] += _dot(p_scr[...], v_ref[...])

    @pl.when(j == pl.num_programs(1) - 1)
    def _():
        x2_ref[...] = _layer_norm(DEEPNORM_ALPHA * x1_ref[...] + g2_ref[0] * acc_scr[...],
                                  lng_ref[...], lnb_ref[...])


def _peer(h2, wq_bf, keys_bf, u_bf, v_bf, x1, mod_l, ln_g, ln_b, rows_of_tile, tb):
    t, d = x1.shape
    n_e = v_bf.shape[0]
    te = PEER_TE
    assert n_e // te == PEER_HEADS
    hk = PEER_HEADS * PEER_TOPK
    pitch = tb + 8
    nblk = t // tb
    mod3 = mod_l.reshape(8, 1, 6 * d)
    pos = _candidate_positions()

    def cur(i):
        return jnp.maximum(i - 1, 0)

    def full(shape):
        return pl.BlockSpec(shape, lambda i, j: (0,) * len(shape))

    return pl.pallas_call(
        functools.partial(_peer_kernel, tb=tb, te=te, pitch=pitch),
        grid=(nblk + 1, PEER_HEADS),
        in_specs=[
            pl.BlockSpec((tb, d), lambda i, j: (jnp.minimum(i, nblk - 1), 0)),
            pl.BlockSpec((d, 2 * PEER_HALF), lambda i, j: (0, j)),
            pl.BlockSpec((1, 2, N_KEYS, PEER_HALF), lambda i, j: (j, 0, 0, 0)),
            full(pos.shape),
            pl.BlockSpec((te, d), lambda i, j: (j, 0)),
            pl.BlockSpec((te, d), lambda i, j: (j, 0)),
            pl.BlockSpec((tb, d), lambda i, j: (cur(i), 0), pipeline_mode=pl.Buffered(1)),
            pl.BlockSpec((1, 1, d), lambda i, j: (rows_of_tile(cur(i)), 0, 5)),
            full((1, d)), full((1, d)),
        ],
        out_specs=pl.BlockSpec((tb, d), lambda i, j: (cur(i), 0)),
        out_shape=jax.ShapeDtypeStruct((t, d), F32),
        scratch_shapes=[pltpu.VMEM((2, 3, hk, tb), F32), pltpu.VMEM((3, tb, hk), F32),
                        pltpu.VMEM((pitch * N_KEYS // 2, N_KEYS), jnp.uint32), pltpu.VMEM((tb, te), BF16),
                        pltpu.VMEM((tb, d), F32), pltpu.VMEM((tb, d), BF16), pltpu.VMEM((tb, d), BF16)],
        compiler_params=pltpu.CompilerParams(dimension_semantics=("arbitrary", "arbitrary"),
                                             vmem_limit_bytes=PEER_VMEM_LIMIT),
        name="peer",
    )(h2, wq_bf, keys_bf, jnp.asarray(pos), u_bf, v_bf, x1, mod3, ln_g.reshape(1, d), ln_b.reshape(1, d))


def _rope_tables(length):
    rows = length // GRID_W
    row = jnp.broadcast_to(jnp.arange(rows, dtype=F32)[:, None], (rows, GRID_W)).reshape(length)
    col = jnp.broadcast_to(jnp.arange(GRID_W, dtype=F32)[None, :], (rows, GRID_W)).reshape(length)
    n_freq = HEAD_DIM // 4
    inv = jnp.power(ROPE_THETA, -jnp.arange(n_freq, dtype=F32) / n_freq)
    ang = jnp.concatenate([row[:, None] * inv, col[:, None] * inv], axis=-1)
    cos, sin = jnp.cos(ang), jnp.sin(ang)
    return jnp.concatenate([cos, cos], axis=-1), jnp.concatenate([-sin, sin], axis=-1)


def kernel(x_prompt, x_sample, cache_attn_k, cache_attn_v, state_hgrn, c, c_ctx, w_mod, b_mod, w_in,
           hgrn_lb_logits, hgrn_norm_g, q_norm_g, k_norm_g, w_branch_rec, w_branch_att, w_out,
           ln1_g, ln1_b, ln2_g, ln2_b, peer_w_query, peer_sub_keys, peer_u, peer_v):
    nb, seq, d = x_prompt.shape
    db, dseq, _ = x_sample.shape
    depth = w_in.shape[0]
    past = cache_attn_k.shape[2]
    t_ctx, t_lat = nb * seq, db * dseq
    t_all = t_ctx + t_lat
    assert d == D_MODEL and 1 + db <= 8

    tm = _tile(math.gcd(t_ctx, dseq), 2048)
    tm_merge = _tile(math.gcd(t_ctx, dseq), 512)
    tb_mix = _tile(math.gcd(t_ctx, dseq), PEER_TB_MIX)

    def rows_of_tile_fn(tile):
        n_ctx = t_ctx // tile
        per = dseq // tile
        return lambda i: jnp.where(i < n_ctx, 0, 1 + (i - n_ctx) // per)

    cond8 = jnp.concatenate([c_ctx[None, :], c, jnp.zeros((8 - 1 - db, d), F32)], axis=0)
    mod = _modulation(cond8, w_mod, b_mod)

    lb = jnp.cumsum(jax.nn.softmax(hgrn_lb_logits.astype(F32), axis=1), axis=1)
    lb = lb - lb[:, :1]

    segs = jnp.split(w_in, np.cumsum([HGRN_WIDTH] * 5 + [ATTN_WIDTH, KV_WIDTH, KV_WIDTH, D_MODEL]).tolist(), axis=-1)
    hq_w, hff_w, hfb_w, hi_w, hg_w, aq_w, ak_w, av_w, ga_w, gb_w = segs
    w_in_bf = jnp.concatenate([hq_w, hff_w, hfb_w, hi_w, hg_w, aq_w, ga_w, gb_w, ak_w, av_w], axis=-1).astype(BF16)
    w_rec_bf, w_att_bf, w_out_bf = (w.astype(BF16) for w in (w_branch_rec, w_branch_att, w_out))
    wq_bf = peer_w_query.astype(BF16)
    keys_bf = peer_sub_keys.astype(BF16)
    u_bf = (peer_u * RSQRT2).astype(BF16)
    v_bf = peer_v.astype(BF16)

    cos_t, sin_t = (jnp.concatenate([t, t], axis=-1) for t in _rope_tables(dseq))
    ones_t, zeros_t = jnp.ones((seq, 2 * HEAD_DIM), F32), jnp.zeros((seq, 2 * HEAD_DIM), F32)
    cache_k_t = jnp.transpose(cache_attn_k, (0, 1, 3, 2, 4))
    cache_k_bf = jnp.concatenate([cache_k_t, cache_k_t], axis=-1).astype(BF16)
    cache_v_t = jnp.transpose(cache_attn_v, (0, 1, 3, 2, 4))
    ones_pad = jnp.zeros(cache_v_t.shape, F32).at[..., 0].set(1.0)
    cache_v_aug = jnp.concatenate([cache_v_t, ones_pad], axis=-1).astype(BF16)

    seq_rows = [(i * seq, seq) for i in range(nb)] + [(t_ctx + i * dseq, dseq) for i in range(db)]
    zero_state = jnp.zeros((nb, 2, HGRN_HEADS, HGRN_DK, HGRN_DK), F32)

    x2d = jnp.concatenate([x_prompt.reshape(t_ctx, d), x_sample.reshape(t_lat, d)], axis=0)
    new_k, new_v, new_s = [], [], []
    for l in range(depth):
        z2d = _in_proj(x2d, mod[l], w_in_bf[l], rows_of_tile_fn(tm), tm)

        s0_all = jnp.concatenate([zero_state, state_hgrn[:, l]], axis=0)
        o_f, o_b, s_fin = _hgrn_scan(z2d, lb[:, l], s0_all, seq_rows, HGRN_CHUNK)
        new_s.append(s_fin[:nb])

        q_c, k_c, v_c, k_norm = _qkv_prep(z2d, 0, nb, seq, q_norm_g[l], k_norm_g[l], ones_t, zeros_t, False, True)
        o_att = _attention(q_c, k_c, v_c, None, 0, t_all)
        new_k.append(k_norm.reshape(nb, seq, KV_HEADS, HEAD_DIM))
        new_v.append(z2d[:t_ctx, COL_AV:].reshape(nb, seq, KV_HEADS, HEAD_DIM))

        q_s, k_s, v_s = _qkv_prep(z2d, t_ctx, db, dseq, q_norm_g[l], k_norm_g[l], cos_t, sin_t, True, False)
        k_s = jnp.concatenate([k_s, cache_k_bf[:, l]], axis=2)
        v_s = jnp.concatenate([v_s, cache_v_aug[:, l]], axis=2)
        o_att = _attention(q_s, k_s, v_s, o_att, t_ctx, t_all)

        x1, h2 = _merge(o_f, o_b, z2d, o_att, x2d, mod[l], w_rec_bf[l], w_att_bf[l], w_out_bf[l],
                        hgrn_norm_g[l], ln1_g[l], ln1_b[l], rows_of_tile_fn(tm_merge), tm_merge)

        x2d = _peer(h2, wq_bf[l], keys_bf[l], u_bf[l], v_bf[l], x1, mod[l], ln2_g[l], ln2_b[l],
                    rows_of_tile_fn(tb_mix), tb_mix)

    y_prompt = x2d[:t_ctx].reshape(nb, seq, d)
    y_sample = x2d[t_ctx:].reshape(db, dseq, d)
    return (y_prompt, y_sample, jnp.stack(new_k, axis=1), jnp.stack(new_v, axis=1), jnp.stack(new_s, axis=1))
```

```python
import functools
import math

import numpy as np
import jax
import jax.numpy as jnp
from jax import lax
from jax.experimental import pallas as pl
from jax.experimental.pallas import tpu as pltpu

F32 = jnp.float32
BF16 = jnp.bfloat16

D_MODEL = 1024
GRID_W = 64
HGRN_HEADS = 8
HGRN_DK = 128
HGRN_WIDTH = HGRN_HEADS * HGRN_DK
HGRN_CHUNK = 128
ATTN_HEADS = 16
KV_HEADS = 4
HEAD_DIM = 64
GROUP = ATTN_HEADS // KV_HEADS
ATTN_WIDTH = ATTN_HEADS * HEAD_DIM
KV_WIDTH = KV_HEADS * HEAD_DIM
ATTN_SCALE = HEAD_DIM ** -0.5
ROPE_THETA = 10000.0
PEER_HEADS = 8
PEER_HALF = 128
N_KEYS = 128
N_EXPERTS = N_KEYS * N_KEYS
PEER_TOPK = 16
MODEL_DEPTH = 4
DEEPNORM_ALPHA = (2 * MODEL_DEPTH) ** 0.25
EPS = 1e-6
NEG_INF = float("-inf")

IN_COLS = 5 * HGRN_WIDTH + ATTN_WIDTH + 2 * D_MODEL + 2 * KV_WIDTH
COL_AQ = 5 * HGRN_WIDTH
COL_GA = COL_AQ + ATTN_WIDTH
COL_GB = COL_GA + D_MODEL
COL_AK = COL_GB + D_MODEL
COL_AV = COL_AK + KV_WIDTH

VMEM_LIMIT = 56 * 1024 * 1024
PEER_VMEM_LIMIT = 60 * 1024 * 1024
LOG2_E = math.log2(math.e)
ATTN_TQ = 512
ATTN_TS = 512
PEER_TB_MIX = 512
PEER_TE = 2048


def _params(*sem):
    return pltpu.CompilerParams(dimension_semantics=sem, vmem_limit_bytes=VMEM_LIMIT)


def _tile(n, pref):
    t = min(n, pref)
    while n % t or t % 8:
        t -= 1
    return t


def _sigmoid(x):
    return 1.0 / (1.0 + jnp.exp(-x))


def _sigmoid_t(x):
    return 0.5 + 0.5 * jnp.tanh(0.5 * x)


def _silu(x):
    h = 0.5 * x
    return h + h * jnp.tanh(h)


def _bf16_floor(x):
    bits = pltpu.bitcast(x, jnp.uint32) & jnp.uint32(0xFFFF0000)
    return pltpu.bitcast(bits, F32)


def _dot(a, b):
    return jnp.dot(a, b, preferred_element_type=F32)


def _dot_nt(a, b):
    return lax.dot_general(a, b, (((1,), (1,)), ((), ())), preferred_element_type=F32)


def _mod_kernel(cond_ref, w_ref, b_ref, o_ref):
    c = cond_ref[...]
    s = c * _sigmoid(c)
    o_ref[0] = jnp.dot(s, w_ref[0], precision=lax.Precision.HIGHEST, preferred_element_type=F32) + b_ref[0]


def _modulation(cond8, w_mod, b_mod):
    depth, d, n = w_mod.shape
    tn = _tile(n, 1536)
    return pl.pallas_call(
        _mod_kernel,
        grid=(depth, n // tn),
        in_specs=[
            pl.BlockSpec((8, d), lambda l, j: (0, 0)),
            pl.BlockSpec((1, d, tn), lambda l, j: (l, 0, j)),
            pl.BlockSpec((1, 1, tn), lambda l, j: (l, 0, j)),
        ],
        out_specs=pl.BlockSpec((1, 8, tn), lambda l, j: (l, 0, j)),
        out_shape=jax.ShapeDtypeStruct((depth, 8, n), F32),
        compiler_params=_params("parallel", "parallel"),
        name="modulation",
    )(cond8, w_mod, b_mod.reshape(depth, 1, n))


def _inproj_kernel(x_ref, sh_ref, sc_ref, w_ref, z_ref, h_scr):
    @pl.when(pl.program_id(1) == 0)
    def _():
        h_scr[...] = (x_ref[...] * (1.0 + sc_ref[0]) + sh_ref[0]).astype(BF16)

    z_ref[...] = _dot(h_scr[...], w_ref[...])


def _in_proj(x2d, mod_l, w_in_bf, rows_of_tile, tm):
    t, d = x2d.shape
    n = w_in_bf.shape[1]
    tn = 512
    mod3 = mod_l.reshape(8, 1, 6 * d)
    return pl.pallas_call(
        _inproj_kernel,
        grid=(t // tm, n // tn),
        in_specs=[
            pl.BlockSpec((tm, d), lambda i, j: (i, 0)),
            pl.BlockSpec((1, 1, d), lambda i, j: (rows_of_tile(i), 0, 0)),
            pl.BlockSpec((1, 1, d), lambda i, j: (rows_of_tile(i), 0, 1)),
            pl.BlockSpec((d, tn), lambda i, j: (0, j)),
        ],
        out_specs=pl.BlockSpec((tm, tn), lambda i, j: (i, j)),
        out_shape=jax.ShapeDtypeStruct((t, n), F32),
        scratch_shapes=[pltpu.VMEM((tm, d), BF16)],
        compiler_params=_params("parallel", "arbitrary"),
        name="in_proj",
    )(x2d, mod3, mod3, w_in_bf)


def _level_reference_rows(beta, c):
    width = beta.shape[1]
    sub = lax.broadcasted_iota(jnp.int32, (8, width), 0)

    def bc(i, n=8):
        return jnp.broadcast_to(beta[i:i + 1, :], (n, width))

    out = []
    m = 1
    while m < c:
        blocks = []
        if m >= 8:
            for blk in range(c // (2 * m)):
                blocks.append(bc(blk * 2 * m + m, 2 * m))
        elif m == 4:
            for v in range(c // 8):
                blocks.append(bc(8 * v + 4))
        elif m == 2:
            for v in range(c // 8):
                blocks.append(jnp.where(sub < 4, bc(8 * v + 2), bc(8 * v + 6)))
        else:
            for v in range(c // 8):
                lo = jnp.where(sub < 2, bc(8 * v + 1), bc(8 * v + 3))
                hi = jnp.where(sub < 6, bc(8 * v + 5), bc(8 * v + 7))
                blocks.append(jnp.where(sub < 4, lo, hi))
        out.append(blocks[0] if len(blocks) == 1 else jnp.concatenate(blocks, axis=0))
        m *= 2
    return out


def _hgrn_direction(zq, zf, v, lb, st_ref, o_ref, tri, eye, q_half, pair_masks, reverse, c):
    heads = [slice(h * HGRN_DK, (h + 1) * HGRN_DK) for h in range(HGRN_HEADS)]
    q = zq * pl.reciprocal(1.0 + jnp.exp2(zq * -LOG2_E), approx=True)
    zl = zf * -LOG2_E
    e = jnp.exp2(zl)
    l2 = jnp.log(1.0 + e) * LOG2_E
    g = jnp.log(1.0 + lb * e) * LOG2_E - l2
    k = (1.0 - lb) * jnp.exp2(zl - l2)
    g1 = _bf16_floor(g)
    r1 = g - g1
    g2 = _bf16_floor(r1)
    g3 = r1 - g2
    beta = (_dot(tri, g1.astype(BF16)) + _dot(tri, g2.astype(BF16))) + _dot(tri, g3.astype(BF16))
    last = 0 if reverse else c - 1
    beta_tot = beta[last:last + 1, :]

    q_bf, k_bf, v_bf = q.astype(BF16), k.astype(BF16), v.astype(BF16)
    a = [jnp.where(eye, _dot_nt(q_bf[:, hs], k_bf[:, hs]), 0.0) for hs in heads]
    for r, qh, pm in zip(_level_reference_rows(beta, c), q_half, pair_masks):
        neg_abs = pltpu.bitcast(pltpu.bitcast(beta - r, jnp.uint32) | jnp.uint32(0x80000000), F32)
        qk = jnp.concatenate([jnp.where(qh, q[:, hs], k[:, hs]) for hs in heads], axis=1)
        u = (qk * jnp.exp2(neg_abs)).astype(BF16)
        a = [ah + jnp.where(pm, _dot_nt(u[:, hs], u[:, hs]), 0.0) for ah, hs in zip(a, heads)]

    qb = (q * jnp.exp2(beta)).astype(BF16)
    kb = (k * jnp.exp2(beta_tot - beta)).astype(BF16)
    dec = jnp.exp2(beta_tot)
    for h, hs in enumerate(heads):
        st = st_ref[h]
        o_ref[:, hs] = _dot(a[h].astype(BF16), v_bf[:, hs]) + _dot_nt(qb[:, hs], st.astype(BF16))
        st_ref[h] = st * dec[:, hs] + lax.dot_general(
            v_bf[:, hs], kb[:, hs], (((0,), (0,)), ((), ())), preferred_element_type=F32)


def _hgrn_kernel(rbf_ref, rbb_ref, first_ref, seq_ref,
                 qf_ref, ff_ref, vf_ref, qb_ref, fb_ref, vb_ref, lb_ref, s0_ref,
                 of_ref, ob_ref, sout_ref, st_scr, *, c, n_last):
    s = pl.program_id(0)

    @pl.when(first_ref[s] == 1)
    def _():
        for d in range(2):
            for h in range(HGRN_HEADS):
                st_scr[d, h] = s0_ref[0, d, h].T

    row = lax.broadcasted_iota(jnp.int32, (c, c), 0)
    col = lax.broadcasted_iota(jnp.int32, (c, c), 1)
    rowd = lax.broadcasted_iota(jnp.int32, (c, HGRN_DK), 0)
    eye = row == col

    for d, (q_ref, f_ref, v_ref, o_ref) in enumerate(
            ((qf_ref, ff_ref, vf_ref, of_ref), (qb_ref, fb_ref, vb_ref, ob_ref))):
        reverse = d == 1
        tri = (col >= row if reverse else col <= row).astype(BF16)
        q_par = 0 if reverse else 1
        q_half, pair_masks = [], []
        m = 1
        while m < c:
            q_half.append(((rowd // m) % 2) == q_par)
            pair_masks.append(((row // (2 * m)) == (col // (2 * m)))
                              & (((row // m) % 2) == q_par) & (((col // m) % 2) == 1 - q_par))
            m *= 2
        _hgrn_direction(q_ref[...], f_ref[...], v_ref[...], lb_ref[pl.ds(d, 1), :], st_scr.at[d], o_ref,
                        tri, eye, q_half, pair_masks, reverse, c)

    @pl.when(first_ref[jnp.minimum(s + 1, n_last)] == 1)
    def _():
        for d in range(2):
            for h in range(HGRN_HEADS):
                sout_ref[0, d, h] = st_scr[d, h].T


def _hgrn_scan(z2d, lb2, s0_all, seq_rows, c):
    t = z2d.shape[0]
    rbf, rbb, first, seq = [], [], [], []
    for si, (r0, length) in enumerate(seq_rows):
        n = length // c
        for ci in range(n):
            rbf.append(r0 // c + ci)
            rbb.append(r0 // c + n - 1 - ci)
            first.append(1 if ci == 0 else 0)
            seq.append(si)
    n_steps = len(rbf)
    first.append(1)
    w = HGRN_WIDTH

    def zspec(which, colblk):
        if which == 0:
            return pl.BlockSpec((c, w), lambda s, rbf, rbb, fi, sq: (rbf[s], colblk))
        return pl.BlockSpec((c, w), lambda s, rbf, rbb, fi, sq: (rbb[s], colblk))

    n_seq = len(seq_rows)
    state_spec = pl.BlockSpec((1, 2, HGRN_HEADS, HGRN_DK, HGRN_DK),
                              lambda s, rbf, rbb, fi, sq: (sq[s], 0, 0, 0, 0))
    grid_spec = pltpu.PrefetchScalarGridSpec(
        num_scalar_prefetch=4,
        grid=(n_steps,),
        in_specs=[zspec(0, 0), zspec(0, 1), zspec(0, 3), zspec(1, 0), zspec(1, 2), zspec(1, 3),
                  pl.BlockSpec((2, w), lambda s, *_: (0, 0)), state_spec],
        out_specs=[zspec(0, 0), zspec(1, 0), state_spec],
        scratch_shapes=[pltpu.VMEM((2, HGRN_HEADS, HGRN_DK, HGRN_DK), F32)],
    )
    return pl.pallas_call(
        functools.partial(_hgrn_kernel, c=c, n_last=n_steps),
        grid_spec=grid_spec,
        out_shape=[jax.ShapeDtypeStruct((t, w), F32), jax.ShapeDtypeStruct((t, w), F32),
                   jax.ShapeDtypeStruct((n_seq, 2, HGRN_HEADS, HGRN_DK, HGRN_DK), F32)],
        compiler_params=_params("arbitrary"),
        name="hgrn_scan",
    )(jnp.asarray(rbf, jnp.int32), jnp.asarray(rbb, jnp.int32), jnp.asarray(first, jnp.int32),
      jnp.asarray(seq, jnp.int32), z2d, z2d, z2d, z2d, z2d, z2d, lb2, s0_all)


def _qkv_prep_kernel(aq_ref, ak_ref, av_ref, qg_ref, kg_ref, cos_ref, sin_ref, swap_ref, avg_ref, *out_refs,
                     rope, q_scale, keep_k):
    q_out, k_out, v_out = out_refs[:3]
    hd = HEAD_DIM
    tl = aq_ref.shape[0]
    lane = lax.broadcasted_iota(jnp.int32, (tl, 2 * hd), 1)
    low = lane < hd
    ones_col = jnp.where(lane == hd, 1.0, 0.0)

    def norm_rope(pair, g):
        sq = pair * pair
        hi = _bf16_floor(sq)
        ms = _dot(hi.astype(BF16), avg_ref[...]) + _dot((sq - hi).astype(BF16), avg_ref[...])
        y = pair * lax.rsqrt(ms + EPS) * g
        if rope:
            y = y * cos_ref[...] + _dot(y.astype(BF16), swap_ref[...]) * sin_ref[...]
        return y

    for j in range(ATTN_WIDTH // (2 * hd)):
        y = norm_rope(aq_ref[:, j * 2 * hd:(j + 1) * 2 * hd], qg_ref[...]) * q_scale
        q_out[0, 2 * j] = jnp.where(low, y, 0.0).astype(BF16)
        q_out[0, 2 * j + 1] = jnp.where(low, 0.0, y).astype(BF16)
    for j in range(KV_WIDTH // (2 * hd)):
        cols = slice(j * 2 * hd, (j + 1) * 2 * hd)
        y = norm_rope(ak_ref[:, cols], kg_ref[...])
        y_other = pltpu.roll(y, hd, axis=1)
        k_out[0, 2 * j] = jnp.where(low, y, y_other).astype(BF16)
        k_out[0, 2 * j + 1] = jnp.where(low, y_other, y).astype(BF16)
        if keep_k:
            out_refs[3][:, cols] = y
        vp = av_ref[:, cols]
        v_out[0, 2 * j] = jnp.where(low, vp, ones_col).astype(BF16)
        v_out[0, 2 * j + 1] = jnp.where(low, pltpu.roll(vp, hd, axis=1), ones_col).astype(BF16)


def _qkv_prep(z2d, row0, n_seq, length, q_gain, k_gain, cos, sin, rope, keep_k):
    hd = HEAD_DIM
    tl = _tile(length, 512)
    nt = length // tl
    blk0 = row0 // tl
    swap = np.zeros((2 * hd, 2 * hd), np.float32)
    avg = np.zeros((2 * hd, 2 * hd), np.float32)
    for h0 in (0, hd):
        avg[h0:h0 + hd, h0:h0 + hd] = 1.0 / hd
        for i in range(hd):
            swap[h0 + (i + hd // 2) % hd, h0 + i] = 1.0

    def zspec(width, col):
        return pl.BlockSpec((tl, width), lambda b, j: (blk0 + b * nt + j, col // width))

    def full(shape):
        return pl.BlockSpec(shape, lambda b, j: (0,) * len(shape))

    def head_major(n_heads):
        return (pl.BlockSpec((1, n_heads, tl, 2 * hd), lambda b, j: (b, 0, j, 0)),
                jax.ShapeDtypeStruct((n_seq, n_heads, length, 2 * hd), BF16))

    out_specs, out_shape = map(list, zip(head_major(ATTN_HEADS), head_major(KV_HEADS), head_major(KV_HEADS)))
    if keep_k:
        out_specs.append(pl.BlockSpec((tl, KV_WIDTH), lambda b, j: (b * nt + j, 0)))
        out_shape.append(jax.ShapeDtypeStruct((n_seq * length, KV_WIDTH), F32))

    def both_halves(g):
        return jnp.concatenate([g, g]).reshape(1, 2 * hd)

    return pl.pallas_call(
        functools.partial(_qkv_prep_kernel, rope=rope, q_scale=ATTN_SCALE * LOG2_E, keep_k=keep_k),
        grid=(n_seq, nt),
        in_specs=[zspec(ATTN_WIDTH, COL_AQ), zspec(KV_WIDTH, COL_AK), zspec(KV_WIDTH, COL_AV),
                  full((1, 2 * hd)), full((1, 2 * hd)),
                  pl.BlockSpec((tl, 2 * hd), lambda b, j: (j, 0)), pl.BlockSpec((tl, 2 * hd), lambda b, j: (j, 0)),
                  full((2 * hd, 2 * hd)), full((2 * hd, 2 * hd))],
        out_specs=out_specs,
        out_shape=out_shape,
        compiler_params=_params("parallel", "parallel"),
        name="qkv_prep",
    )(z2d, z2d, z2d, both_halves(q_gain), both_halves(k_gain), cos, sin,
      jnp.asarray(swap, BF16), jnp.asarray(avg, BF16))


def _attn_kernel(q_ref, k_ref, v_ref, *rest, ts, n_s, tq):
    o_ref = rest[-1]
    rows = GROUP * tq
    q = q_ref[0].reshape(rows, 2 * HEAD_DIM)

    def body(i, carry):
        m, acc = carry
        sl = pl.ds(pl.multiple_of(i * ts, ts), ts)
        s = _dot_nt(q, k_ref[0, 0, sl, :])
        m_new = jnp.maximum(m, jnp.max(s, axis=-1, keepdims=True))
        p = jnp.exp2(s - m_new)
        acc = jnp.exp2(m - m_new) * acc + _dot(p.astype(BF16), v_ref[0, 0, sl, :])
        return m_new, acc

    m0 = jnp.full((rows, 1), NEG_INF, F32)
    acc0 = jnp.zeros((rows, 2 * HEAD_DIM), F32)
    m, acc = lax.fori_loop(0, n_s, body, (m0, acc0), unroll=True)
    o = acc[:, :HEAD_DIM] / acc[:, HEAD_DIM:HEAD_DIM + 1]
    for g in range(GROUP):
        o_ref[:, g * HEAD_DIM:(g + 1) * HEAD_DIM] = o[g * tq:(g + 1) * tq]


def _attention(q, k, v_aug, o_prev, row0, t_all):
    b, _, length, _ = q.shape
    hd = HEAD_DIM
    s_len = k.shape[2]
    tq = _tile(length, ATTN_TQ)
    ts = _tile(s_len, ATTN_TS)
    nq = length // tq
    blk0 = row0 // tq
    in_specs = [
        pl.BlockSpec((1, GROUP, tq, 2 * hd), lambda bi, hi, qi: (bi, hi, qi, 0)),
        pl.BlockSpec((1, 1, s_len, 2 * hd), lambda bi, hi, qi: (bi, hi, 0, 0)),
        pl.BlockSpec((1, 1, s_len, 2 * hd), lambda bi, hi, qi: (bi, hi, 0, 0)),
    ]
    args = [q, k, v_aug]
    aliases = {}
    if o_prev is not None:
        in_specs.append(pl.BlockSpec(memory_space=pl.ANY))
        args.append(o_prev)
        aliases = {3: 0}
    return pl.pallas_call(
        functools.partial(_attn_kernel, ts=ts, n_s=s_len // ts, tq=tq),
        grid=(b, KV_HEADS, nq),
        in_specs=in_specs,
        out_specs=pl.BlockSpec((tq, GROUP * hd), lambda bi, hi, qi: (blk0 + bi * nq + qi, hi)),
        out_shape=jax.ShapeDtypeStruct((t_all, ATTN_WIDTH), F32),
        input_output_aliases=aliases,
        compiler_params=_params("parallel", "parallel", "parallel"),
        name="attention",
    )(*args)


def _layer_norm(t, g, b):
    mu = jnp.mean(t, axis=-1, keepdims=True)
    tc = t - mu
    var = jnp.mean(tc * tc, axis=-1, keepdims=True)
    return tc * lax.rsqrt(var + EPS) * g + b


def _merge_kernel(of_ref, ob_ref, hg_ref, ga_ref, gb_ref, oatt_ref, x_ref, g1_ref, sh2_ref, sc2_ref,
                  wrec_ref, watt_ref, wout_ref, ng_ref, lng_ref, lnb_ref, x1_ref, h2_ref, orec_scr):
    for h in range(HGRN_HEADS):
        cs = slice(h * HGRN_DK, (h + 1) * HGRN_DK)
        o = of_ref[:, cs] + ob_ref[:, cs]
        ms = jnp.mean(o * o, axis=-1, keepdims=True)
        zg = hg_ref[:, cs]
        orec_scr[:, cs] = (o * lax.rsqrt(ms + EPS) * ng_ref[...] * _silu(zg)).astype(BF16)
    y = (_sigmoid_t(ga_ref[...]) * _dot(orec_scr[...], wrec_ref[...])
         + _sigmoid_t(gb_ref[...]) * _dot(oatt_ref[...].astype(BF16), watt_ref[...]))
    u = _dot(y.astype(BF16), wout_ref[...])
    x1 = _layer_norm(DEEPNORM_ALPHA * x_ref[...] + g1_ref[0] * u, lng_ref[...], lnb_ref[...])
    x1_ref[...] = x1
    h2_ref[...] = x1 * (1.0 + sc2_ref[0]) + sh2_ref[0]


def _merge(o_f, o_b, z2d, o_att, x2d, mod_l, w_rec, w_att, w_out, norm_g, ln_g, ln_b, rows_of_tile, tm):
    t, d = x2d.shape
    mod3 = mod_l.reshape(8, 1, 6 * d)

    def rows(colblk):
        return pl.BlockSpec((tm, d), lambda i: (i, colblk))

    def modspec(chunk):
        return pl.BlockSpec((1, 1, d), lambda i: (rows_of_tile(i), 0, chunk))

    def full(shape):
        return pl.BlockSpec(shape, lambda i: (0,) * len(shape))

    def weight():
        return pl.BlockSpec((d, d), lambda i: (0, 0), pipeline_mode=pl.Buffered(1))

    return pl.pallas_call(
        _merge_kernel,
        grid=(t // tm,),
        in_specs=[rows(0), rows(0), rows(4), rows(COL_GA // d), rows(COL_GB // d), rows(0), rows(0),
                  modspec(2), modspec(3), modspec(4),
                  weight(), weight(), weight(), full((1, HGRN_DK)), full((1, d)), full((1, d))],
        out_specs=[rows(0), rows(0)],
        out_shape=[jax.ShapeDtypeStruct((t, d), F32), jax.ShapeDtypeStruct((t, d), F32)],
        scratch_shapes=[pltpu.VMEM((tm, d), BF16)],
        compiler_params=_params("parallel"),
        name="merge",
    )(o_f, o_b, z2d, z2d, z2d, o_att, x2d, mod3, mod3, mod3, w_rec, w_att, w_out,
      norm_g.reshape(1, HGRN_DK), ln_g.reshape(1, d), ln_b.reshape(1, d))


def _top16(s, n):
    iota = lax.broadcasted_iota(jnp.int32, s.shape, 0).astype(F32)
    vals, idxs = [], []
    for _ in range(PEER_TOPK):
        m = jnp.max(s, axis=0, keepdims=True)
        idx = jnp.min(jnp.where(s == m, iota, float(n)), axis=0, keepdims=True)
        vals.append(m)
        idxs.append(idx)
        s = jnp.where(iota == idx, NEG_INF, s)
    return vals, idxs


def _candidate_positions():
    rows = []
    for a in range(4):
        for b in range(16 if a == 0 else 8):
            rows.append((a, b, (a + 1) * (b + 1) <= PEER_TOPK))
    for b, n_a in ((0, 16), (1, 8), (2, 8)):
        for a in range(n_a):
            rows.append((a, b, a >= 4 and (a + 1) * (b + 1) <= PEER_TOPK))
    return np.array([[a * PEER_TOPK + b if ok else -1.0] for a, b, ok in rows], np.float32)


def _peer_head_topk(qh, keys0, keys1, posb):
    k = PEER_TOPK
    tops = []
    for p, keys in enumerate((keys0, keys1)):
        st = _dot_nt(keys, qh[:, p * PEER_HALF:(p + 1) * PEER_HALF].astype(BF16))
        tops.append(_top16(st, N_KEYS))
    (v0, i0), (v1, i1) = tops
    v0c = jnp.concatenate(v0, axis=0)
    v1c = jnp.concatenate(v1, axis=0)
    cand = jnp.concatenate([v0[0] + v1c, v0[1] + v1c[:8], v0[2] + v1c[:8], v0[3] + v1c[:8],
                            v0c + v1[0], v0c[:8] + v1[1], v0c[:8] + v1[2]], axis=0)
    cand = jnp.where(posb >= 0.0, cand, NEG_INF)
    best, poss = [], []
    for _ in range(k):
        m = jnp.max(cand, axis=0, keepdims=True)
        pos = jnp.min(jnp.where(cand == m, posb, float(k * k)), axis=0, keepdims=True)
        best.append(m)
        poss.append(pos)
        cand = jnp.where(posb == pos, NEG_INF, cand)
    best = jnp.concatenate(best, axis=0)
    posc = jnp.concatenate(poss, axis=0)
    a_rank = jnp.floor(posc * (1.0 / k))
    b_rank = posc - k * a_rank
    ea = jnp.zeros_like(posc)
    eb = jnp.zeros_like(posc)
    for r in range(k):
        ea = jnp.where(a_rank == float(r), i0[r], ea)
        eb = jnp.where(b_rank == float(r), i1[r], eb)
    e = jnp.exp(best - best[0:1, :])
    return ea, eb, e / jnp.sum(e, axis=0, keepdims=True)


RSQRT2 = 1.0 / math.sqrt(2.0)


def _peer_kernel(h2n_ref, wq_ref, keys_ref, pos_ref, u_ref, v_ref, x1_ref, g2_ref, lng_ref, lnb_ref,
                 x2_ref, ids_scr, idt_scr, w_scr, p_scr, acc_scr, h2n_scr, h2_scr, *, tb, te, pitch, nblk):
    i = pl.program_id(0)
    j = pl.program_id(1)
    slot = i % 2
    k = PEER_TOPK
    rows_per_tile = te // N_KEYS
    half = N_KEYS // 2
    hi_mask = jnp.uint32(0xFFFF0000)
    first_row = i == 0
    last_row = i == nblk

    @pl.when(jnp.logical_and(j == 0, jnp.logical_not(first_row)))
    def _():
        h2_scr[...] = h2n_scr[...]
        for n in range(3):
            idt_scr[n] = ids_scr[1 - slot, n].T
        acc_scr[...] = jnp.zeros_like(acc_scr)
        iota = lax.broadcasted_iota(jnp.int32, (N_KEYS, PEER_HEADS * k), 0).astype(F32)

        def tok(t, carry):
            a_row = idt_scr[0, pl.ds(t, 1), :]
            b_row = idt_scr[1, pl.ds(t, 1), :]
            g_row = idt_scr[2, pl.ds(t, 1), :] * RSQRT2
            at = jnp.where(iota == a_row, 1.0, 0.0).astype(BF16)
            gbt = jnp.where(iota == b_row, g_row, 0.0).astype(BF16)
            w = pltpu.bitcast(_dot_nt(at, gbt), jnp.uint32)
            w_scr[pl.ds(t, half, stride=pitch), :] = (w[:half] & hi_mask) | (w[half:] >> 16)
            return carry

        lax.fori_loop(0, tb, tok, 0, unroll=32)

    @pl.when(jnp.logical_and(j == 0, jnp.logical_not(last_row)))
    def _():
        h2n_scr[...] = h2n_ref[...].astype(BF16)

    def retrieve():
        posb = jnp.broadcast_to(pos_ref[...], (pos_ref.shape[0], tb))
        ea, eb, gate = _peer_head_topk(_dot(h2n_scr[...], wq_ref[...]), keys_ref[0, 0], keys_ref[0, 1], posb)
        rows = pl.ds(pl.multiple_of(j * k, k), k)
        ids_scr[slot, 0, rows, :] = ea
        ids_scr[slot, 1, rows, :] = eb
        ids_scr[slot, 2, rows, :] = gate

    def mix():
        a = _dot_nt(h2_scr[...], u_ref[...])
        for r in range(rows_per_tile):
            i1 = j * rows_per_tile + r
            packed = w_scr[pl.ds(pl.multiple_of((i1 % half) * pitch, 8), tb), :]
            shift = jnp.where(i1 >= half, 16, 0).astype(jnp.uint32)
            w = pltpu.bitcast((packed << shift) & hi_mask, F32)
            cs = slice(r * N_KEYS, (r + 1) * N_KEYS)
            ar = a[:, cs]
            p_scr[:, cs] = (w * (ar * (1.0 + lax.erf(ar)))).astype(BF16)
        acc_scr[...] += _dot(p_scr[...], v_ref[...])

    @pl.when(first_row)
    def _():
        retrieve()

    @pl.when(last_row)
    def _():
        mix()

    @pl.when(jnp.logical_not(jnp.logical_or(first_row, last_row)))
    def _():
        retrieve()
        mix()

    @pl.when(jnp.logical_and(j == pl.num_programs(1) - 1, jnp.logical_not(first_row)))
    def _():
        x2_ref[...] = _layer_norm(DEEPNORM_ALPHA * x1_ref[...] + g2_ref[0] * acc_scr[...],
                                  lng_ref[...], lnb_ref[...])


def _peer(h2, wq_bf, keys_bf, u_bf, v_bf, x1, mod_l, ln_g, ln_b, rows_of_tile, tb):
    t, d = x1.shape
    n_e = v_bf.shape[0]
    te = PEER_TE
    assert n_e // te == PEER_HEADS
    hk = PEER_HEADS * PEER_TOPK
    pitch = tb + 8
    nblk = t // tb
    mod3 = mod_l.reshape(8, 1, 6 * d)
    pos = _candidate_positions()

    def cur(i):
        return jnp.maximum(i - 1, 0)

    def full(shape):
        return pl.BlockSpec(shape, lambda i, j: (0,) * len(shape))

    return pl.pallas_call(
        functools.partial(_peer_kernel, tb=tb, te=te, pitch=pitch, nblk=nblk),
        grid=(nblk + 1, PEER_HEADS),
        in_specs=[
            pl.BlockSpec((tb, d), lambda i, j: (jnp.minimum(i, nblk - 1), 0)),
            pl.BlockSpec((d, 2 * PEER_HALF), lambda i, j: (0, j)),
            pl.BlockSpec((1, 2, N_KEYS, PEER_HALF), lambda i, j: (j, 0, 0, 0)),
            full(pos.shape),
            pl.BlockSpec((te, d), lambda i, j: (j, 0)),
            pl.BlockSpec((te, d), lambda i, j: (j, 0)),
            pl.BlockSpec((tb, d), lambda i, j: (cur(i), 0), pipeline_mode=pl.Buffered(1)),
            pl.BlockSpec((1, 1, d), lambda i, j: (rows_of_tile(cur(i)), 0, 5)),
            full((1, d)), full((1, d)),
        ],
        out_specs=pl.BlockSpec((tb, d), lambda i, j: (cur(i), 0)),
        out_shape=jax.ShapeDtypeStruct((t, d), F32),
        scratch_shapes=[pltpu.VMEM((2, 3, hk, tb), F32), pltpu.VMEM((3, tb, hk), F32),
                        pltpu.VMEM((pitch * N_KEYS // 2, N_KEYS), jnp.uint32), pltpu.VMEM((tb, te), BF16),
                        pltpu.VMEM((tb, d), F32), pltpu.VMEM((tb, d), BF16), pltpu.VMEM((tb, d), BF16)],
        compiler_params=pltpu.CompilerParams(dimension_semantics=("arbitrary", "arbitrary"),
                                             vmem_limit_bytes=PEER_VMEM_LIMIT),
        name="peer",
    )(h2, wq_bf, keys_bf, jnp.asarray(pos), u_bf, v_bf, x1, mod3, ln_g.reshape(1, d), ln_b.reshape(1, d))


def _rope_tables(length):
    rows = length // GRID_W
    row = jnp.broadcast_to(jnp.arange(rows, dtype=F32)[:, None], (rows, GRID_W)).reshape(length)
    col = jnp.broadcast_to(jnp.arange(GRID_W, dtype=F32)[None, :], (rows, GRID_W)).reshape(length)
    n_freq = HEAD_DIM // 4
    inv = jnp.power(ROPE_THETA, -jnp.arange(n_freq, dtype=F32) / n_freq)
    ang = jnp.concatenate([row[:, None] * inv, col[:, None] * inv], axis=-1)
    cos, sin = jnp.cos(ang), jnp.sin(ang)
    return jnp.concatenate([cos, cos], axis=-1), jnp.concatenate([-sin, sin], axis=-1)


def kernel(x_prompt, x_sample, cache_attn_k, cache_attn_v, state_hgrn, c, c_ctx, w_mod, b_mod, w_in,
           hgrn_lb_logits, hgrn_norm_g, q_norm_g, k_norm_g, w_branch_rec, w_branch_att, w_out,
           ln1_g, ln1_b, ln2_g, ln2_b, peer_w_query, peer_sub_keys, peer_u, peer_v):
    nb, seq, d = x_prompt.shape
    db, dseq, _ = x_sample.shape
    depth = w_in.shape[0]
    past = cache_attn_k.shape[2]
    t_ctx, t_lat = nb * seq, db * dseq
    t_all = t_ctx + t_lat
    assert d == D_MODEL and 1 + db <= 8

    tm = _tile(math.gcd(t_ctx, dseq), 2048)
    tm_merge = _tile(math.gcd(t_ctx, dseq), 512)
    tb_mix = _tile(math.gcd(t_ctx, dseq), PEER_TB_MIX)

    def rows_of_tile_fn(tile):
        n_ctx = t_ctx // tile
        per = dseq // tile
        return lambda i: jnp.where(i < n_ctx, 0, 1 + (i - n_ctx) // per)

    cond8 = jnp.concatenate([c_ctx[None, :], c, jnp.zeros((8 - 1 - db, d), F32)], axis=0)
    mod = _modulation(cond8, w_mod, b_mod)

    lb = jnp.cumsum(jax.nn.softmax(hgrn_lb_logits.astype(F32), axis=1), axis=1)
    lb = lb - lb[:, :1]

    segs = jnp.split(w_in, np.cumsum([HGRN_WIDTH] * 5 + [ATTN_WIDTH, KV_WIDTH, KV_WIDTH, D_MODEL]).tolist(), axis=-1)
    hq_w, hff_w, hfb_w, hi_w, hg_w, aq_w, ak_w, av_w, ga_w, gb_w = segs
    w_in_bf = jnp.concatenate([hq_w, hff_w, hfb_w, hi_w, hg_w, aq_w, ga_w, gb_w, ak_w, av_w], axis=-1).astype(BF16)
    w_rec_bf, w_att_bf, w_out_bf = (w.astype(BF16) for w in (w_branch_rec, w_branch_att, w_out))
    wq_bf = peer_w_query.astype(BF16)
    keys_bf = peer_sub_keys.astype(BF16)
    u_bf = (peer_u * RSQRT2).astype(BF16)
    v_bf = peer_v.astype(BF16)

    cos_t, sin_t = (jnp.concatenate([t, t], axis=-1) for t in _rope_tables(dseq))
    ones_t, zeros_t = jnp.ones((seq, 2 * HEAD_DIM), F32), jnp.zeros((seq, 2 * HEAD_DIM), F32)
    cache_k_t = jnp.transpose(cache_attn_k, (0, 1, 3, 2, 4))
    cache_k_bf = jnp.concatenate([cache_k_t, cache_k_t], axis=-1).astype(BF16)
    cache_v_t = jnp.transpose(cache_attn_v, (0, 1, 3, 2, 4))
    ones_pad = jnp.zeros(cache_v_t.shape, F32).at[..., 0].set(1.0)
    cache_v_aug = jnp.concatenate([cache_v_t, ones_pad], axis=-1).astype(BF16)

    seq_rows = [(i * seq, seq) for i in range(nb)] + [(t_ctx + i * dseq, dseq) for i in range(db)]
    zero_state = jnp.zeros((nb, 2, HGRN_HEADS, HGRN_DK, HGRN_DK), F32)

    x2d = jnp.concatenate([x_prompt.reshape(t_ctx, d), x_sample.reshape(t_lat, d)], axis=0)
    new_k, new_v, new_s = [], [], []
    for l in range(depth):
        z2d = _in_proj(x2d, mod[l], w_in_bf[l], rows_of_tile_fn(tm), tm)

        s0_all = jnp.concatenate([zero_state, state_hgrn[:, l]], axis=0)
        o_f, o_b, s_fin = _hgrn_scan(z2d, lb[:, l], s0_all, seq_rows, HGRN_CHUNK)
        new_s.append(s_fin[:nb])

        q_c, k_c, v_c, k_norm = _qkv_prep(z2d, 0, nb, seq, q_norm_g[l], k_norm_g[l], ones_t, zeros_t, False, True)
        o_att = _attention(q_c, k_c, v_c, None, 0, t_all)
        new_k.append(k_norm.reshape(nb, seq, KV_HEADS, HEAD_DIM))
        new_v.append(z2d[:t_ctx, COL_AV:].reshape(nb, seq, KV_HEADS, HEAD_DIM))

        q_s, k_s, v_s = _qkv_prep(z2d, t_ctx, db, dseq, q_norm_g[l], k_norm_g[l], cos_t, sin_t, True, False)
        k_s = jnp.concatenate([k_s, cache_k_bf[:, l]], axis=2)
        v_s = jnp.concatenate([v_s, cache_v_aug[:, l]], axis=2)
        o_att = _attention(q_s, k_s, v_s, o_att, t_ctx, t_all)

        x1, h2 = _merge(o_f, o_b, z2d, o_att, x2d, mod[l], w_rec_bf[l], w_att_bf[l], w_out_bf[l],
                        hgrn_norm_g[l], ln1_g[l], ln1_b[l], rows_of_tile_fn(tm_merge), tm_merge)

        x2d = _peer(h2, wq_bf[l], keys_bf[l], u_bf[l], v_bf[l], x1, mod[l], ln2_g[l], ln2_b[l],
                    rows_of_tile_fn(tb_mix), tb_mix)

    y_prompt = x2d[:t_ctx].reshape(nb, seq, d)
    y_sample = x2d[t_ctx:].reshape(db, dseq, d)
    return (y_prompt, y_sample, jnp.stack(new_k, axis=1), jnp.stack(new_v, axis=1), jnp.stack(new_s, axis=1))
```

```python
import functools
import math

import numpy as np
import jax
import jax.numpy as jnp
from jax import lax
from jax.experimental import pallas as pl
from jax.experimental.pallas import tpu as pltpu

F32 = jnp.float32
BF16 = jnp.bfloat16

D_MODEL = 1024
GRID_W = 64
HGRN_HEADS = 8
HGRN_DK = 128
HGRN_WIDTH = HGRN_HEADS * HGRN_DK
HGRN_CHUNK = 128
ATTN_HEADS = 16
KV_HEADS = 4
HEAD_DIM = 64
GROUP = ATTN_HEADS // KV_HEADS
ATTN_WIDTH = ATTN_HEADS * HEAD_DIM
KV_WIDTH = KV_HEADS * HEAD_DIM
ATTN_SCALE = HEAD_DIM ** -0.5
ROPE_THETA = 10000.0
PEER_HEADS = 8
PEER_HALF = 128
N_KEYS = 128
N_EXPERTS = N_KEYS * N_KEYS
PEER_TOPK = 16
MODEL_DEPTH = 4
DEEPNORM_ALPHA = (2 * MODEL_DEPTH) ** 0.25
EPS = 1e-6
NEG_INF = float("-inf")

IN_COLS = 5 * HGRN_WIDTH + ATTN_WIDTH + 2 * KV_WIDTH + 2 * D_MODEL
COL_AQ = 5 * HGRN_WIDTH
COL_AK = COL_AQ + ATTN_WIDTH
COL_AV = COL_AK + KV_WIDTH
COL_GA = COL_AV + KV_WIDTH
COL_GB = COL_GA + D_MODEL

VMEM_LIMIT = 56 * 1024 * 1024
PEER_VMEM_LIMIT = 60 * 1024 * 1024
LOG2_E = math.log2(math.e)
ATTN_TQ = 512
ATTN_TS = 512
PEER_TB_MIX = 512
PEER_TE = 2048


def _params(*sem):
    return pltpu.CompilerParams(dimension_semantics=sem, vmem_limit_bytes=VMEM_LIMIT)


def _tile(n, pref):
    t = min(n, pref)
    while n % t or t % 8:
        t -= 1
    return t


def _sigmoid(x):
    return 1.0 / (1.0 + jnp.exp(-x))


def _sigmoid_t(x):
    return 0.5 + 0.5 * jnp.tanh(0.5 * x)


def _silu(x):
    h = 0.5 * x
    return h + h * jnp.tanh(h)


def _bf16_floor(x):
    bits = pltpu.bitcast(x, jnp.uint32) & jnp.uint32(0xFFFF0000)
    return pltpu.bitcast(bits, F32)


def _dot(a, b):
    return jnp.dot(a, b, preferred_element_type=F32)


def _dot_nt(a, b):
    return lax.dot_general(a, b, (((1,), (1,)), ((), ())), preferred_element_type=F32)


def _mod_kernel(cond_ref, w_ref, b_ref, o_ref):
    c = cond_ref[...]
    s = c * _sigmoid(c)
    o_ref[0] = jnp.dot(s, w_ref[0], precision=lax.Precision.HIGHEST, preferred_element_type=F32) + b_ref[0]


def _modulation(cond8, w_mod, b_mod):
    depth, d, n = w_mod.shape
    tn = _tile(n, 1536)
    return pl.pallas_call(
        _mod_kernel,
        grid=(depth, n // tn),
        in_specs=[
            pl.BlockSpec((8, d), lambda l, j: (0, 0)),
            pl.BlockSpec((1, d, tn), lambda l, j: (l, 0, j)),
            pl.BlockSpec((1, 1, tn), lambda l, j: (l, 0, j)),
        ],
        out_specs=pl.BlockSpec((1, 8, tn), lambda l, j: (l, 0, j)),
        out_shape=jax.ShapeDtypeStruct((depth, 8, n), F32),
        compiler_params=_params("parallel", "parallel"),
        name="modulation",
    )(cond8, w_mod, b_mod.reshape(depth, 1, n))


def _inproj_kernel(x_ref, sh_ref, sc_ref, w_ref, z_ref, h_scr):
    @pl.when(pl.program_id(1) == 0)
    def _():
        h_scr[...] = (x_ref[...] * (1.0 + sc_ref[0]) + sh_ref[0]).astype(BF16)

    z_ref[...] = _dot(h_scr[...], w_ref[...])


def _in_proj(x2d, mod_l, w_in_bf, rows_of_tile, tm):
    t, d = x2d.shape
    n = w_in_bf.shape[1]
    tn = 512
    mod3 = mod_l.reshape(8, 1, 6 * d)
    return pl.pallas_call(
        _inproj_kernel,
        grid=(t // tm, n // tn),
        in_specs=[
            pl.BlockSpec((tm, d), lambda i, j: (i, 0)),
            pl.BlockSpec((1, 1, d), lambda i, j: (rows_of_tile(i), 0, 0)),
            pl.BlockSpec((1, 1, d), lambda i, j: (rows_of_tile(i), 0, 1)),
            pl.BlockSpec((d, tn), lambda i, j: (0, j)),
        ],
        out_specs=pl.BlockSpec((tm, tn), lambda i, j: (i, j)),
        out_shape=jax.ShapeDtypeStruct((t, n), F32),
        scratch_shapes=[pltpu.VMEM((tm, d), BF16)],
        compiler_params=_params("parallel", "arbitrary"),
        name="in_proj",
    )(x2d, mod3, mod3, w_in_bf)


def _level_reference_rows(beta, c):
    width = beta.shape[1]
    sub = lax.broadcasted_iota(jnp.int32, (8, width), 0)

    def bc(i, n=8):
        return jnp.broadcast_to(beta[i:i + 1, :], (n, width))

    out = []
    m = 1
    while m < c:
        blocks = []
        if m >= 8:
            for blk in range(c // (2 * m)):
                blocks.append(bc(blk * 2 * m + m, 2 * m))
        elif m == 4:
            for v in range(c // 8):
                blocks.append(bc(8 * v + 4))
        elif m == 2:
            for v in range(c // 8):
                blocks.append(jnp.where(sub < 4, bc(8 * v + 2), bc(8 * v + 6)))
        else:
            for v in range(c // 8):
                lo = jnp.where(sub < 2, bc(8 * v + 1), bc(8 * v + 3))
                hi = jnp.where(sub < 6, bc(8 * v + 5), bc(8 * v + 7))
                blocks.append(jnp.where(sub < 4, lo, hi))
        out.append(blocks[0] if len(blocks) == 1 else jnp.concatenate(blocks, axis=0))
        m *= 2
    return out


def _hgrn_direction(zq, zf, v, lb, st_ref, o_ref, tri, eye, q_half, pair_masks, reverse, c):
    heads = [slice(h * HGRN_DK, (h + 1) * HGRN_DK) for h in range(HGRN_HEADS)]
    q = zq * pl.reciprocal(1.0 + jnp.exp2(zq * -LOG2_E), approx=True)
    zl = zf * -LOG2_E
    e = jnp.exp2(zl)
    l2 = jnp.log(1.0 + e) * LOG2_E
    g = jnp.log(1.0 + lb * e) * LOG2_E - l2
    k = (1.0 - lb) * jnp.exp2(zl - l2)
    g1 = _bf16_floor(g)
    r1 = g - g1
    g2 = _bf16_floor(r1)
    g3 = r1 - g2
    beta = (_dot(tri, g1.astype(BF16)) + _dot(tri, g2.astype(BF16))) + _dot(tri, g3.astype(BF16))
    last = 0 if reverse else c - 1
    beta_tot = beta[last:last + 1, :]

    q_bf, k_bf, v_bf = q.astype(BF16), k.astype(BF16), v.astype(BF16)
    a = [jnp.where(eye, _dot_nt(q_bf[:, hs], k_bf[:, hs]), 0.0) for hs in heads]
    for r, qh, pm in zip(_level_reference_rows(beta, c), q_half, pair_masks):
        neg_abs = pltpu.bitcast(pltpu.bitcast(beta - r, jnp.uint32) | jnp.uint32(0x80000000), F32)
        qk = jnp.concatenate([jnp.where(qh, q[:, hs], k[:, hs]) for hs in heads], axis=1)
        u = (qk * jnp.exp2(neg_abs)).astype(BF16)
        a = [ah + jnp.where(pm, _dot_nt(u[:, hs], u[:, hs]), 0.0) for ah, hs in zip(a, heads)]

    qb = (q * jnp.exp2(beta)).astype(BF16)
    kb = (k * jnp.exp2(beta_tot - beta)).astype(BF16)
    dec = jnp.exp2(beta_tot)
    for h, hs in enumerate(heads):
        st = st_ref[h]
        o_ref[:, hs] = _dot(a[h].astype(BF16), v_bf[:, hs]) + _dot_nt(qb[:, hs], st.astype(BF16))
        st_ref[h] = st * dec[:, hs] + lax.dot_general(
            v_bf[:, hs], kb[:, hs], (((0,), (0,)), ((), ())), preferred_element_type=F32)


def _hgrn_kernel(rbf_ref, rbb_ref, first_ref, seq_ref,
                 qf_ref, ff_ref, vf_ref, qb_ref, fb_ref, vb_ref, lb_ref, s0_ref,
                 of_ref, ob_ref, sout_ref, st_scr, *, c, n_last):
    s = pl.program_id(0)

    @pl.when(first_ref[s] == 1)
    def _():
        for d in range(2):
            for h in range(HGRN_HEADS):
                st_scr[d, h] = s0_ref[0, d, h].T

    row = lax.broadcasted_iota(jnp.int32, (c, c), 0)
    col = lax.broadcasted_iota(jnp.int32, (c, c), 1)
    rowd = lax.broadcasted_iota(jnp.int32, (c, HGRN_DK), 0)
    eye = row == col

    for d, (q_ref, f_ref, v_ref, o_ref) in enumerate(
            ((qf_ref, ff_ref, vf_ref, of_ref), (qb_ref, fb_ref, vb_ref, ob_ref))):
        reverse = d == 1
        tri = (col >= row if reverse else col <= row).astype(BF16)
        q_par = 0 if reverse else 1
        q_half, pair_masks = [], []
        m = 1
        while m < c:
            q_half.append(((rowd // m) % 2) == q_par)
            pair_masks.append(((row // (2 * m)) == (col // (2 * m)))
                              & (((row // m) % 2) == q_par) & (((col // m) % 2) == 1 - q_par))
            m *= 2
        _hgrn_direction(q_ref[...], f_ref[...], v_ref[...], lb_ref[pl.ds(d, 1), :], st_scr.at[d], o_ref,
                        tri, eye, q_half, pair_masks, reverse, c)

    @pl.when(first_ref[jnp.minimum(s + 1, n_last)] == 1)
    def _():
        for d in range(2):
            for h in range(HGRN_HEADS):
                sout_ref[0, d, h] = st_scr[d, h].T


def _hgrn_scan(z2d, lb2, s0_all, seq_rows, c):
    t = z2d.shape[0]
    rbf, rbb, first, seq = [], [], [], []
    for si, (r0, length) in enumerate(seq_rows):
        n = length // c
        for ci in range(n):
            rbf.append(r0 // c + ci)
            rbb.append(r0 // c + n - 1 - ci)
            first.append(1 if ci == 0 else 0)
            seq.append(si)
    n_steps = len(rbf)
    first.append(1)
    w = HGRN_WIDTH

    def zspec(which, colblk):
        if which == 0:
            return pl.BlockSpec((c, w), lambda s, rbf, rbb, fi, sq: (rbf[s], colblk))
        return pl.BlockSpec((c, w), lambda s, rbf, rbb, fi, sq: (rbb[s], colblk))

    n_seq = len(seq_rows)
    state_spec = pl.BlockSpec((1, 2, HGRN_HEADS, HGRN_DK, HGRN_DK),
                              lambda s, rbf, rbb, fi, sq: (sq[s], 0, 0, 0, 0))
    grid_spec = pltpu.PrefetchScalarGridSpec(
        num_scalar_prefetch=4,
        grid=(n_steps,),
        in_specs=[zspec(0, 0), zspec(0, 1), zspec(0, 3), zspec(1, 0), zspec(1, 2), zspec(1, 3),
                  pl.BlockSpec((2, w), lambda s, *_: (0, 0)), state_spec],
        out_specs=[zspec(0, 0), zspec(1, 0), state_spec],
        scratch_shapes=[pltpu.VMEM((2, HGRN_HEADS, HGRN_DK, HGRN_DK), F32)],
    )
    return pl.pallas_call(
        functools.partial(_hgrn_kernel, c=c, n_last=n_steps),
        grid_spec=grid_spec,
        out_shape=[jax.ShapeDtypeStruct((t, w), F32), jax.ShapeDtypeStruct((t, w), F32),
                   jax.ShapeDtypeStruct((n_seq, 2, HGRN_HEADS, HGRN_DK, HGRN_DK), F32)],
        compiler_params=_params("arbitrary"),
        name="hgrn_scan",
    )(jnp.asarray(rbf, jnp.int32), jnp.asarray(rbb, jnp.int32), jnp.asarray(first, jnp.int32),
      jnp.asarray(seq, jnp.int32), z2d, z2d, z2d, z2d, z2d, z2d, lb2, s0_all)


def _qkv_prep_kernel(aq_ref, ak_ref, av_ref, qg_ref, kg_ref, cos_ref, sin_ref, swap_ref, avg_ref, *out_refs,
                     rope, q_scale, keep_k):
    q_out, k_out, v_out = out_refs[:3]
    hd = HEAD_DIM
    tl = aq_ref.shape[0]
    lane = lax.broadcasted_iota(jnp.int32, (tl, 2 * hd), 1)
    low = lane < hd
    ones_col = jnp.where(lane == hd, 1.0, 0.0)

    def norm_rope(pair, g):
        sq = pair * pair
        hi = _bf16_floor(sq)
        ms = _dot(hi.astype(BF16), avg_ref[...]) + _dot((sq - hi).astype(BF16), avg_ref[...])
        y = pair * lax.rsqrt(ms + EPS) * g
        if rope:
            y = y * cos_ref[...] + _dot(y.astype(BF16), swap_ref[...]) * sin_ref[...]
        return y

    for j in range(ATTN_WIDTH // (2 * hd)):
        y = norm_rope(aq_ref[:, j * 2 * hd:(j + 1) * 2 * hd], qg_ref[...]) * q_scale
        q_out[0, 2 * j] = jnp.where(low, y, 0.0).astype(BF16)
        q_out[0, 2 * j + 1] = jnp.where(low, 0.0, y).astype(BF16)
    for j in range(KV_WIDTH // (2 * hd)):
        cols = slice(j * 2 * hd, (j + 1) * 2 * hd)
        y = norm_rope(ak_ref[:, cols], kg_ref[...])
        y_other = pltpu.roll(y, hd, axis=1)
        k_out[0, 2 * j] = jnp.where(low, y, y_other).astype(BF16)
        k_out[0, 2 * j + 1] = jnp.where(low, y_other, y).astype(BF16)
        if keep_k:
            out_refs[3][:, cols] = y
        vp = av_ref[:, cols]
        v_out[0, 2 * j] = jnp.where(low, vp, ones_col).astype(BF16)
        v_out[0, 2 * j + 1] = jnp.where(low, pltpu.roll(vp, hd, axis=1), ones_col).astype(BF16)


def _qkv_prep(z2d, row0, n_seq, length, q_gain, k_gain, cos, sin, rope, keep_k):
    hd = HEAD_DIM
    tl = _tile(length, 512)
    nt = length // tl
    blk0 = row0 // tl
    swap = np.zeros((2 * hd, 2 * hd), np.float32)
    avg = np.zeros((2 * hd, 2 * hd), np.float32)
    for h0 in (0, hd):
        avg[h0:h0 + hd, h0:h0 + hd] = 1.0 / hd
        for i in range(hd):
            swap[h0 + (i + hd // 2) % hd, h0 + i] = 1.0

    def zspec(width, col):
        return pl.BlockSpec((tl, width), lambda b, j: (blk0 + b * nt + j, col // width))

    def full(shape):
        return pl.BlockSpec(shape, lambda b, j: (0,) * len(shape))

    def head_major(n_heads):
        return (pl.BlockSpec((1, n_heads, tl, 2 * hd), lambda b, j: (b, 0, j, 0)),
                jax.ShapeDtypeStruct((n_seq, n_heads, length, 2 * hd), BF16))

    out_specs, out_shape = map(list, zip(head_major(ATTN_HEADS), head_major(KV_HEADS), head_major(KV_HEADS)))
    if keep_k:
        out_specs.append(pl.BlockSpec((tl, KV_WIDTH), lambda b, j: (b * nt + j, 0)))
        out_shape.append(jax.ShapeDtypeStruct((n_seq * length, KV_WIDTH), F32))

    def both_halves(g):
        return jnp.concatenate([g, g]).reshape(1, 2 * hd)

    return pl.pallas_call(
        functools.partial(_qkv_prep_kernel, rope=rope, q_scale=ATTN_SCALE * LOG2_E, keep_k=keep_k),
        grid=(n_seq, nt),
        in_specs=[zspec(ATTN_WIDTH, COL_AQ), zspec(KV_WIDTH, COL_AK), zspec(KV_WIDTH, COL_AV),
                  full((1, 2 * hd)), full((1, 2 * hd)),
                  pl.BlockSpec((tl, 2 * hd), lambda b, j: (j, 0)), pl.BlockSpec((tl, 2 * hd), lambda b, j: (j, 0)),
                  full((2 * hd, 2 * hd)), full((2 * hd, 2 * hd))],
        out_specs=out_specs,
        out_shape=out_shape,
        compiler_params=_params("parallel", "parallel"),
        name="qkv_prep",
    )(z2d, z2d, z2d, both_halves(q_gain), both_halves(k_gain), cos, sin,
      jnp.asarray(swap, BF16), jnp.asarray(avg, BF16))


def _attn_kernel(q_ref, k_ref, v_ref, *rest, ts, n_s, tq):
    o_ref = rest[-1]
    rows = GROUP * tq
    q = q_ref[0].reshape(rows, 2 * HEAD_DIM)

    def body(i, carry):
        m, acc = carry
        sl = pl.ds(pl.multiple_of(i * ts, ts), ts)
        s = _dot_nt(q, k_ref[0, 0, sl, :])
        m_new = jnp.maximum(m, jnp.max(s, axis=-1, keepdims=True))
        p = jnp.exp2(s - m_new)
        acc = jnp.exp2(m - m_new) * acc + _dot(p.astype(BF16), v_ref[0, 0, sl, :])
        return m_new, acc

    m0 = jnp.full((rows, 1), NEG_INF, F32)
    acc0 = jnp.zeros((rows, 2 * HEAD_DIM), F32)
    m, acc = lax.fori_loop(0, n_s, body, (m0, acc0), unroll=True)
    o = acc[:, :HEAD_DIM] / acc[:, HEAD_DIM:HEAD_DIM + 1]
    for g in range(GROUP):
        o_ref[:, g * HEAD_DIM:(g + 1) * HEAD_DIM] = o[g * tq:(g + 1) * tq]


def _attention(q, k, v_aug, o_prev, row0, t_all):
    b, _, length, _ = q.shape
    hd = HEAD_DIM
    s_len = k.shape[2]
    tq = _tile(length, ATTN_TQ)
    ts = _tile(s_len, ATTN_TS)
    nq = length // tq
    blk0 = row0 // tq
    in_specs = [
        pl.BlockSpec((1, GROUP, tq, 2 * hd), lambda bi, hi, qi: (bi, hi, qi, 0)),
        pl.BlockSpec((1, 1, s_len, 2 * hd), lambda bi, hi, qi: (bi, hi, 0, 0)),
        pl.BlockSpec((1, 1, s_len, 2 * hd), lambda bi, hi, qi: (bi, hi, 0, 0)),
    ]
    args = [q, k, v_aug]
    aliases = {}
    if o_prev is not None:
        in_specs.append(pl.BlockSpec(memory_space=pl.ANY))
        args.append(o_prev)
        aliases = {3: 0}
    return pl.pallas_call(
        functools.partial(_attn_kernel, ts=ts, n_s=s_len // ts, tq=tq),
        grid=(b, KV_HEADS, nq),
        in_specs=in_specs,
        out_specs=pl.BlockSpec((tq, GROUP * hd), lambda bi, hi, qi: (blk0 + bi * nq + qi, hi)),
        out_shape=jax.ShapeDtypeStruct((t_all, ATTN_WIDTH), F32),
        input_output_aliases=aliases,
        compiler_params=_params("parallel", "parallel", "parallel"),
        name="attention",
    )(*args)


def _layer_norm(t, g, b):
    mu = jnp.mean(t, axis=-1, keepdims=True)
    tc = t - mu
    var = jnp.mean(tc * tc, axis=-1, keepdims=True)
    return tc * lax.rsqrt(var + EPS) * g + b


def _merge_kernel(of_ref, ob_ref, hg_ref, ga0_ref, ga1_ref, gb0_ref, gb1_ref, oatt_ref, x_ref,
                  g1_ref, sh2_ref, sc2_ref, wrec_ref, watt_ref, wout_ref, ng_ref, lng_ref, lnb_ref,
                  x1_ref, h2_ref, orec_scr):
    for h in range(HGRN_HEADS):
        cs = slice(h * HGRN_DK, (h + 1) * HGRN_DK)
        o = of_ref[:, cs] + ob_ref[:, cs]
        ms = jnp.mean(o * o, axis=-1, keepdims=True)
        zg = hg_ref[:, cs]
        orec_scr[:, cs] = (o * lax.rsqrt(ms + EPS) * ng_ref[...] * _silu(zg)).astype(BF16)
    ga = jnp.concatenate([ga0_ref[...], ga1_ref[...]], axis=1)
    gb = jnp.concatenate([gb0_ref[...], gb1_ref[...]], axis=1)
    y = (_sigmoid_t(ga) * _dot(orec_scr[...], wrec_ref[...])
         + _sigmoid_t(gb) * _dot(oatt_ref[...].astype(BF16), watt_ref[...]))
    u = _dot(y.astype(BF16), wout_ref[...])
    x1 = _layer_norm(DEEPNORM_ALPHA * x_ref[...] + g1_ref[0] * u, lng_ref[...], lnb_ref[...])
    x1_ref[...] = x1
    h2_ref[...] = x1 * (1.0 + sc2_ref[0]) + sh2_ref[0]


def _merge(o_f, o_b, z2d, o_att, x2d, mod_l, w_rec, w_att, w_out, norm_g, ln_g, ln_b, rows_of_tile, tm):
    t, d = x2d.shape
    mod3 = mod_l.reshape(8, 1, 6 * d)

    def rows(colblk):
        return pl.BlockSpec((tm, d), lambda i: (i, colblk))

    def half_rows(col, part):
        return pl.BlockSpec((tm, d // 2), lambda i: (i, col // (d // 2) + part))

    def modspec(chunk):
        return pl.BlockSpec((1, 1, d), lambda i: (rows_of_tile(i), 0, chunk))

    def full(shape):
        return pl.BlockSpec(shape, lambda i: (0,) * len(shape))

    def weight():
        return pl.BlockSpec((d, d), lambda i: (0, 0), pipeline_mode=pl.Buffered(1))

    return pl.pallas_call(
        _merge_kernel,
        grid=(t // tm,),
        in_specs=[rows(0), rows(0), rows(4), half_rows(COL_GA, 0), half_rows(COL_GA, 1),
                  half_rows(COL_GB, 0), half_rows(COL_GB, 1), rows(0), rows(0),
                  modspec(2), modspec(3), modspec(4),
                  weight(), weight(), weight(), full((1, HGRN_DK)), full((1, d)), full((1, d))],
        out_specs=[rows(0), rows(0)],
        out_shape=[jax.ShapeDtypeStruct((t, d), F32), jax.ShapeDtypeStruct((t, d), F32)],
        scratch_shapes=[pltpu.VMEM((tm, d), BF16)],
        compiler_params=_params("parallel"),
        name="merge",
    )(o_f, o_b, z2d, z2d, z2d, z2d, z2d, o_att, x2d, mod3, mod3, mod3, w_rec, w_att, w_out,
      norm_g.reshape(1, HGRN_DK), ln_g.reshape(1, d), ln_b.reshape(1, d))


def _top16(s, n):
    iota = lax.broadcasted_iota(jnp.int32, s.shape, 0).astype(F32)
    vals, idxs = [], []
    for _ in range(PEER_TOPK):
        m = jnp.max(s, axis=0, keepdims=True)
        idx = jnp.min(jnp.where(s == m, iota, float(n)), axis=0, keepdims=True)
        vals.append(m)
        idxs.append(idx)
        s = jnp.where(iota == idx, NEG_INF, s)
    return vals, idxs


def _candidate_positions():
    rows = []
    for a in range(4):
        for b in range(16 if a == 0 else 8):
            rows.append((a, b, (a + 1) * (b + 1) <= PEER_TOPK))
    for b, n_a in ((0, 16), (1, 8), (2, 8)):
        for a in range(n_a):
            rows.append((a, b, a >= 4 and (a + 1) * (b + 1) <= PEER_TOPK))
    return np.array([[a * PEER_TOPK + b if ok else -1.0] for a, b, ok in rows], np.float32)


def _peer_head_topk(qh, keys0, keys1, posb):
    k = PEER_TOPK
    tops = []
    for p, keys in enumerate((keys0, keys1)):
        st = _dot_nt(keys, qh[:, p * PEER_HALF:(p + 1) * PEER_HALF].astype(BF16))
        tops.append(_top16(st, N_KEYS))
    (v0, i0), (v1, i1) = tops
    v0c = jnp.concatenate(v0, axis=0)
    v1c = jnp.concatenate(v1, axis=0)
    cand = jnp.concatenate([v0[0] + v1c, v0[1] + v1c[:8], v0[2] + v1c[:8], v0[3] + v1c[:8],
                            v0c + v1[0], v0c[:8] + v1[1], v0c[:8] + v1[2]], axis=0)
    cand = jnp.where(posb >= 0.0, cand, NEG_INF)
    best, poss = [], []
    for _ in range(k):
        m = jnp.max(cand, axis=0, keepdims=True)
        pos = jnp.min(jnp.where(cand == m, posb, float(k * k)), axis=0, keepdims=True)
        best.append(m)
        poss.append(pos)
        cand = jnp.where(posb == pos, NEG_INF, cand)
    best = jnp.concatenate(best, axis=0)
    posc = jnp.concatenate(poss, axis=0)
    a_rank = jnp.floor(posc * (1.0 / k))
    b_rank = posc - k * a_rank
    ea = jnp.zeros_like(posc)
    eb = jnp.zeros_like(posc)
    for r in range(k):
        ea = jnp.where(a_rank == float(r), i0[r], ea)
        eb = jnp.where(b_rank == float(r), i1[r], eb)
    e = jnp.exp(best - best[0:1, :])
    return ea, eb, e / jnp.sum(e, axis=0, keepdims=True)


RSQRT2 = 1.0 / math.sqrt(2.0)


def _peer_kernel(h2n_ref, wq_ref, keys_ref, pos_ref, u_ref, v_ref, x1_ref, g2_ref, lng_ref, lnb_ref,
                 x2_ref, ids_scr, idt_scr, w_scr, p_scr, acc_scr, h2n_scr, h2_scr, *, tb, te, pitch, nblk):
    i = pl.program_id(0)
    j = pl.program_id(1)
    slot = i % 2
    k = PEER_TOPK
    rows_per_tile = te // N_KEYS
    half = N_KEYS // 2
    hi_mask = jnp.uint32(0xFFFF0000)
    first_row = i == 0
    last_row = i == nblk

    @pl.when(jnp.logical_and(j == 0, jnp.logical_not(first_row)))
    def _():
        h2_scr[...] = h2n_scr[...]
        for n in range(3):
            idt_scr[n] = ids_scr[1 - slot, n].T
        acc_scr[...] = jnp.zeros_like(acc_scr)
        iota = lax.broadcasted_iota(jnp.int32, (N_KEYS, PEER_HEADS * k), 0).astype(F32)

        def tok(t, carry):
            a_row = idt_scr[0, pl.ds(t, 1), :]
            b_row = idt_scr[1, pl.ds(t, 1), :]
            g_row = idt_scr[2, pl.ds(t, 1), :] * RSQRT2
            at = jnp.where(iota == a_row, 1.0, 0.0).astype(BF16)
            gbt = jnp.where(iota == b_row, g_row, 0.0).astype(BF16)
            w = pltpu.bitcast(_dot_nt(at, gbt), jnp.uint32)
            w_scr[pl.ds(t, half, stride=pitch), :] = (w[:half] & hi_mask) | (w[half:] >> 16)
            return carry

        lax.fori_loop(0, tb, tok, 0, unroll=64)

    @pl.when(jnp.logical_and(j == 0, jnp.logical_not(last_row)))
    def _():
        h2n_scr[...] = h2n_ref[...].astype(BF16)

    def retrieve():
        posb = jnp.broadcast_to(pos_ref[...], (pos_ref.shape[0], tb))
        ea, eb, gate = _peer_head_topk(_dot(h2n_scr[...], wq_ref[...]), keys_ref[0, 0], keys_ref[0, 1], posb)
        rows = pl.ds(pl.multiple_of(j * k, k), k)
        ids_scr[slot, 0, rows, :] = ea
        ids_scr[slot, 1, rows, :] = eb
        ids_scr[slot, 2, rows, :] = gate

    def mix():
        a = _dot_nt(h2_scr[...], u_ref[...])
        for r in range(rows_per_tile):
            i1 = j * rows_per_tile + r
            packed = w_scr[pl.ds(pl.multiple_of((i1 % half) * pitch, 8), tb), :]
            shift = jnp.where(i1 >= half, 16, 0).astype(jnp.uint32)
            w = pltpu.bitcast((packed << shift) & hi_mask, F32)
            cs = slice(r * N_KEYS, (r + 1) * N_KEYS)
            ar = a[:, cs]
            p_scr[:, cs] = (w * (ar * (1.0 + lax.erf(ar)))).astype(BF16)
        acc_scr[...] += _dot(p_scr[...], v_ref[...])

    @pl.when(first_row)
    def _():
        retrieve()

    @pl.when(last_row)
    def _():
        mix()

    @pl.when(jnp.logical_not(jnp.logical_or(first_row, last_row)))
    def _():
        retrieve()
        mix()

    @pl.when(jnp.logical_and(j == pl.num_programs(1) - 1, jnp.logical_not(first_row)))
    def _():
        x2_ref[...] = _layer_norm(DEEPNORM_ALPHA * x1_ref[...] + g2_ref[0] * acc_scr[...],
                                  lng_ref[...], lnb_ref[...])


def _peer(h2, wq_bf, keys_bf, u_bf, v_bf, x1, mod_l, ln_g, ln_b, rows_of_tile, tb):
    t, d = x1.shape
    n_e = v_bf.shape[0]
    te = PEER_TE
    assert n_e // te == PEER_HEADS
    hk = PEER_HEADS * PEER_TOPK
    pitch = tb + 8
    nblk = t // tb
    mod3 = mod_l.reshape(8, 1, 6 * d)
    pos = _candidate_positions()

    def cur(i):
        return jnp.maximum(i - 1, 0)

    def full(shape):
        return pl.BlockSpec(shape, lambda i, j: (0,) * len(shape))

    return pl.pallas_call(
        functools.partial(_peer_kernel, tb=tb, te=te, pitch=pitch, nblk=nblk),
        grid=(nblk + 1, PEER_HEADS),
        in_specs=[
            pl.BlockSpec((tb, d), lambda i, j: (jnp.minimum(i, nblk - 1), 0)),
            pl.BlockSpec((d, 2 * PEER_HALF), lambda i, j: (0, j)),
            pl.BlockSpec((1, 2, N_KEYS, PEER_HALF), lambda i, j: (j, 0, 0, 0)),
            full(pos.shape),
            pl.BlockSpec((te, d), lambda i, j: (j, 0)),
            pl.BlockSpec((te, d), lambda i, j: (j, 0)),
            pl.BlockSpec((tb, d), lambda i, j: (cur(i), 0), pipeline_mode=pl.Buffered(1)),
            pl.BlockSpec((1, 1, d), lambda i, j: (rows_of_tile(cur(i)), 0, 5)),
            full((1, d)), full((1, d)),
        ],
        out_specs=pl.BlockSpec((tb, d), lambda i, j: (cur(i), 0)),
        out_shape=jax.ShapeDtypeStruct((t, d), F32),
        scratch_shapes=[pltpu.VMEM((2, 3, hk, tb), F32), pltpu.VMEM((3, tb, hk), F32),
                        pltpu.VMEM((pitch * N_KEYS // 2, N_KEYS), jnp.uint32), pltpu.VMEM((tb, te), BF16),
                        pltpu.VMEM((tb, d), F32), pltpu.VMEM((tb, d), BF16), pltpu.VMEM((tb, d), BF16)],
        compiler_params=pltpu.CompilerParams(dimension_semantics=("arbitrary", "arbitrary"),
                                             vmem_limit_bytes=PEER_VMEM_LIMIT),
        name="peer",
    )(h2, wq_bf, keys_bf, jnp.asarray(pos), u_bf, v_bf, x1, mod3, ln_g.reshape(1, d), ln_b.reshape(1, d))


def _rope_tables(length):
    rows = length // GRID_W
    row = jnp.broadcast_to(jnp.arange(rows, dtype=F32)[:, None], (rows, GRID_W)).reshape(length)
    col = jnp.broadcast_to(jnp.arange(GRID_W, dtype=F32)[None, :], (rows, GRID_W)).reshape(length)
    n_freq = HEAD_DIM // 4
    inv = jnp.power(ROPE_THETA, -jnp.arange(n_freq, dtype=F32) / n_freq)
    ang = jnp.concatenate([row[:, None] * inv, col[:, None] * inv], axis=-1)
    cos, sin = jnp.cos(ang), jnp.sin(ang)
    return jnp.concatenate([cos, cos], axis=-1), jnp.concatenate([-sin, sin], axis=-1)


def kernel(x_prompt, x_sample, cache_attn_k, cache_attn_v, state_hgrn, c, c_ctx, w_mod, b_mod, w_in,
           hgrn_lb_logits, hgrn_norm_g, q_norm_g, k_norm_g, w_branch_rec, w_branch_att, w_out,
           ln1_g, ln1_b, ln2_g, ln2_b, peer_w_query, peer_sub_keys, peer_u, peer_v):
    nb, seq, d = x_prompt.shape
    db, dseq, _ = x_sample.shape
    depth = w_in.shape[0]
    past = cache_attn_k.shape[2]
    t_ctx, t_lat = nb * seq, db * dseq
    t_all = t_ctx + t_lat
    assert d == D_MODEL and 1 + db <= 8

    tm = _tile(math.gcd(t_ctx, dseq), 2048)
    tm_merge = _tile(math.gcd(t_ctx, dseq), 512)
    tb_mix = _tile(math.gcd(t_ctx, dseq), PEER_TB_MIX)

    def rows_of_tile_fn(tile):
        n_ctx = t_ctx // tile
        per = dseq // tile
        return lambda i: jnp.where(i < n_ctx, 0, 1 + (i - n_ctx) // per)

    cond8 = jnp.concatenate([c_ctx[None, :], c, jnp.zeros((8 - 1 - db, d), F32)], axis=0)
    mod = _modulation(cond8, w_mod, b_mod)

    lb = jnp.cumsum(jax.nn.softmax(hgrn_lb_logits.astype(F32), axis=1), axis=1)
    lb = lb - lb[:, :1]

    w_in_bf = w_in.astype(BF16)
    w_rec_bf, w_att_bf, w_out_bf = (w.astype(BF16) for w in (w_branch_rec, w_branch_att, w_out))
    wq_bf = peer_w_query.astype(BF16)
    keys_bf = peer_sub_keys.astype(BF16)
    u_bf = (peer_u * RSQRT2).astype(BF16)
    v_bf = peer_v.astype(BF16)

    cos_t, sin_t = (jnp.concatenate([t, t], axis=-1) for t in _rope_tables(dseq))
    ones_t, zeros_t = jnp.ones((seq, 2 * HEAD_DIM), F32), jnp.zeros((seq, 2 * HEAD_DIM), F32)
    cache_k_t = jnp.transpose(cache_attn_k, (0, 1, 3, 2, 4))
    cache_k_bf = jnp.concatenate([cache_k_t, cache_k_t], axis=-1).astype(BF16)
    cache_v_t = jnp.transpose(cache_attn_v, (0, 1, 3, 2, 4))
    ones_pad = jnp.zeros(cache_v_t.shape, F32).at[..., 0].set(1.0)
    cache_v_aug = jnp.concatenate([cache_v_t, ones_pad], axis=-1).astype(BF16)

    seq_rows = [(i * seq, seq) for i in range(nb)] + [(t_ctx + i * dseq, dseq) for i in range(db)]
    zero_state = jnp.zeros((nb, 2, HGRN_HEADS, HGRN_DK, HGRN_DK), F32)

    x2d = jnp.concatenate([x_prompt.reshape(t_ctx, d), x_sample.reshape(t_lat, d)], axis=0)
    new_k, new_v, new_s = [], [], []
    for l in range(depth):
        z2d = _in_proj(x2d, mod[l], w_in_bf[l], rows_of_tile_fn(tm), tm)

        s0_all = jnp.concatenate([zero_state, state_hgrn[:, l]], axis=0)
        o_f, o_b, s_fin = _hgrn_scan(z2d, lb[:, l], s0_all, seq_rows, HGRN_CHUNK)
        new_s.append(s_fin[:nb])

        q_c, k_c, v_c, k_norm = _qkv_prep(z2d, 0, nb, seq, q_norm_g[l], k_norm_g[l], ones_t, zeros_t, False, True)
        o_att = _attention(q_c, k_c, v_c, None, 0, t_all)
        new_k.append(k_norm.reshape(nb, seq, KV_HEADS, HEAD_DIM))
        new_v.append(z2d[:t_ctx, COL_AV:COL_AV + KV_WIDTH].reshape(nb, seq, KV_HEADS, HEAD_DIM))

        q_s, k_s, v_s = _qkv_prep(z2d, t_ctx, db, dseq, q_norm_g[l], k_norm_g[l], cos_t, sin_t, True, False)
        k_s = jnp.concatenate([k_s, cache_k_bf[:, l]], axis=2)
        v_s = jnp.concatenate([v_s, cache_v_aug[:, l]], axis=2)
        o_att = _attention(q_s, k_s, v_s, o_att, t_ctx, t_all)

        x1, h2 = _merge(o_f, o_b, z2d, o_att, x2d, mod[l], w_rec_bf[l], w_att_bf[l], w_out_bf[l],
                        hgrn_norm_g[l], ln1_g[l], ln1_b[l], rows_of_tile_fn(tm_merge), tm_merge)

        x2d = _peer(h2, wq_bf[l], keys_bf[l], u_bf[l], v_bf[l], x1, mod[l], ln2_g[l], ln2_b[l],
                    rows_of_tile_fn(tb_mix), tb_mix)

    y_prompt = x2d[:t_ctx].reshape(nb, seq, d)
    y_sample = x2d[t_ctx:].reshape(db, dseq, d)
    return (y_prompt, y_sample, jnp.stack(new_k, axis=1), jnp.stack(new_v, axis=1), jnp.stack(new_s, axis=1))
```

```python
import functools
import math

import numpy as np
import jax
import jax.numpy as jnp
from jax import lax
from jax.experimental import pallas as pl
from jax.experimental.pallas import tpu as pltpu

F32 = jnp.float32
BF16 = jnp.bfloat16

D_MODEL = 1024
GRID_W = 64
HGRN_HEADS = 8
HGRN_DK = 128
HGRN_WIDTH = HGRN_HEADS * HGRN_DK
HGRN_CHUNK = 128
ATTN_HEADS = 16
KV_HEADS = 4
HEAD_DIM = 64
GROUP = ATTN_HEADS // KV_HEADS
ATTN_WIDTH = ATTN_HEADS * HEAD_DIM
KV_WIDTH = KV_HEADS * HEAD_DIM
ATTN_SCALE = HEAD_DIM ** -0.5
ROPE_THETA = 10000.0
PEER_HEADS = 8
PEER_HALF = 128
N_KEYS = 128
N_EXPERTS = N_KEYS * N_KEYS
PEER_TOPK = 16
MODEL_DEPTH = 4
DEEPNORM_ALPHA = (2 * MODEL_DEPTH) ** 0.25
EPS = 1e-6
NEG_INF = float("-inf")

IN_COLS = 5 * HGRN_WIDTH + ATTN_WIDTH + 2 * KV_WIDTH + 2 * D_MODEL
COL_AQ = 5 * HGRN_WIDTH
COL_AK = COL_AQ + ATTN_WIDTH
COL_AV = COL_AK + KV_WIDTH
COL_GA = COL_AV + KV_WIDTH
COL_GB = COL_GA + D_MODEL

VMEM_LIMIT = 56 * 1024 * 1024
PEER_VMEM_LIMIT = 60 * 1024 * 1024
LOG2_E = math.log2(math.e)
ATTN_TQ = 512
ATTN_TS = 512
PEER_TB_MIX = 512
PEER_TE = 2048


def _params(*sem):
    return pltpu.CompilerParams(dimension_semantics=sem, vmem_limit_bytes=VMEM_LIMIT)


def _tile(n, pref):
    t = min(n, pref)
    while n % t or t % 8:
        t -= 1
    return t


def _sigmoid(x):
    return 1.0 / (1.0 + jnp.exp(-x))


def _sigmoid_t(x):
    return 0.5 + 0.5 * jnp.tanh(0.5 * x)


def _silu(x):
    h = 0.5 * x
    return h + h * jnp.tanh(h)


def _bf16_floor(x):
    bits = pltpu.bitcast(x, jnp.uint32) & jnp.uint32(0xFFFF0000)
    return pltpu.bitcast(bits, F32)


def _dot(a, b):
    return jnp.dot(a, b, preferred_element_type=F32)


def _dot_nt(a, b):
    return lax.dot_general(a, b, (((1,), (1,)), ((), ())), preferred_element_type=F32)


def _mod_kernel(cond_ref, w_ref, b_ref, o_ref):
    c = cond_ref[...]
    s = c * _sigmoid(c)
    o_ref[0] = jnp.dot(s, w_ref[0], precision=lax.Precision.HIGHEST, preferred_element_type=F32) + b_ref[0]


def _modulation(cond8, w_mod, b_mod):
    depth, d, n = w_mod.shape
    tn = _tile(n, 1536)
    return pl.pallas_call(
        _mod_kernel,
        grid=(depth, n // tn),
        in_specs=[
            pl.BlockSpec((8, d), lambda l, j: (0, 0)),
            pl.BlockSpec((1, d, tn), lambda l, j: (l, 0, j)),
            pl.BlockSpec((1, 1, tn), lambda l, j: (l, 0, j)),
        ],
        out_specs=pl.BlockSpec((1, 8, tn), lambda l, j: (l, 0, j)),
        out_shape=jax.ShapeDtypeStruct((depth, 8, n), F32),
        compiler_params=_params("parallel", "parallel"),
        name="modulation",
    )(cond8, w_mod, b_mod.reshape(depth, 1, n))


def _inproj_kernel(x_ref, sh_ref, sc_ref, w_ref, z_ref, h_scr):
    @pl.when(pl.program_id(1) == 0)
    def _():
        h_scr[...] = (x_ref[...] * (1.0 + sc_ref[0]) + sh_ref[0]).astype(BF16)

    z_ref[...] = _dot(h_scr[...], w_ref[...])


def _in_proj(x2d, mod_l, w_in_bf, rows_of_tile, tm):
    t, d = x2d.shape
    n = w_in_bf.shape[1]
    tn = 512
    mod3 = mod_l.reshape(8, 1, 6 * d)
    return pl.pallas_call(
        _inproj_kernel,
        grid=(t // tm, n // tn),
        in_specs=[
            pl.BlockSpec((tm, d), lambda i, j: (i, 0)),
            pl.BlockSpec((1, 1, d), lambda i, j: (rows_of_tile(i), 0, 0)),
            pl.BlockSpec((1, 1, d), lambda i, j: (rows_of_tile(i), 0, 1)),
            pl.BlockSpec((d, tn), lambda i, j: (0, j)),
        ],
        out_specs=pl.BlockSpec((tm, tn), lambda i, j: (i, j)),
        out_shape=jax.ShapeDtypeStruct((t, n), F32),
        scratch_shapes=[pltpu.VMEM((tm, d), BF16)],
        compiler_params=_params("parallel", "arbitrary"),
        name="in_proj",
    )(x2d, mod3, mod3, w_in_bf)


def _level_reference_rows(beta, c):
    width = beta.shape[1]
    sub = lax.broadcasted_iota(jnp.int32, (8, width), 0)

    def bc(i, n=8):
        return jnp.broadcast_to(beta[i:i + 1, :], (n, width))

    out = []
    m = 1
    while m < c:
        blocks = []
        if m >= 8:
            for blk in range(c // (2 * m)):
                blocks.append(bc(blk * 2 * m + m, 2 * m))
        elif m == 4:
            for v in range(c // 8):
                blocks.append(bc(8 * v + 4))
        elif m == 2:
            for v in range(c // 8):
                blocks.append(jnp.where(sub < 4, bc(8 * v + 2), bc(8 * v + 6)))
        else:
            for v in range(c // 8):
                lo = jnp.where(sub < 2, bc(8 * v + 1), bc(8 * v + 3))
                hi = jnp.where(sub < 6, bc(8 * v + 5), bc(8 * v + 7))
                blocks.append(jnp.where(sub < 4, lo, hi))
        out.append(blocks[0] if len(blocks) == 1 else jnp.concatenate(blocks, axis=0))
        m *= 2
    return out


def _hgrn_direction(zq, zf, v, lb, st_ref, o_ref, tri, eye, q_half, pair_masks, reverse, c):
    heads = [slice(h * HGRN_DK, (h + 1) * HGRN_DK) for h in range(HGRN_HEADS)]
    q = zq * pl.reciprocal(1.0 + jnp.exp2(zq * -LOG2_E), approx=True)
    zl = zf * -LOG2_E
    e = jnp.exp2(zl)
    l2 = jnp.log(1.0 + e) * LOG2_E
    g = jnp.log(1.0 + lb * e) * LOG2_E - l2
    k = (1.0 - lb) * jnp.exp2(zl - l2)
    g1 = _bf16_floor(g)
    r1 = g - g1
    g2 = _bf16_floor(r1)
    g3 = r1 - g2
    beta = (_dot(tri, g1.astype(BF16)) + _dot(tri, g2.astype(BF16))) + _dot(tri, g3.astype(BF16))
    last = 0 if reverse else c - 1
    beta_tot = beta[last:last + 1, :]

    q_bf, k_bf, v_bf = q.astype(BF16), k.astype(BF16), v.astype(BF16)
    a = [jnp.where(eye, _dot_nt(q_bf[:, hs], k_bf[:, hs]), 0.0) for hs in heads]
    for r, qh, pm in zip(_level_reference_rows(beta, c), q_half, pair_masks):
        neg_abs = pltpu.bitcast(pltpu.bitcast(beta - r, jnp.uint32) | jnp.uint32(0x80000000), F32)
        qk = jnp.concatenate([jnp.where(qh, q[:, hs], k[:, hs]) for hs in heads], axis=1)
        u = (qk * jnp.exp2(neg_abs)).astype(BF16)
        a = [ah + jnp.where(pm, _dot_nt(u[:, hs], u[:, hs]), 0.0) for ah, hs in zip(a, heads)]

    qb = (q * jnp.exp2(beta)).astype(BF16)
    kb = (k * jnp.exp2(beta_tot - beta)).astype(BF16)
    dec = jnp.exp2(beta_tot)
    for h, hs in enumerate(heads):
        st = st_ref[h]
        o_ref[:, hs] = _dot(a[h].astype(BF16), v_bf[:, hs]) + _dot_nt(qb[:, hs], st.astype(BF16))
        st_ref[h] = st * dec[:, hs] + lax.dot_general(
            v_bf[:, hs], kb[:, hs], (((0,), (0,)), ((), ())), preferred_element_type=F32)


def _hgrn_kernel(rbf_ref, rbb_ref, first_ref, seq_ref,
                 qf_ref, ff_ref, vf_ref, qb_ref, fb_ref, vb_ref, lb_ref, s0_ref,
                 of_ref, ob_ref, sout_ref, st_scr, *, c, n_last):
    s = pl.program_id(0)

    @pl.when(first_ref[s] == 1)
    def _():
        for d in range(2):
            for h in range(HGRN_HEADS):
                st_scr[d, h] = s0_ref[0, d, h].T

    row = lax.broadcasted_iota(jnp.int32, (c, c), 0)
    col = lax.broadcasted_iota(jnp.int32, (c, c), 1)
    rowd = lax.broadcasted_iota(jnp.int32, (c, HGRN_DK), 0)
    eye = row == col

    for d, (q_ref, f_ref, v_ref, o_ref) in enumerate(
            ((qf_ref, ff_ref, vf_ref, of_ref), (qb_ref, fb_ref, vb_ref, ob_ref))):
        reverse = d == 1
        tri = (col >= row if reverse else col <= row).astype(BF16)
        q_par = 0 if reverse else 1
        q_half, pair_masks = [], []
        m = 1
        while m < c:
            q_half.append(((rowd // m) % 2) == q_par)
            pair_masks.append(((row // (2 * m)) == (col // (2 * m)))
                              & (((row // m) % 2) == q_par) & (((col // m) % 2) == 1 - q_par))
            m *= 2
        _hgrn_direction(q_ref[...], f_ref[...], v_ref[...], lb_ref[pl.ds(d, 1), :], st_scr.at[d], o_ref,
                        tri, eye, q_half, pair_masks, reverse, c)

    @pl.when(first_ref[jnp.minimum(s + 1, n_last)] == 1)
    def _():
        for d in range(2):
            for h in range(HGRN_HEADS):
                sout_ref[0, d, h] = st_scr[d, h].T


def _hgrn_scan(z2d, lb2, s0_all, seq_rows, c):
    t = z2d.shape[0]
    rbf, rbb, first, seq = [], [], [], []
    for si, (r0, length) in enumerate(seq_rows):
        n = length // c
        for ci in range(n):
            rbf.append(r0 // c + ci)
            rbb.append(r0 // c + n - 1 - ci)
            first.append(1 if ci == 0 else 0)
            seq.append(si)
    n_steps = len(rbf)
    first.append(1)
    w = HGRN_WIDTH

    def zspec(which, colblk):
        if which == 0:
            return pl.BlockSpec((c, w), lambda s, rbf, rbb, fi, sq: (rbf[s], colblk))
        return pl.BlockSpec((c, w), lambda s, rbf, rbb, fi, sq: (rbb[s], colblk))

    n_seq = len(seq_rows)
    state_spec = pl.BlockSpec((1, 2, HGRN_HEADS, HGRN_DK, HGRN_DK),
                              lambda s, rbf, rbb, fi, sq: (sq[s], 0, 0, 0, 0))
    grid_spec = pltpu.PrefetchScalarGridSpec(
        num_scalar_prefetch=4,
        grid=(n_steps,),
        in_specs=[zspec(0, 0), zspec(0, 1), zspec(0, 3), zspec(1, 0), zspec(1, 2), zspec(1, 3),
                  pl.BlockSpec((2, w), lambda s, *_: (0, 0)), state_spec],
        out_specs=[zspec(0, 0), zspec(1, 0), state_spec],
        scratch_shapes=[pltpu.VMEM((2, HGRN_HEADS, HGRN_DK, HGRN_DK), F32)],
    )
    return pl.pallas_call(
        functools.partial(_hgrn_kernel, c=c, n_last=n_steps),
        grid_spec=grid_spec,
        out_shape=[jax.ShapeDtypeStruct((t, w), F32), jax.ShapeDtypeStruct((t, w), F32),
                   jax.ShapeDtypeStruct((n_seq, 2, HGRN_HEADS, HGRN_DK, HGRN_DK), F32)],
        compiler_params=_params("arbitrary"),
        name="hgrn_scan",
    )(jnp.asarray(rbf, jnp.int32), jnp.asarray(rbb, jnp.int32), jnp.asarray(first, jnp.int32),
      jnp.asarray(seq, jnp.int32), z2d, z2d, z2d, z2d, z2d, z2d, lb2, s0_all)


def _qkv_prep_kernel(aq_ref, ak_ref, av_ref, qg_ref, kg_ref, cos_ref, sin_ref, swap_ref, avg_ref, *out_refs,
                     rope, q_scale, keep_k):
    q_out, k_out, v_out = out_refs[:3]
    hd = HEAD_DIM
    tl = aq_ref.shape[0]
    lane = lax.broadcasted_iota(jnp.int32, (tl, 2 * hd), 1)
    low = lane < hd
    ones_col = jnp.where(lane == hd, 1.0, 0.0)

    def norm_rope(pair, g):
        sq = pair * pair
        hi = _bf16_floor(sq)
        ms = _dot(hi.astype(BF16), avg_ref[...]) + _dot((sq - hi).astype(BF16), avg_ref[...])
        y = pair * lax.rsqrt(ms + EPS) * g
        if rope:
            y = y * cos_ref[...] + _dot(y.astype(BF16), swap_ref[...]) * sin_ref[...]
        return y

    for j in range(ATTN_WIDTH // (2 * hd)):
        y = norm_rope(aq_ref[:, j * 2 * hd:(j + 1) * 2 * hd], qg_ref[...]) * q_scale
        q_out[0, 2 * j] = jnp.where(low, y, 0.0).astype(BF16)
        q_out[0, 2 * j + 1] = jnp.where(low, 0.0, y).astype(BF16)
    for j in range(KV_WIDTH // (2 * hd)):
        cols = slice(j * 2 * hd, (j + 1) * 2 * hd)
        y = norm_rope(ak_ref[:, cols], kg_ref[...])
        y_other = pltpu.roll(y, hd, axis=1)
        k_out[0, 2 * j] = jnp.where(low, y, y_other).astype(BF16)
        k_out[0, 2 * j + 1] = jnp.where(low, y_other, y).astype(BF16)
        if keep_k:
            out_refs[3][:, cols] = y
        vp = av_ref[:, cols]
        v_out[0, 2 * j] = jnp.where(low, vp, ones_col).astype(BF16)
        v_out[0, 2 * j + 1] = jnp.where(low, pltpu.roll(vp, hd, axis=1), ones_col).astype(BF16)


def _qkv_prep(z2d, row0, n_seq, length, q_gain, k_gain, cos, sin, rope, keep_k):
    hd = HEAD_DIM
    tl = _tile(length, 512)
    nt = length // tl
    blk0 = row0 // tl
    swap = np.zeros((2 * hd, 2 * hd), np.float32)
    avg = np.zeros((2 * hd, 2 * hd), np.float32)
    for h0 in (0, hd):
        avg[h0:h0 + hd, h0:h0 + hd] = 1.0 / hd
        for i in range(hd):
            swap[h0 + (i + hd // 2) % hd, h0 + i] = 1.0

    def zspec(width, col):
        return pl.BlockSpec((tl, width), lambda b, j: (blk0 + b * nt + j, col // width))

    def full(shape):
        return pl.BlockSpec(shape, lambda b, j: (0,) * len(shape))

    def head_major(n_heads):
        return (pl.BlockSpec((1, n_heads, tl, 2 * hd), lambda b, j: (b, 0, j, 0)),
                jax.ShapeDtypeStruct((n_seq, n_heads, length, 2 * hd), BF16))

    out_specs, out_shape = map(list, zip(head_major(ATTN_HEADS), head_major(KV_HEADS), head_major(KV_HEADS)))
    if keep_k:
        out_specs.append(pl.BlockSpec((tl, KV_WIDTH), lambda b, j: (b * nt + j, 0)))
        out_shape.append(jax.ShapeDtypeStruct((n_seq * length, KV_WIDTH), F32))

    def both_halves(g):
        return jnp.concatenate([g, g]).reshape(1, 2 * hd)

    return pl.pallas_call(
        functools.partial(_qkv_prep_kernel, rope=rope, q_scale=ATTN_SCALE * LOG2_E, keep_k=keep_k),
        grid=(n_seq, nt),
        in_specs=[zspec(ATTN_WIDTH, COL_AQ), zspec(KV_WIDTH, COL_AK), zspec(KV_WIDTH, COL_AV),
                  full((1, 2 * hd)), full((1, 2 * hd)),
                  pl.BlockSpec((tl, 2 * hd), lambda b, j: (j, 0)), pl.BlockSpec((tl, 2 * hd), lambda b, j: (j, 0)),
                  full((2 * hd, 2 * hd)), full((2 * hd, 2 * hd))],
        out_specs=out_specs,
        out_shape=out_shape,
        compiler_params=_params("parallel", "parallel"),
        name="qkv_prep",
    )(z2d, z2d, z2d, both_halves(q_gain), both_halves(k_gain), cos, sin,
      jnp.asarray(swap, BF16), jnp.asarray(avg, BF16))


def _attn_kernel(q_ref, k_ref, v_ref, *rest, ts, n_s, tq):
    o_ref = rest[-1]
    rows = GROUP * tq
    q = q_ref[0].reshape(rows, 2 * HEAD_DIM)

    def body(i, carry):
        m, acc = carry
        sl = pl.ds(pl.multiple_of(i * ts, ts), ts)
        s = _dot_nt(q, k_ref[0, 0, sl, :])
        m_new = jnp.maximum(m, jnp.max(s, axis=-1, keepdims=True))
        p = jnp.exp2(s - m_new)
        acc = jnp.exp2(m - m_new) * acc + _dot(p.astype(BF16), v_ref[0, 0, sl, :])
        return m_new, acc

    m0 = jnp.full((rows, 1), NEG_INF, F32)
    acc0 = jnp.zeros((rows, 2 * HEAD_DIM), F32)
    m, acc = lax.fori_loop(0, n_s, body, (m0, acc0), unroll=True)
    o = acc[:, :HEAD_DIM] / acc[:, HEAD_DIM:HEAD_DIM + 1]
    for g in range(GROUP):
        o_ref[:, g * HEAD_DIM:(g + 1) * HEAD_DIM] = o[g * tq:(g + 1) * tq]


def _attention(q, k, v_aug, o_prev, row0, t_all):
    b, _, length, _ = q.shape
    hd = HEAD_DIM
    s_len = k.shape[2]
    tq = _tile(length, ATTN_TQ)
    ts = _tile(s_len, ATTN_TS)
    nq = length // tq
    blk0 = row0 // tq
    in_specs = [
        pl.BlockSpec((1, GROUP, tq, 2 * hd), lambda bi, hi, qi: (bi, hi, qi, 0)),
        pl.BlockSpec((1, 1, s_len, 2 * hd), lambda bi, hi, qi: (bi, hi, 0, 0)),
        pl.BlockSpec((1, 1, s_len, 2 * hd), lambda bi, hi, qi: (bi, hi, 0, 0)),
    ]
    args = [q, k, v_aug]
    aliases = {}
    if o_prev is not None:
        in_specs.append(pl.BlockSpec(memory_space=pl.ANY))
        args.append(o_prev)
        aliases = {3: 0}
    return pl.pallas_call(
        functools.partial(_attn_kernel, ts=ts, n_s=s_len // ts, tq=tq),
        grid=(b, KV_HEADS, nq),
        in_specs=in_specs,
        out_specs=pl.BlockSpec((tq, GROUP * hd), lambda bi, hi, qi: (blk0 + bi * nq + qi, hi)),
        out_shape=jax.ShapeDtypeStruct((t_all, ATTN_WIDTH), F32),
        input_output_aliases=aliases,
        compiler_params=_params("parallel", "parallel", "parallel"),
        name="attention",
    )(*args)


def _layer_norm(t, g, b):
    mu = jnp.mean(t, axis=-1, keepdims=True)
    tc = t - mu
    var = jnp.mean(tc * tc, axis=-1, keepdims=True)
    return tc * lax.rsqrt(var + EPS) * g + b


def _merge_kernel(of_ref, ob_ref, hg_ref, ga0_ref, ga1_ref, gb0_ref, gb1_ref, oatt_ref, x_ref,
                  g1_ref, sh2_ref, sc2_ref, wrec_ref, watt_ref, wout_ref, ng_ref, lng_ref, lnb_ref,
                  x1_ref, h2_ref, orec_scr):
    for h in range(HGRN_HEADS):
        cs = slice(h * HGRN_DK, (h + 1) * HGRN_DK)
        o = of_ref[:, cs] + ob_ref[:, cs]
        ms = jnp.mean(o * o, axis=-1, keepdims=True)
        zg = hg_ref[:, cs]
        orec_scr[:, cs] = (o * lax.rsqrt(ms + EPS) * ng_ref[...] * _silu(zg)).astype(BF16)
    ga = jnp.concatenate([ga0_ref[...], ga1_ref[...]], axis=1)
    gb = jnp.concatenate([gb0_ref[...], gb1_ref[...]], axis=1)
    y = (_sigmoid_t(ga) * _dot(orec_scr[...], wrec_ref[...])
         + _sigmoid_t(gb) * _dot(oatt_ref[...].astype(BF16), watt_ref[...]))
    u = _dot(y.astype(BF16), wout_ref[...])
    x1 = _layer_norm(DEEPNORM_ALPHA * x_ref[...] + g1_ref[0] * u, lng_ref[...], lnb_ref[...])
    x1_ref[...] = x1
    h2_ref[...] = x1 * (1.0 + sc2_ref[0]) + sh2_ref[0]


def _merge(o_f, o_b, z2d, o_att, x2d, mod_l, w_rec, w_att, w_out, norm_g, ln_g, ln_b, rows_of_tile, tm):
    t, d = x2d.shape
    mod3 = mod_l.reshape(8, 1, 6 * d)

    def rows(colblk):
        return pl.BlockSpec((tm, d), lambda i: (i, colblk))

    def half_rows(col, part):
        return pl.BlockSpec((tm, d // 2), lambda i: (i, col // (d // 2) + part))

    def modspec(chunk):
        return pl.BlockSpec((1, 1, d), lambda i: (rows_of_tile(i), 0, chunk))

    def full(shape):
        return pl.BlockSpec(shape, lambda i: (0,) * len(shape))

    def weight():
        return pl.BlockSpec((d, d), lambda i: (0, 0), pipeline_mode=pl.Buffered(1))

    return pl.pallas_call(
        _merge_kernel,
        grid=(t // tm,),
        in_specs=[rows(0), rows(0), rows(4), half_rows(COL_GA, 0), half_rows(COL_GA, 1),
                  half_rows(COL_GB, 0), half_rows(COL_GB, 1), rows(0), rows(0),
                  modspec(2), modspec(3), modspec(4),
                  weight(), weight(), weight(), full((1, HGRN_DK)), full((1, d)), full((1, d))],
        out_specs=[rows(0), rows(0)],
        out_shape=[jax.ShapeDtypeStruct((t, d), F32), jax.ShapeDtypeStruct((t, d), F32)],
        scratch_shapes=[pltpu.VMEM((tm, d), BF16)],
        compiler_params=_params("parallel"),
        name="merge",
    )(o_f, o_b, z2d, z2d, z2d, z2d, z2d, o_att, x2d, mod3, mod3, mod3, w_rec, w_att, w_out,
      norm_g.reshape(1, HGRN_DK), ln_g.reshape(1, d), ln_b.reshape(1, d))


def _top16(s, n):
    iota = lax.broadcasted_iota(jnp.int32, s.shape, 0).astype(F32)
    vals, idxs = [], []
    for _ in range(PEER_TOPK):
        m = jnp.max(s, axis=0, keepdims=True)
        idx = jnp.min(jnp.where(s == m, iota, float(n)), axis=0, keepdims=True)
        vals.append(m)
        idxs.append(idx)
        s = jnp.where(iota == idx, NEG_INF, s)
    return vals, idxs


def _candidate_positions():
    rows = []
    for a in range(4):
        for b in range(16 if a == 0 else 8):
            rows.append((a, b, (a + 1) * (b + 1) <= PEER_TOPK))
    for b, n_a in ((0, 16), (1, 8), (2, 8)):
        for a in range(n_a):
            rows.append((a, b, a >= 4 and (a + 1) * (b + 1) <= PEER_TOPK))
    return np.array([[a * PEER_TOPK + b if ok else -1.0] for a, b, ok in rows], np.float32)


def _peer_head_topk(qh, keys0, keys1, posb):
    k = PEER_TOPK
    tops = []
    for p, keys in enumerate((keys0, keys1)):
        st = _dot_nt(keys, qh[:, p * PEER_HALF:(p + 1) * PEER_HALF].astype(BF16))
        tops.append(_top16(st, N_KEYS))
    (v0, i0), (v1, i1) = tops
    v0c = jnp.concatenate(v0, axis=0)
    v1c = jnp.concatenate(v1, axis=0)
    cand = jnp.concatenate([v0[0] + v1c, v0[1] + v1c[:8], v0[2] + v1c[:8], v0[3] + v1c[:8],
                            v0c + v1[0], v0c[:8] + v1[1], v0c[:8] + v1[2]], axis=0)
    cand = jnp.where(posb >= 0.0, cand, NEG_INF)
    best, poss = [], []
    for _ in range(k):
        m = jnp.max(cand, axis=0, keepdims=True)
        pos = jnp.min(jnp.where(cand == m, posb, float(k * k)), axis=0, keepdims=True)
        best.append(m)
        poss.append(pos)
        cand = jnp.where(posb == pos, NEG_INF, cand)
    best = jnp.concatenate(best, axis=0)
    posc = jnp.concatenate(poss, axis=0)
    a_rank = jnp.floor(posc * (1.0 / k))
    b_rank = posc - k * a_rank
    ea = jnp.zeros_like(posc)
    eb = jnp.zeros_like(posc)
    for r in range(k):
        ea = jnp.where(a_rank == float(r), i0[r], ea)
        eb = jnp.where(b_rank == float(r), i1[r], eb)
    e = jnp.exp(best - best[0:1, :])
    return ea, eb, e / jnp.sum(e, axis=0, keepdims=True)


RSQRT2 = 1.0 / math.sqrt(2.0)


def _peer_kernel(h2n_ref, wq_ref, keys_ref, pos_ref, u_ref, v_ref, x1_ref, g2_ref, lng_ref, lnb_ref,
                 x2_ref, ids_scr, idt_scr, w_scr, p_scr, acc_scr, h2n_scr, h2_scr, *, tb, te, pitch, nblk):
    i = pl.program_id(0)
    j = pl.program_id(1)
    slot = i % 2
    k = PEER_TOPK
    rows_per_tile = te // N_KEYS
    half = N_KEYS // 2
    hi_mask = jnp.uint32(0xFFFF0000)
    first_row = i == 0
    last_row = i == nblk

    @pl.when(jnp.logical_and(j == 0, jnp.logical_not(first_row)))
    def _():
        h2_scr[...] = h2n_scr[...]
        for n in range(3):
            idt_scr[n] = ids_scr[1 - slot, n].T
        acc_scr[...] = jnp.zeros_like(acc_scr)
        iota = lax.broadcasted_iota(jnp.int32, (N_KEYS, PEER_HEADS * k), 0).astype(F32)

        def tok(t, carry):
            a_row = idt_scr[0, pl.ds(t, 1), :]
            b_row = idt_scr[1, pl.ds(t, 1), :]
            g_row = idt_scr[2, pl.ds(t, 1), :] * RSQRT2
            at = jnp.where(iota == a_row, 1.0, 0.0).astype(BF16)
            gbt = jnp.where(iota == b_row, g_row, 0.0).astype(BF16)
            w = pltpu.bitcast(_dot_nt(at, gbt), jnp.uint32)
            w_scr[pl.ds(t, half, stride=pitch), :] = (w[:half] & hi_mask) | (w[half:] >> 16)
            return carry

        lax.fori_loop(0, tb, tok, 0, unroll=64)

    @pl.when(jnp.logical_and(j == 0, jnp.logical_not(last_row)))
    def _():
        h2n_scr[...] = h2n_ref[...].astype(BF16)

    def retrieve():
        posb = jnp.broadcast_to(pos_ref[...], (pos_ref.shape[0], tb))
        ea, eb, gate = _peer_head_topk(_dot(h2n_scr[...], wq_ref[...]), keys_ref[0, 0], keys_ref[0, 1], posb)
        rows = pl.ds(pl.multiple_of(j * k, k), k)
        ids_scr[slot, 0, rows, :] = ea
        ids_scr[slot, 1, rows, :] = eb
        ids_scr[slot, 2, rows, :] = gate

    def mix():
        a = _dot_nt(h2_scr[...], u_ref[...])
        for r in range(rows_per_tile):
            i1 = j * rows_per_tile + r
            packed = w_scr[pl.ds(pl.multiple_of((i1 % half) * pitch, 8), tb), :]
            shift = jnp.where(i1 >= half, 16, 0).astype(jnp.uint32)
            w = pltpu.bitcast((packed << shift) & hi_mask, F32)
            cs = slice(r * N_KEYS, (r + 1) * N_KEYS)
            ar = a[:, cs]
            p_scr[:, cs] = (w * (ar * (1.0 + lax.erf(ar)))).astype(BF16)
        acc_scr[...] += _dot(p_scr[...], v_ref[...])

    @pl.when(first_row)
    def _():
        retrieve()

    @pl.when(last_row)
    def _():
        mix()

    @pl.when(jnp.logical_not(jnp.logical_or(first_row, last_row)))
    def _():
        retrieve()
        mix()

    @pl.when(jnp.logical_and(j == pl.num_programs(1) - 1, jnp.logical_not(first_row)))
    def _():
        x2_ref[...] = _layer_norm(DEEPNORM_ALPHA * x1_ref[...] + g2_ref[0] * acc_scr[...],
                                  lng_ref[...], lnb_ref[...])


def _peer(h2, wq_bf, keys_bf, u_bf, v_bf, x1, mod_l, ln_g, ln_b, rows_of_tile, tb):
    t, d = x1.shape
    n_e = v_bf.shape[0]
    te = PEER_TE
    assert n_e // te == PEER_HEADS
    hk = PEER_HEADS * PEER_TOPK
    pitch = tb + 8
    nblk = t // tb
    mod3 = mod_l.reshape(8, 1, 6 * d)
    pos = _candidate_positions()

    def cur(i):
        return jnp.maximum(i - 1, 0)

    def full(shape):
        return pl.BlockSpec(shape, lambda i, j: (0,) * len(shape))

    return pl.pallas_call(
        functools.partial(_peer_kernel, tb=tb, te=te, pitch=pitch, nblk=nblk),
        grid=(nblk + 1, PEER_HEADS),
        in_specs=[
            pl.BlockSpec((tb, d), lambda i, j: (jnp.minimum(i, nblk - 1), 0)),
            pl.BlockSpec((d, 2 * PEER_HALF), lambda i, j: (0, j)),
            pl.BlockSpec((1, 2, N_KEYS, PEER_HALF), lambda i, j: (j, 0, 0, 0)),
            full(pos.shape),
            pl.BlockSpec((te, d), lambda i, j: (j, 0)),
            pl.BlockSpec((te, d), lambda i, j: (j, 0)),
            pl.BlockSpec((tb, d), lambda i, j: (cur(i), 0)),
            pl.BlockSpec((1, 1, d), lambda i, j: (rows_of_tile(cur(i)), 0, 5)),
            full((1, d)), full((1, d)),
        ],
        out_specs=pl.BlockSpec((tb, d), lambda i, j: (cur(i), 0)),
        out_shape=jax.ShapeDtypeStruct((t, d), F32),
        scratch_shapes=[pltpu.VMEM((2, 3, hk, tb), F32), pltpu.VMEM((3, tb, hk), F32),
                        pltpu.VMEM((pitch * N_KEYS // 2, N_KEYS), jnp.uint32), pltpu.VMEM((tb, te), BF16),
                        pltpu.VMEM((tb, d), F32), pltpu.VMEM((tb, d), BF16), pltpu.VMEM((tb, d), BF16)],
        compiler_params=pltpu.CompilerParams(dimension_semantics=("arbitrary", "arbitrary"),
                                             vmem_limit_bytes=PEER_VMEM_LIMIT),
        name="peer",
    )(h2, wq_bf, keys_bf, jnp.asarray(pos), u_bf, v_bf, x1, mod3, ln_g.reshape(1, d), ln_b.reshape(1, d))


def _rope_tables(length):
    rows = length // GRID_W
    row = jnp.broadcast_to(jnp.arange(rows, dtype=F32)[:, None], (rows, GRID_W)).reshape(length)
    col = jnp.broadcast_to(jnp.arange(GRID_W, dtype=F32)[None, :], (rows, GRID_W)).reshape(length)
    n_freq = HEAD_DIM // 4
    inv = jnp.power(ROPE_THETA, -jnp.arange(n_freq, dtype=F32) / n_freq)
    ang = jnp.concatenate([row[:, None] * inv, col[:, None] * inv], axis=-1)
    cos, sin = jnp.cos(ang), jnp.sin(ang)
    return jnp.concatenate([cos, cos], axis=-1), jnp.concatenate([-sin, sin], axis=-1)


def kernel(x_prompt, x_sample, cache_attn_k, cache_attn_v, state_hgrn, c, c_ctx, w_mod, b_mod, w_in,
           hgrn_lb_logits, hgrn_norm_g, q_norm_g, k_norm_g, w_branch_rec, w_branch_att, w_out,
           ln1_g, ln1_b, ln2_g, ln2_b, peer_w_query, peer_sub_keys, peer_u, peer_v):
    nb, seq, d = x_prompt.shape
    db, dseq, _ = x_sample.shape
    depth = w_in.shape[0]
    past = cache_attn_k.shape[2]
    t_ctx, t_lat = nb * seq, db * dseq
    t_all = t_ctx + t_lat
    assert d == D_MODEL and 1 + db <= 8

    tm = _tile(math.gcd(t_ctx, dseq), 2048)
    tm_merge = _tile(math.gcd(t_ctx, dseq), 512)
    tb_mix = _tile(math.gcd(t_ctx, dseq), PEER_TB_MIX)

    def rows_of_tile_fn(tile):
        n_ctx = t_ctx // tile
        per = dseq // tile
        return lambda i: jnp.where(i < n_ctx, 0, 1 + (i - n_ctx) // per)

    cond8 = jnp.concatenate([c_ctx[None, :], c, jnp.zeros((8 - 1 - db, d), F32)], axis=0)
    mod = _modulation(cond8, w_mod, b_mod)

    lb = jnp.cumsum(jax.nn.softmax(hgrn_lb_logits.astype(F32), axis=1), axis=1)
    lb = lb - lb[:, :1]

    w_in_bf = w_in.astype(BF16)
    w_rec_bf, w_att_bf, w_out_bf = (w.astype(BF16) for w in (w_branch_rec, w_branch_att, w_out))
    wq_bf = peer_w_query.astype(BF16)
    keys_bf = peer_sub_keys.astype(BF16)
    u_bf = (peer_u * RSQRT2).astype(BF16)
    v_bf = peer_v.astype(BF16)

    cos_t, sin_t = (jnp.concatenate([t, t], axis=-1) for t in _rope_tables(dseq))
    ones_t, zeros_t = jnp.ones((seq, 2 * HEAD_DIM), F32), jnp.zeros((seq, 2 * HEAD_DIM), F32)
    cache_k_t = jnp.transpose(cache_attn_k, (0, 1, 3, 2, 4))
    cache_k_bf = jnp.concatenate([cache_k_t, cache_k_t], axis=-1).astype(BF16)
    cache_v_t = jnp.transpose(cache_attn_v, (0, 1, 3, 2, 4))
    ones_pad = jnp.zeros(cache_v_t.shape, F32).at[..., 0].set(1.0)
    cache_v_aug = jnp.concatenate([cache_v_t, ones_pad], axis=-1).astype(BF16)

    seq_rows = [(i * seq, seq) for i in range(nb)] + [(t_ctx + i * dseq, dseq) for i in range(db)]
    zero_state = jnp.zeros((nb, 2, HGRN_HEADS, HGRN_DK, HGRN_DK), F32)

    x2d = jnp.concatenate([x_prompt.reshape(t_ctx, d), x_sample.reshape(t_lat, d)], axis=0)
    new_k, new_v, new_s = [], [], []
    for l in range(depth):
        z2d = _in_proj(x2d, mod[l], w_in_bf[l], rows_of_tile_fn(tm), tm)

        s0_all = jnp.concatenate([zero_state, state_hgrn[:, l]], axis=0)
        o_f, o_b, s_fin = _hgrn_scan(z2d, lb[:, l], s0_all, seq_rows, HGRN_CHUNK)
        new_s.append(s_fin[:nb])

        q_c, k_c, v_c, k_norm = _qkv_prep(z2d, 0, nb, seq, q_norm_g[l], k_norm_g[l], ones_t, zeros_t, False, True)
        o_att = _attention(q_c, k_c, v_c, None, 0, t_all)
        new_k.append(k_norm.reshape(nb, seq, KV_HEADS, HEAD_DIM))
        new_v.append(z2d[:t_ctx, COL_AV:COL_AV + KV_WIDTH].reshape(nb, seq, KV_HEADS, HEAD_DIM))

        q_s, k_s, v_s = _qkv_prep(z2d, t_ctx, db, dseq, q_norm_g[l], k_norm_g[l], cos_t, sin_t, True, False)
        k_s = jnp.concatenate([k_s, cache_k_bf[:, l]], axis=2)
        v_s = jnp.concatenate([v_s, cache_v_aug[:, l]], axis=2)
        o_att = _attention(q_s, k_s, v_s, o_att, t_ctx, t_all)

        x1, h2 = _merge(o_f, o_b, z2d, o_att, x2d, mod[l], w_rec_bf[l], w_att_bf[l], w_out_bf[l],
                        hgrn_norm_g[l], ln1_g[l], ln1_b[l], rows_of_tile_fn(tm_merge), tm_merge)

        x2d = _peer(h2, wq_bf[l], keys_bf[l], u_bf[l], v_bf[l], x1, mod[l], ln2_g[l], ln2_b[l],
                    rows_of_tile_fn(tb_mix), tb_mix)

    y_prompt = x2d[:t_ctx].reshape(nb, seq, d)
    y_sample = x2d[t_ctx:].reshape(db, dseq, d)
    return (y_prompt, y_sample, jnp.stack(new_k, axis=1), jnp.stack(new_v, axis=1), jnp.stack(new_s, axis=1))
```

```python
import functools
import math

import numpy as np
import jax
import jax.numpy as jnp
from jax import lax
from jax.experimental import pallas as pl
from jax.experimental.pallas import tpu as pltpu

F32 = jnp.float32
BF16 = jnp.bfloat16

D_MODEL = 1024
GRID_W = 64
HGRN_HEADS = 8
HGRN_DK = 128
HGRN_WIDTH = HGRN_HEADS * HGRN_DK
HGRN_CHUNK = 128
ATTN_HEADS = 16
KV_HEADS = 4
HEAD_DIM = 64
GROUP = ATTN_HEADS // KV_HEADS
ATTN_WIDTH = ATTN_HEADS * HEAD_DIM
KV_WIDTH = KV_HEADS * HEAD_DIM
ATTN_SCALE = HEAD_DIM ** -0.5
ROPE_THETA = 10000.0
PEER_HEADS = 8
PEER_HALF = 128
N_KEYS = 128
N_EXPERTS = N_KEYS * N_KEYS
PEER_TOPK = 16
MODEL_DEPTH = 4
DEEPNORM_ALPHA = (2 * MODEL_DEPTH) ** 0.25
EPS = 1e-6
NEG_INF = float("-inf")

IN_COLS = 5 * HGRN_WIDTH + ATTN_WIDTH + 2 * KV_WIDTH + 2 * D_MODEL
COL_AQ = 5 * HGRN_WIDTH
COL_AK = COL_AQ + ATTN_WIDTH
COL_AV = COL_AK + KV_WIDTH
COL_GA = COL_AV + KV_WIDTH
COL_GB = COL_GA + D_MODEL

VMEM_LIMIT = 56 * 1024 * 1024
PEER_VMEM_LIMIT = 60 * 1024 * 1024
LOG2_E = math.log2(math.e)
ATTN_TQ = 512
ATTN_TS = 512
PEER_TB_MIX = 512
PEER_TE = 2048


def _params(*sem):
    return pltpu.CompilerParams(dimension_semantics=sem, vmem_limit_bytes=VMEM_LIMIT)


def _tile(n, pref):
    t = min(n, pref)
    while n % t or t % 8:
        t -= 1
    return t


def _sigmoid(x):
    return 1.0 / (1.0 + jnp.exp(-x))


def _sigmoid_t(x):
    return 0.5 + 0.5 * jnp.tanh(0.5 * x)


def _silu(x):
    h = 0.5 * x
    return h + h * jnp.tanh(h)


def _bf16_floor(x):
    bits = pltpu.bitcast(x, jnp.uint32) & jnp.uint32(0xFFFF0000)
    return pltpu.bitcast(bits, F32)


def _dot(a, b):
    return jnp.dot(a, b, preferred_element_type=F32)


def _dot_nt(a, b):
    return lax.dot_general(a, b, (((1,), (1,)), ((), ())), preferred_element_type=F32)


def _mod_kernel(cond_ref, w_ref, b_ref, o_ref):
    c = cond_ref[...]
    s = c * _sigmoid(c)
    o_ref[0] = jnp.dot(s, w_ref[0], precision=lax.Precision.HIGHEST, preferred_element_type=F32) + b_ref[0]


def _modulation(cond8, w_mod, b_mod):
    depth, d, n = w_mod.shape
    tn = _tile(n, 1536)
    return pl.pallas_call(
        _mod_kernel,
        grid=(depth, n // tn),
        in_specs=[
            pl.BlockSpec((8, d), lambda l, j: (0, 0)),
            pl.BlockSpec((1, d, tn), lambda l, j: (l, 0, j)),
            pl.BlockSpec((1, 1, tn), lambda l, j: (l, 0, j)),
        ],
        out_specs=pl.BlockSpec((1, 8, tn), lambda l, j: (l, 0, j)),
        out_shape=jax.ShapeDtypeStruct((depth, 8, n), F32),
        compiler_params=_params("parallel", "parallel"),
        name="modulation",
    )(cond8, w_mod, b_mod.reshape(depth, 1, n))


def _inproj_kernel(x_ref, sh_ref, sc_ref, w_ref, z_ref, h_scr):
    @pl.when(pl.program_id(1) == 0)
    def _():
        h_scr[...] = (x_ref[...] * (1.0 + sc_ref[0]) + sh_ref[0]).astype(BF16)

    z_ref[...] = _dot(h_scr[...], w_ref[...])


def _in_proj(x2d, mod_l, w_in_bf, rows_of_tile, tm):
    t, d = x2d.shape
    n = w_in_bf.shape[1]
    tn = 512
    mod3 = mod_l.reshape(8, 1, 6 * d)
    return pl.pallas_call(
        _inproj_kernel,
        grid=(t // tm, n // tn),
        in_specs=[
            pl.BlockSpec((tm, d), lambda i, j: (i, 0)),
            pl.BlockSpec((1, 1, d), lambda i, j: (rows_of_tile(i), 0, 0)),
            pl.BlockSpec((1, 1, d), lambda i, j: (rows_of_tile(i), 0, 1)),
            pl.BlockSpec((d, tn), lambda i, j: (0, j)),
        ],
        out_specs=pl.BlockSpec((tm, tn), lambda i, j: (i, j)),
        out_shape=jax.ShapeDtypeStruct((t, n), F32),
        scratch_shapes=[pltpu.VMEM((tm, d), BF16)],
        compiler_params=_params("parallel", "arbitrary"),
        name="in_proj",
    )(x2d, mod3, mod3, w_in_bf)


def _level_reference_rows(beta, c):
    width = beta.shape[1]
    sub = lax.broadcasted_iota(jnp.int32, (8, width), 0)

    def bc(i, n=8):
        return jnp.broadcast_to(beta[i:i + 1, :], (n, width))

    out = []
    m = 1
    while m < c:
        blocks = []
        if m >= 8:
            for blk in range(c // (2 * m)):
                blocks.append(bc(blk * 2 * m + m, 2 * m))
        elif m == 4:
            for v in range(c // 8):
                blocks.append(bc(8 * v + 4))
        elif m == 2:
            for v in range(c // 8):
                blocks.append(jnp.where(sub < 4, bc(8 * v + 2), bc(8 * v + 6)))
        else:
            for v in range(c // 8):
                lo = jnp.where(sub < 2, bc(8 * v + 1), bc(8 * v + 3))
                hi = jnp.where(sub < 6, bc(8 * v + 5), bc(8 * v + 7))
                blocks.append(jnp.where(sub < 4, lo, hi))
        out.append(blocks[0] if len(blocks) == 1 else jnp.concatenate(blocks, axis=0))
        m *= 2
    return out


def _hgrn_direction(zq, zf, v, lb, st_ref, o_ref, tri, eye, q_half, pair_masks, reverse, c):
    heads = [slice(h * HGRN_DK, (h + 1) * HGRN_DK) for h in range(HGRN_HEADS)]
    q = zq * pl.reciprocal(1.0 + jnp.exp2(zq * -LOG2_E), approx=True)
    zl = zf * -LOG2_E
    e = jnp.exp2(zl)
    l2 = jnp.log(1.0 + e) * LOG2_E
    g = jnp.log(1.0 + lb * e) * LOG2_E - l2
    k = (1.0 - lb) * jnp.exp2(zl - l2)
    g1 = _bf16_floor(g)
    r1 = g - g1
    g2 = _bf16_floor(r1)
    g3 = r1 - g2
    beta = (_dot(tri, g1.astype(BF16)) + _dot(tri, g2.astype(BF16))) + _dot(tri, g3.astype(BF16))
    last = 0 if reverse else c - 1
    beta_tot = beta[last:last + 1, :]

    q_bf, k_bf, v_bf = q.astype(BF16), k.astype(BF16), v.astype(BF16)
    a = [jnp.where(eye, _dot_nt(q_bf[:, hs], k_bf[:, hs]), 0.0) for hs in heads]
    for r, qh, pm in zip(_level_reference_rows(beta, c), q_half, pair_masks):
        neg_abs = pltpu.bitcast(pltpu.bitcast(beta - r, jnp.uint32) | jnp.uint32(0x80000000), F32)
        qk = jnp.concatenate([jnp.where(qh, q[:, hs], k[:, hs]) for hs in heads], axis=1)
        u = (qk * jnp.exp2(neg_abs)).astype(BF16)
        a = [ah + jnp.where(pm, _dot_nt(u[:, hs], u[:, hs]), 0.0) for ah, hs in zip(a, heads)]

    qb = (q * jnp.exp2(beta)).astype(BF16)
    kb = (k * jnp.exp2(beta_tot - beta)).astype(BF16)
    dec = jnp.exp2(beta_tot)
    for h, hs in enumerate(heads):
        st = st_ref[h]
        o_ref[:, hs] = _dot(a[h].astype(BF16), v_bf[:, hs]) + _dot_nt(qb[:, hs], st.astype(BF16))
        st_ref[h] = st * dec[:, hs] + lax.dot_general(
            v_bf[:, hs], kb[:, hs], (((0,), (0,)), ((), ())), preferred_element_type=F32)


def _hgrn_kernel(rbf_ref, rbb_ref, first_ref, seq_ref,
                 qf_ref, ff_ref, vf_ref, qb_ref, fb_ref, vb_ref, lb_ref, s0_ref,
                 of_ref, ob_ref, sout_ref, st_scr, *, c, n_last):
    s = pl.program_id(0)

    @pl.when(first_ref[s] == 1)
    def _():
        for d in range(2):
            for h in range(HGRN_HEADS):
                st_scr[d, h] = s0_ref[0, d, h].T

    row = lax.broadcasted_iota(jnp.int32, (c, c), 0)
    col = lax.broadcasted_iota(jnp.int32, (c, c), 1)
    rowd = lax.broadcasted_iota(jnp.int32, (c, HGRN_DK), 0)
    eye = row == col

    for d, (q_ref, f_ref, v_ref, o_ref) in enumerate(
            ((qf_ref, ff_ref, vf_ref, of_ref), (qb_ref, fb_ref, vb_ref, ob_ref))):
        reverse = d == 1
        tri = (col >= row if reverse else col <= row).astype(BF16)
        q_par = 0 if reverse else 1
        q_half, pair_masks = [], []
        m = 1
        while m < c:
            q_half.append(((rowd // m) % 2) == q_par)
            pair_masks.append(((row // (2 * m)) == (col // (2 * m)))
                              & (((row // m) % 2) == q_par) & (((col // m) % 2) == 1 - q_par))
            m *= 2
        _hgrn_direction(q_ref[...], f_ref[...], v_ref[...], lb_ref[pl.ds(d, 1), :], st_scr.at[d], o_ref,
                        tri, eye, q_half, pair_masks, reverse, c)

    @pl.when(first_ref[jnp.minimum(s + 1, n_last)] == 1)
    def _():
        for d in range(2):
            for h in range(HGRN_HEADS):
                sout_ref[0, d, h] = st_scr[d, h].T


def _hgrn_scan(z2d, lb2, s0_all, seq_rows, c):
    t = z2d.shape[0]
    rbf, rbb, first, seq = [], [], [], []
    for si, (r0, length) in enumerate(seq_rows):
        n = length // c
        for ci in range(n):
            rbf.append(r0 // c + ci)
            rbb.append(r0 // c + n - 1 - ci)
            first.append(1 if ci == 0 else 0)
            seq.append(si)
    n_steps = len(rbf)
    first.append(1)
    w = HGRN_WIDTH

    def zspec(which, colblk):
        if which == 0:
            return pl.BlockSpec((c, w), lambda s, rbf, rbb, fi, sq: (rbf[s], colblk))
        return pl.BlockSpec((c, w), lambda s, rbf, rbb, fi, sq: (rbb[s], colblk))

    n_seq = len(seq_rows)
    state_spec = pl.BlockSpec((1, 2, HGRN_HEADS, HGRN_DK, HGRN_DK),
                              lambda s, rbf, rbb, fi, sq: (sq[s], 0, 0, 0, 0))
    grid_spec = pltpu.PrefetchScalarGridSpec(
        num_scalar_prefetch=4,
        grid=(n_steps,),
        in_specs=[zspec(0, 0), zspec(0, 1), zspec(0, 3), zspec(1, 0), zspec(1, 2), zspec(1, 3),
                  pl.BlockSpec((2, w), lambda s, *_: (0, 0)), state_spec],
        out_specs=[zspec(0, 0), zspec(1, 0), state_spec],
        scratch_shapes=[pltpu.VMEM((2, HGRN_HEADS, HGRN_DK, HGRN_DK), F32)],
    )
    return pl.pallas_call(
        functools.partial(_hgrn_kernel, c=c, n_last=n_steps),
        grid_spec=grid_spec,
        out_shape=[jax.ShapeDtypeStruct((t, w), F32), jax.ShapeDtypeStruct((t, w), F32),
                   jax.ShapeDtypeStruct((n_seq, 2, HGRN_HEADS, HGRN_DK, HGRN_DK), F32)],
        compiler_params=_params("arbitrary"),
        name="hgrn_scan",
    )(jnp.asarray(rbf, jnp.int32), jnp.asarray(rbb, jnp.int32), jnp.asarray(first, jnp.int32),
      jnp.asarray(seq, jnp.int32), z2d, z2d, z2d, z2d, z2d, z2d, lb2, s0_all)


def _qkv_prep_kernel(aq_ref, ak_ref, av_ref, qg_ref, kg_ref, cos_ref, sin_ref, swap_ref, avg_ref, *out_refs,
                     rope, q_scale, keep_k):
    q_out, k_out, v_out = out_refs[:3]
    hd = HEAD_DIM
    tl = aq_ref.shape[0]
    lane = lax.broadcasted_iota(jnp.int32, (tl, 2 * hd), 1)
    low = lane < hd
    ones_col = jnp.where(lane == hd, 1.0, 0.0)

    def norm_rope(pair, g):
        sq = pair * pair
        hi = _bf16_floor(sq)
        ms = _dot(hi.astype(BF16), avg_ref[...]) + _dot((sq - hi).astype(BF16), avg_ref[...])
        y = pair * lax.rsqrt(ms + EPS) * g
        if rope:
            y = y * cos_ref[...] + _dot(y.astype(BF16), swap_ref[...]) * sin_ref[...]
        return y

    for j in range(ATTN_WIDTH // (2 * hd)):
        y = norm_rope(aq_ref[:, j * 2 * hd:(j + 1) * 2 * hd], qg_ref[...]) * q_scale
        q_out[0, 2 * j] = jnp.where(low, y, 0.0).astype(BF16)
        q_out[0, 2 * j + 1] = jnp.where(low, 0.0, y).astype(BF16)
    for j in range(KV_WIDTH // (2 * hd)):
        cols = slice(j * 2 * hd, (j + 1) * 2 * hd)
        y = norm_rope(ak_ref[:, cols], kg_ref[...])
        y_other = pltpu.roll(y, hd, axis=1)
        k_out[0, 2 * j] = jnp.where(low, y, y_other).astype(BF16)
        k_out[0, 2 * j + 1] = jnp.where(low, y_other, y).astype(BF16)
        if keep_k:
            out_refs[3][:, cols] = y
        vp = av_ref[:, cols]
        v_out[0, 2 * j] = jnp.where(low, vp, ones_col).astype(BF16)
        v_out[0, 2 * j + 1] = jnp.where(low, pltpu.roll(vp, hd, axis=1), ones_col).astype(BF16)


def _qkv_prep(z2d, row0, n_seq, length, q_gain, k_gain, cos, sin, rope, keep_k):
    hd = HEAD_DIM
    tl = _tile(length, 512)
    nt = length // tl
    blk0 = row0 // tl
    swap = np.zeros((2 * hd, 2 * hd), np.float32)
    avg = np.zeros((2 * hd, 2 * hd), np.float32)
    for h0 in (0, hd):
        avg[h0:h0 + hd, h0:h0 + hd] = 1.0 / hd
        for i in range(hd):
            swap[h0 + (i + hd // 2) % hd, h0 + i] = 1.0

    def zspec(width, col):
        return pl.BlockSpec((tl, width), lambda b, j: (blk0 + b * nt + j, col // width))

    def full(shape):
        return pl.BlockSpec(shape, lambda b, j: (0,) * len(shape))

    def head_major(n_heads):
        return (pl.BlockSpec((1, n_heads, tl, 2 * hd), lambda b, j: (b, 0, j, 0)),
                jax.ShapeDtypeStruct((n_seq, n_heads, length, 2 * hd), BF16))

    out_specs, out_shape = map(list, zip(head_major(ATTN_HEADS), head_major(KV_HEADS), head_major(KV_HEADS)))
    if keep_k:
        out_specs.append(pl.BlockSpec((tl, KV_WIDTH), lambda b, j: (b * nt + j, 0)))
        out_shape.append(jax.ShapeDtypeStruct((n_seq * length, KV_WIDTH), F32))

    def both_halves(g):
        return jnp.concatenate([g, g]).reshape(1, 2 * hd)

    return pl.pallas_call(
        functools.partial(_qkv_prep_kernel, rope=rope, q_scale=ATTN_SCALE * LOG2_E, keep_k=keep_k),
        grid=(n_seq, nt),
        in_specs=[zspec(ATTN_WIDTH, COL_AQ), zspec(KV_WIDTH, COL_AK), zspec(KV_WIDTH, COL_AV),
                  full((1, 2 * hd)), full((1, 2 * hd)),
                  pl.BlockSpec((tl, 2 * hd), lambda b, j: (j, 0)), pl.BlockSpec((tl, 2 * hd), lambda b, j: (j, 0)),
                  full((2 * hd, 2 * hd)), full((2 * hd, 2 * hd))],
        out_specs=out_specs,
        out_shape=out_shape,
        compiler_params=_params("parallel", "parallel"),
        name="qkv_prep",
    )(z2d, z2d, z2d, both_halves(q_gain), both_halves(k_gain), cos, sin,
      jnp.asarray(swap, BF16), jnp.asarray(avg, BF16))


def _attn_kernel(q_ref, k_ref, v_ref, *rest, ts, n_s, tq):
    o_ref = rest[-1]
    rows = GROUP * tq
    q = q_ref[0].reshape(rows, 2 * HEAD_DIM)

    def body(i, carry):
        m, acc = carry
        sl = pl.ds(pl.multiple_of(i * ts, ts), ts)
        s = _dot_nt(q, k_ref[0, 0, sl, :])
        m_new = jnp.maximum(m, jnp.max(s, axis=-1, keepdims=True))
        p = jnp.exp2(s - m_new)
        acc = jnp.exp2(m - m_new) * acc + _dot(p.astype(BF16), v_ref[0, 0, sl, :])
        return m_new, acc

    m0 = jnp.full((rows, 1), NEG_INF, F32)
    acc0 = jnp.zeros((rows, 2 * HEAD_DIM), F32)
    m, acc = lax.fori_loop(0, n_s, body, (m0, acc0), unroll=True)
    o = acc[:, :HEAD_DIM] / acc[:, HEAD_DIM:HEAD_DIM + 1]
    for g in range(GROUP):
        o_ref[:, g * HEAD_DIM:(g + 1) * HEAD_DIM] = o[g * tq:(g + 1) * tq]


def _attention(q, k, v_aug, o_prev, row0, t_all):
    b, _, length, _ = q.shape
    hd = HEAD_DIM
    s_len = k.shape[2]
    tq = _tile(length, ATTN_TQ)
    ts = _tile(s_len, ATTN_TS)
    nq = length // tq
    blk0 = row0 // tq
    in_specs = [
        pl.BlockSpec((1, GROUP, tq, 2 * hd), lambda bi, hi, qi: (bi, hi, qi, 0)),
        pl.BlockSpec((1, 1, s_len, 2 * hd), lambda bi, hi, qi: (bi, hi, 0, 0)),
        pl.BlockSpec((1, 1, s_len, 2 * hd), lambda bi, hi, qi: (bi, hi, 0, 0)),
    ]
    args = [q, k, v_aug]
    aliases = {}
    if o_prev is not None:
        in_specs.append(pl.BlockSpec(memory_space=pl.ANY))
        args.append(o_prev)
        aliases = {3: 0}
    return pl.pallas_call(
        functools.partial(_attn_kernel, ts=ts, n_s=s_len // ts, tq=tq),
        grid=(b, KV_HEADS, nq),
        in_specs=in_specs,
        out_specs=pl.BlockSpec((tq, GROUP * hd), lambda bi, hi, qi: (blk0 + bi * nq + qi, hi)),
        out_shape=jax.ShapeDtypeStruct((t_all, ATTN_WIDTH), F32),
        input_output_aliases=aliases,
        compiler_params=_params("parallel", "parallel", "parallel"),
        name="attention",
    )(*args)


def _layer_norm(t, g, b):
    mu = jnp.mean(t, axis=-1, keepdims=True)
    tc = t - mu
    var = jnp.mean(tc * tc, axis=-1, keepdims=True)
    return tc * lax.rsqrt(var + EPS) * g + b


def _merge_kernel(of_ref, ob_ref, hg_ref, ga0_ref, ga1_ref, gb0_ref, gb1_ref, oatt_ref, x_ref,
                  g1_ref, sh2_ref, sc2_ref, wrec_ref, watt_ref, wout_ref, ng_ref, lng_ref, lnb_ref,
                  x1_ref, h2_ref, orec_scr):
    for h in range(HGRN_HEADS):
        cs = slice(h * HGRN_DK, (h + 1) * HGRN_DK)
        o = of_ref[:, cs] + ob_ref[:, cs]
        ms = jnp.mean(o * o, axis=-1, keepdims=True)
        zg = hg_ref[:, cs]
        orec_scr[:, cs] = (o * lax.rsqrt(ms + EPS) * ng_ref[...] * _silu(zg)).astype(BF16)
    ga = jnp.concatenate([ga0_ref[...], ga1_ref[...]], axis=1)
    gb = jnp.concatenate([gb0_ref[...], gb1_ref[...]], axis=1)
    y = (_sigmoid_t(ga) * _dot(orec_scr[...], wrec_ref[...])
         + _sigmoid_t(gb) * _dot(oatt_ref[...].astype(BF16), watt_ref[...]))
    u = _dot(y.astype(BF16), wout_ref[...])
    x1 = _layer_norm(DEEPNORM_ALPHA * x_ref[...] + g1_ref[0] * u, lng_ref[...], lnb_ref[...])
    x1_ref[...] = x1
    h2_ref[...] = x1 * (1.0 + sc2_ref[0]) + sh2_ref[0]


def _merge(o_f, o_b, z2d, o_att, x2d, mod_l, w_rec, w_att, w_out, norm_g, ln_g, ln_b, rows_of_tile, tm):
    t, d = x2d.shape
    mod3 = mod_l.reshape(8, 1, 6 * d)

    def rows(colblk):
        return pl.BlockSpec((tm, d), lambda i: (i, colblk))

    def half_rows(col, part):
        return pl.BlockSpec((tm, d // 2), lambda i: (i, col // (d // 2) + part))

    def modspec(chunk):
        return pl.BlockSpec((1, 1, d), lambda i: (rows_of_tile(i), 0, chunk))

    def full(shape):
        return pl.BlockSpec(shape, lambda i: (0,) * len(shape))

    def weight():
        return pl.BlockSpec((d, d), lambda i: (0, 0))

    return pl.pallas_call(
        _merge_kernel,
        grid=(t // tm,),
        in_specs=[rows(0), rows(0), rows(4), half_rows(COL_GA, 0), half_rows(COL_GA, 1),
                  half_rows(COL_GB, 0), half_rows(COL_GB, 1), rows(0), rows(0),
                  modspec(2), modspec(3), modspec(4),
                  weight(), weight(), weight(), full((1, HGRN_DK)), full((1, d)), full((1, d))],
        out_specs=[rows(0), rows(0)],
        out_shape=[jax.ShapeDtypeStruct((t, d), F32), jax.ShapeDtypeStruct((t, d), F32)],
        scratch_shapes=[pltpu.VMEM((tm, d), BF16)],
        compiler_params=_params("parallel"),
        name="merge",
    )(o_f, o_b, z2d, z2d, z2d, z2d, z2d, o_att, x2d, mod3, mod3, mod3, w_rec, w_att, w_out,
      norm_g.reshape(1, HGRN_DK), ln_g.reshape(1, d), ln_b.reshape(1, d))


def _top16(s, n):
    iota = lax.broadcasted_iota(jnp.int32, s.shape, 0).astype(F32)
    vals, idxs = [], []
    for _ in range(PEER_TOPK):
        m = jnp.max(s, axis=0, keepdims=True)
        idx = jnp.min(jnp.where(s == m, iota, float(n)), axis=0, keepdims=True)
        vals.append(m)
        idxs.append(idx)
        s = jnp.where(iota == idx, NEG_INF, s)
    return vals, idxs


def _candidate_positions():
    rows = []
    for a in range(4):
        for b in range(16 if a == 0 else 8):
            rows.append((a, b, (a + 1) * (b + 1) <= PEER_TOPK))
    for b, n_a in ((0, 16), (1, 8), (2, 8)):
        for a in range(n_a):
            rows.append((a, b, a >= 4 and (a + 1) * (b + 1) <= PEER_TOPK))
    return np.array([[a * PEER_TOPK + b if ok else -1.0] for a, b, ok in rows], np.float32)


def _peer_head_topk(qh, keys0, keys1, posb):
    k = PEER_TOPK
    tops = []
    for p, keys in enumerate((keys0, keys1)):
        st = _dot_nt(keys, qh[:, p * PEER_HALF:(p + 1) * PEER_HALF].astype(BF16))
        tops.append(_top16(st, N_KEYS))
    (v0, i0), (v1, i1) = tops
    v0c = jnp.concatenate(v0, axis=0)
    v1c = jnp.concatenate(v1, axis=0)
    cand = jnp.concatenate([v0[0] + v1c, v0[1] + v1c[:8], v0[2] + v1c[:8], v0[3] + v1c[:8],
                            v0c + v1[0], v0c[:8] + v1[1], v0c[:8] + v1[2]], axis=0)
    cand = jnp.where(posb >= 0.0, cand, NEG_INF)
    best, poss = [], []
    for _ in range(k):
        m = jnp.max(cand, axis=0, keepdims=True)
        pos = jnp.min(jnp.where(cand == m, posb, float(k * k)), axis=0, keepdims=True)
        best.append(m)
        poss.append(pos)
        cand = jnp.where(posb == pos, NEG_INF, cand)
    best = jnp.concatenate(best, axis=0)
    posc = jnp.concatenate(poss, axis=0)
    a_rank = jnp.floor(posc * (1.0 / k))
    b_rank = posc - k * a_rank
    ea = jnp.zeros_like(posc)
    eb = jnp.zeros_like(posc)
    for r in range(k):
        ea = jnp.where(a_rank == float(r), i0[r], ea)
        eb = jnp.where(b_rank == float(r), i1[r], eb)
    e = jnp.exp(best - best[0:1, :])
    return ea, eb, e / jnp.sum(e, axis=0, keepdims=True)


RSQRT2 = 1.0 / math.sqrt(2.0)


def _peer_kernel(h2n_ref, wq_ref, keys_ref, pos_ref, u_ref, v_ref, x1_ref, g2_ref, lng_ref, lnb_ref,
                 x2_ref, ids_scr, idt_scr, w_scr, p_scr, acc_scr, h2n_scr, h2_scr, *, tb, te, pitch, nblk):
    i = pl.program_id(0)
    j = pl.program_id(1)
    slot = i % 2
    k = PEER_TOPK
    rows_per_tile = te // N_KEYS
    half = N_KEYS // 2
    hi_mask = jnp.uint32(0xFFFF0000)
    first_row = i == 0
    last_row = i == nblk

    @pl.when(jnp.logical_and(j == 0, jnp.logical_not(first_row)))
    def _():
        h2_scr[...] = h2n_scr[...]
        for n in range(3):
            idt_scr[n] = ids_scr[1 - slot, n].T
        acc_scr[...] = jnp.zeros_like(acc_scr)
        iota = lax.broadcasted_iota(jnp.int32, (N_KEYS, PEER_HEADS * k), 0).astype(F32)

        def tok(t, carry):
            a_row = idt_scr[0, pl.ds(t, 1), :]
            b_row = idt_scr[1, pl.ds(t, 1), :]
            g_row = idt_scr[2, pl.ds(t, 1), :] * RSQRT2
            at = jnp.where(iota == a_row, 1.0, 0.0).astype(BF16)
            gbt = jnp.where(iota == b_row, g_row, 0.0).astype(BF16)
            w = pltpu.bitcast(_dot_nt(at, gbt), jnp.uint32)
            w_scr[pl.ds(t, half, stride=pitch), :] = (w[:half] & hi_mask) | (w[half:] >> 16)
            return carry

        lax.fori_loop(0, tb, tok, 0, unroll=64)

    @pl.when(jnp.logical_and(j == 0, jnp.logical_not(last_row)))
    def _():
        h2n_scr[...] = h2n_ref[...].astype(BF16)

    def retrieve():
        posb = jnp.broadcast_to(pos_ref[...], (pos_ref.shape[0], tb))
        ea, eb, gate = _peer_head_topk(_dot(h2n_scr[...], wq_ref[...]), keys_ref[0, 0], keys_ref[0, 1], posb)
        rows = pl.ds(pl.multiple_of(j * k, k), k)
        ids_scr[slot, 0, rows, :] = ea
        ids_scr[slot, 1, rows, :] = eb
        ids_scr[slot, 2, rows, :] = gate

    def mix():
        a = _dot_nt(h2_scr[...], u_ref[...])
        for r in range(rows_per_tile):
            i1 = j * rows_per_tile + r
            packed = w_scr[pl.ds(pl.multiple_of((i1 % half) * pitch, 8), tb), :]
            shift = jnp.where(i1 >= half, 16, 0).astype(jnp.uint32)
            w = pltpu.bitcast((packed << shift) & hi_mask, F32)
            cs = slice(r * N_KEYS, (r + 1) * N_KEYS)
            ar = a[:, cs]
            p_scr[:, cs] = (w * (ar * (1.0 + lax.erf(ar)))).astype(BF16)
        acc_scr[...] += _dot(p_scr[...], v_ref[...])

    @pl.when(first_row)
    def _():
        retrieve()

    @pl.when(last_row)
    def _():
        mix()

    @pl.when(jnp.logical_not(jnp.logical_or(first_row, last_row)))
    def _():
        retrieve()
        mix()

    @pl.when(jnp.logical_and(j == pl.num_programs(1) - 1, jnp.logical_not(first_row)))
    def _():
        x2_ref[...] = _layer_norm(DEEPNORM_ALPHA * x1_ref[...] + g2_ref[0] * acc_scr[...],
                                  lng_ref[...], lnb_ref[...])


def _peer(h2, wq_bf, keys_bf, u_bf, v_bf, x1, mod_l, ln_g, ln_b, rows_of_tile, tb):
    t, d = x1.shape
    n_e = v_bf.shape[0]
    te = PEER_TE
    assert n_e // te == PEER_HEADS
    hk = PEER_HEADS * PEER_TOPK
    pitch = tb + 8
    nblk = t // tb
    mod3 = mod_l.reshape(8, 1, 6 * d)
    pos = _candidate_positions()

    def cur(i):
        return jnp.maximum(i - 1, 0)

    def full(shape):
        return pl.BlockSpec(shape, lambda i, j: (0,) * len(shape))

    return pl.pallas_call(
        functools.partial(_peer_kernel, tb=tb, te=te, pitch=pitch, nblk=nblk),
        grid=(nblk + 1, PEER_HEADS),
        in_specs=[
            pl.BlockSpec((tb, d), lambda i, j: (jnp.minimum(i, nblk - 1), 0)),
            pl.BlockSpec((d, 2 * PEER_HALF), lambda i, j: (0, j)),
            pl.BlockSpec((1, 2, N_KEYS, PEER_HALF), lambda i, j: (j, 0, 0, 0)),
            full(pos.shape),
            pl.BlockSpec((te, d), lambda i, j: (j, 0)),
            pl.BlockSpec((te, d), lambda i, j: (j, 0)),
            pl.BlockSpec((tb, d), lambda i, j: (cur(i), 0)),
            pl.BlockSpec((1, 1, d), lambda i, j: (rows_of_tile(cur(i)), 0, 5)),
            full((1, d)), full((1, d)),
        ],
        out_specs=pl.BlockSpec((tb, d), lambda i, j: (cur(i), 0)),
        out_shape=jax.ShapeDtypeStruct((t, d), F32),
        scratch_shapes=[pltpu.VMEM((2, 3, hk, tb), F32), pltpu.VMEM((3, tb, hk), F32),
                        pltpu.VMEM((pitch * N_KEYS // 2, N_KEYS), jnp.uint32), pltpu.VMEM((tb, te), BF16),
                        pltpu.VMEM((tb, d), F32), pltpu.VMEM((tb, d), BF16), pltpu.VMEM((tb, d), BF16)],
        compiler_params=pltpu.CompilerParams(dimension_semantics=("arbitrary", "arbitrary"),
                                             vmem_limit_bytes=PEER_VMEM_LIMIT),
        name="peer",
    )(h2, wq_bf, keys_bf, jnp.asarray(pos), u_bf, v_bf, x1, mod3, ln_g.reshape(1, d), ln_b.reshape(1, d))


def _rope_tables(length):
    rows = length // GRID_W
    row = jnp.broadcast_to(jnp.arange(rows, dtype=F32)[:, None], (rows, GRID_W)).reshape(length)
    col = jnp.broadcast_to(jnp.arange(GRID_W, dtype=F32)[None, :], (rows, GRID_W)).reshape(length)
    n_freq = HEAD_DIM // 4
    inv = jnp.power(ROPE_THETA, -jnp.arange(n_freq, dtype=F32) / n_freq)
    ang = jnp.concatenate([row[:, None] * inv, col[:, None] * inv], axis=-1)
    cos, sin = jnp.cos(ang), jnp.sin(ang)
    return jnp.concatenate([cos, cos], axis=-1), jnp.concatenate([-sin, sin], axis=-1)


def kernel(x_prompt, x_sample, cache_attn_k, cache_attn_v, state_hgrn, c, c_ctx, w_mod, b_mod, w_in,
           hgrn_lb_logits, hgrn_norm_g, q_norm_g, k_norm_g, w_branch_rec, w_branch_att, w_out,
           ln1_g, ln1_b, ln2_g, ln2_b, peer_w_query, peer_sub_keys, peer_u, peer_v):
    nb, seq, d = x_prompt.shape
    db, dseq, _ = x_sample.shape
    depth = w_in.shape[0]
    past = cache_attn_k.shape[2]
    t_ctx, t_lat = nb * seq, db * dseq
    t_all = t_ctx + t_lat
    assert d == D_MODEL and 1 + db <= 8

    tm = _tile(math.gcd(t_ctx, dseq), 2048)
    tm_merge = _tile(math.gcd(t_ctx, dseq), 512)
    tb_mix = _tile(math.gcd(t_ctx, dseq), PEER_TB_MIX)

    def rows_of_tile_fn(tile):
        n_ctx = t_ctx // tile
        per = dseq // tile
        return lambda i: jnp.where(i < n_ctx, 0, 1 + (i - n_ctx) // per)

    cond8 = jnp.concatenate([c_ctx[None, :], c, jnp.zeros((8 - 1 - db, d), F32)], axis=0)
    mod = _modulation(cond8, w_mod, b_mod)

    lb = jnp.cumsum(jax.nn.softmax(hgrn_lb_logits.astype(F32), axis=1), axis=1)
    lb = lb - lb[:, :1]

    w_in_bf = w_in.astype(BF16)
    w_rec_bf, w_att_bf, w_out_bf = (w.astype(BF16) for w in (w_branch_rec, w_branch_att, w_out))
    wq_bf = peer_w_query.astype(BF16)
    keys_bf = peer_sub_keys.astype(BF16)
    u_bf = (peer_u * RSQRT2).astype(BF16)
    v_bf = peer_v.astype(BF16)

    cos_t, sin_t = (jnp.concatenate([t, t], axis=-1) for t in _rope_tables(dseq))
    ones_t, zeros_t = jnp.ones((seq, 2 * HEAD_DIM), F32), jnp.zeros((seq, 2 * HEAD_DIM), F32)
    cache_k_t = jnp.transpose(cache_attn_k, (0, 1, 3, 2, 4))
    cache_k_bf = jnp.concatenate([cache_k_t, cache_k_t], axis=-1).astype(BF16)
    cache_v_t = jnp.transpose(cache_attn_v, (0, 1, 3, 2, 4))
    ones_pad = jnp.zeros(cache_v_t.shape, F32).at[..., 0].set(1.0)
    cache_v_aug = jnp.concatenate([cache_v_t, ones_pad], axis=-1).astype(BF16)

    seq_rows = [(i * seq, seq) for i in range(nb)] + [(t_ctx + i * dseq, dseq) for i in range(db)]
    zero_state = jnp.zeros((nb, 2, HGRN_HEADS, HGRN_DK, HGRN_DK), F32)

    x2d = jnp.concatenate([x_prompt.reshape(t_ctx, d), x_sample.reshape(t_lat, d)], axis=0)
    new_k, new_v, new_s = [], [], []
    for l in range(depth):
        z2d = _in_proj(x2d, mod[l], w_in_bf[l], rows_of_tile_fn(tm), tm)

        s0_all = jnp.concatenate([zero_state, state_hgrn[:, l]], axis=0)
        o_f, o_b, s_fin = _hgrn_scan(z2d, lb[:, l], s0_all, seq_rows, HGRN_CHUNK)
        new_s.append(s_fin[:nb])

        q_c, k_c, v_c, k_norm = _qkv_prep(z2d, 0, nb, seq, q_norm_g[l], k_norm_g[l], ones_t, zeros_t, False, True)
        o_att = _attention(q_c, k_c, v_c, None, 0, t_all)
        new_k.append(k_norm.reshape(nb, seq, KV_HEADS, HEAD_DIM))
        new_v.append(z2d[:t_ctx, COL_AV:COL_AV + KV_WIDTH].reshape(nb, seq, KV_HEADS, HEAD_DIM))

        q_s, k_s, v_s = _qkv_prep(z2d, t_ctx, db, dseq, q_norm_g[l], k_norm_g[l], cos_t, sin_t, True, False)
        k_s = jnp.concatenate([k_s, cache_k_bf[:, l]], axis=2)
        v_s = jnp.concatenate([v_s, cache_v_aug[:, l]], axis=2)
        o_att = _attention(q_s, k_s, v_s, o_att, t_ctx, t_all)

        x1, h2 = _merge(o_f, o_b, z2d, o_att, x2d, mod[l], w_rec_bf[l], w_att_bf[l], w_out_bf[l],
                        hgrn_norm_g[l], ln1_g[l], ln1_b[l], rows_of_tile_fn(tm_merge), tm_merge)

        x2d = _peer(h2, wq_bf[l], keys_bf[l], u_bf[l], v_bf[l], x1, mod[l], ln2_g[l], ln2_b[l],
                    rows_of_tile_fn(tb_mix), tb_mix)

    y_prompt = x2d[:t_ctx].reshape(nb, seq, d)
    y_sample = x2d[t_ctx:].reshape(db, dseq, d)
    return (y_prompt, y_sample, jnp.stack(new_k, axis=1), jnp.stack(new_v, axis=1), jnp.stack(new_s, axis=1))
```
